```python
import math
import jax, jax.numpy as jnp
from jax import lax
import numpy as np

D_MODEL = 1024
BATCH = 8
SEQ = 8192
DEPTH = 2
DEC_BATCH = 16
DEC_SEQ = 16
PAST_LEN = 4096

CHUNK = 64
N_MIXERS = 2
N_S5 = (DEPTH + 1) // 2
N_GDN = DEPTH // 2
EPS = 1e-6

S5_WIDTH = D_MODEL
S5_GROUP = 16
S5_GROUPS = S5_WIDTH // S5_GROUP
S5_STATE = 64

GDN_DK = 128
GDN_DV = 128
GDN_QK_HEADS = D_MODEL // 256
GDN_V_HEADS = 2 * GDN_QK_HEADS
GDN_QK_WIDTH = GDN_QK_HEADS * GDN_DK
GDN_V_WIDTH = GDN_V_HEADS * GDN_DV
GDN_CONV = 4
GDN_CONV_CH = 2 * GDN_QK_WIDTH + GDN_V_WIDTH
GDN_IN = GDN_CONV_CH + GDN_V_WIDTH + 2 * GDN_V_HEADS

kernel_name = "s5_gdn_hybrid_stream_step"

F32 = jnp.float32


def rms_norm(x, g):
    xf = x.astype(F32)
    y = xf * lax.rsqrt(jnp.mean(xf * xf, axis=-1, keepdims=True) + EPS)
    return (y * g.astype(F32)).astype(x.dtype)


def l2norm(x):
    return x * lax.rsqrt(jnp.sum(x * x, axis=-1, keepdims=True) + EPS)


def s5_discretize(log_step, lam_re, lam_im, b_re, b_im):
    step = jnp.exp(log_step.astype(F32))[:, None]
    lr, li = lam_re.astype(F32), lam_im.astype(F32)
    mag = jnp.exp(lr * step)
    ar, ai = mag * jnp.cos(li * step), mag * jnp.sin(li * step)
    den = lr * lr + li * li
    xr = ar - 1.0
    nr = (xr * lr + ai * li) / den
    ni = (ai * lr - xr * li) / den
    br, bi = b_re.astype(F32), b_im.astype(F32)
    bbr = nr[..., None] * br - ni[..., None] * bi
    bbi = nr[..., None] * bi + ni[..., None] * br
    return ar, ai, bbr, bbi


def _cplx_combine(e1, e2):
    a1r, a1i, b1r, b1i = e1
    a2r, a2i, b2r, b2i = e2
    return (a2r * a1r - a2i * a1i, a2r * a1i + a2i * a1r,
            a2r * b1r - a2i * b1i + b2r, a2r * b1i + a2i * b1r + b2i)


def s5_scan(u, h_re, h_im, ar, ai, bbr, bbi, c_re, c_im):
    bsz, L, E = u.shape
    T = CHUNK if L % CHUNK == 0 else L
    n = L // T
    ug = u.reshape(bsz, n, T, S5_GROUPS, S5_GROUP).transpose(1, 0, 2, 3, 4)
    a_r = jnp.broadcast_to(ar[None, None], (bsz, T, S5_GROUPS, S5_STATE))
    a_i = jnp.broadcast_to(ai[None, None], (bsz, T, S5_GROUPS, S5_STATE))

    def step(carry, uc):
        hr, hi = carry
        br = jnp.einsum('gpc,btgc->btgp', bbr, uc)
        bi = jnp.einsum('gpc,btgc->btgp', bbi, uc)
        pr, pi, sr, si = lax.associative_scan(_cplx_combine, (a_r, a_i, br, bi), axis=1)
        xr = pr * hr[:, None] - pi * hi[:, None] + sr
        xi = pr * hi[:, None] + pi * hr[:, None] + si
        y = jnp.einsum('gcp,btgp->btgc', c_re, xr) - jnp.einsum('gcp,btgp->btgc', c_im, xi)
        return (xr[:, -1], xi[:, -1]), y

    (hr, hi), ys = lax.scan(step, (h_re, h_im), ug)
    y = ys.transpose(1, 0, 2, 3, 4).reshape(bsz, L, E)
    return y, hr, hi


def s5_branch(h, h_re, h_im, w_in, log_step, lam_re, lam_im, b_re, b_im,
              c_re, c_im, d, w_glu, b_glu, w_out):
    proj = h @ w_in
    u, z = jnp.split(proj, 2, axis=-1)
    uf = u.astype(F32)
    ar, ai, bbr, bbi = s5_discretize(log_step, lam_re, lam_im, b_re, b_im)
    y, hr, hi = s5_scan(uf, h_re.astype(F32), h_im.astype(F32), ar, ai, bbr, bbi,
                        c_re.astype(F32), c_im.astype(F32))
    y = jax.nn.gelu(y + d.astype(F32) * uf)
    y = y * jax.nn.sigmoid(y @ w_glu.astype(F32) + b_glu.astype(F32))
    y = y * jax.nn.silu(z.astype(F32))
    return y.astype(h.dtype) @ w_out, hr, hi


def gated_delta_rule(q, k, v, beta, g, S0):
    bsz, L, H, _ = q.shape
    T = CHUNK if L % CHUNK == 0 else L
    n = L // T

    def chunks(t):
        t = t.reshape((bsz, n, T, H) + t.shape[3:])
        return jnp.moveaxis(t, (1, 3), (0, 2))

    causal = jnp.tril(jnp.ones((T, T), bool))
    strict = jnp.tril(jnp.ones((T, T), bool), -1)
    eye = jnp.eye(T, dtype=F32)

    def step(S, inp):
        qc, kc, vc, bc, gc = inp
        gcum = jnp.cumsum(gc, axis=-1)
        decay = jnp.exp(jnp.where(causal, gcum[..., :, None] - gcum[..., None, :], -jnp.inf))
        kb = kc * bc[..., None]
        m = jnp.where(strict, jnp.einsum('bhik,bhjk->bhij', kb, kc) * decay, 0.0)
        rhs = jnp.concatenate([vc * bc[..., None], kb * jnp.exp(gcum)[..., None]], axis=-1)
        sol = lax.linalg.triangular_solve(eye + m, rhs, left_side=True, lower=True,
                                          unit_diagonal=True)
        u, w = sol[..., :GDN_DV], sol[..., GDN_DV:]
        v_new = u - jnp.einsum('bhtk,bhkv->bhtv', w, S)
        qk = jnp.einsum('bhik,bhjk->bhij', qc, kc) * decay
        o = (jnp.einsum('bhtk,bhkv->bhtv', qc * jnp.exp(gcum)[..., None], S)
             + jnp.einsum('bhij,bhjv->bhiv', qk, v_new))
        g_last = gcum[..., -1]
        S_new = (S * jnp.exp(g_last)[..., None, None]
                 + jnp.einsum('bhtk,bhtv->bhkv',
                              kc * jnp.exp(g_last[..., None] - gcum)[..., None], v_new))
        return S_new, o

    S, os_ = lax.scan(step, S0, (chunks(q), chunks(k), chunks(v), chunks(beta), chunks(g)))
    o = jnp.moveaxis(os_, (0, 2), (1, 3)).reshape(bsz, L, H, GDN_DV)
    return o, S


def gdn_branch(h, conv_hist, S0, w_in, conv_w, a_log, dt_bias, norm_g, w_out):
    bsz, L, _ = h.shape
    proj = h @ w_in
    o1 = GDN_CONV_CH
    o2 = o1 + GDN_V_WIDTH
    o3 = o2 + GDN_V_HEADS
    qkv, z, b_raw, a_raw = proj[..., :o1], proj[..., o1:o2], proj[..., o2:o3], proj[..., o3:]
    full = jnp.concatenate([conv_hist.astype(qkv.dtype), qkv], axis=1)
    conv = full[:, 0:L] * conv_w[0]
    for j in range(1, GDN_CONV):
        conv = conv + full[:, j:j + L] * conv_w[j]
    new_hist = full[:, L:]
    act = jax.nn.silu(conv.astype(F32))
    q = act[..., :GDN_QK_WIDTH].reshape(bsz, L, GDN_QK_HEADS, GDN_DK)
    k = act[..., GDN_QK_WIDTH:2 * GDN_QK_WIDTH].reshape(bsz, L, GDN_QK_HEADS, GDN_DK)
    v = act[..., 2 * GDN_QK_WIDTH:].reshape(bsz, L, GDN_V_HEADS, GDN_DV)
    rep = GDN_V_HEADS // GDN_QK_HEADS
    q = jnp.repeat(l2norm(q) * (GDN_DK ** -0.5), rep, axis=2)
    k = jnp.repeat(l2norm(k), rep, axis=2)
    beta = jax.nn.sigmoid(b_raw.astype(F32))
    g = -jnp.exp(a_log.astype(F32)) * jax.nn.softplus(a_raw.astype(F32) + dt_bias.astype(F32))
    o, S = gated_delta_rule(q, k, v, beta, g, S0.astype(F32))
    o = rms_norm(o, norm_g) * jax.nn.silu(z.astype(F32).reshape(bsz, L, GDN_V_HEADS, GDN_DV))
    out = o.reshape(bsz, L, GDN_V_WIDTH).astype(h.dtype) @ w_out
    return out, new_hist, S


def trunk(x, c, s5_re0, s5_im0, gdn_s0, gdn_conv0, weights):
    (norm_g, w_ada, b_ada, s5_w_in, s5_log_step, s5_lambda_re, s5_lambda_im, s5_b_re, s5_b_im,
     s5_c_re, s5_c_im, s5_d, s5_w_glu, s5_b_glu, s5_w_out, gdn_w_in, gdn_conv_w, gdn_a_log,
     gdn_dt_bias, gdn_norm_g, gdn_w_out, final_g) = weights
    s5_re_out, s5_im_out, gdn_s_out, gdn_conv_out = [], [], [], []
    cs = jax.nn.silu(c)
    for i in range(DEPTH):
        j = i // N_MIXERS
        mod = cs @ w_ada[i] + b_ada[i]
        shift, scale, gate = jnp.split(mod[:, None, :], 3, axis=-1)
        h = rms_norm(x, norm_g[i]) * (1.0 + scale) + shift
        if i % N_MIXERS == 0:
            out, hr, hi = s5_branch(h, s5_re0[j], s5_im0[j], s5_w_in[j], s5_log_step[j],
                                    s5_lambda_re[j], s5_lambda_im[j], s5_b_re[j], s5_b_im[j],
                                    s5_c_re[j], s5_c_im[j], s5_d[j], s5_w_glu[j], s5_b_glu[j],
                                    s5_w_out[j])
            s5_re_out.append(hr)
            s5_im_out.append(hi)
        else:
            out, hist, S = gdn_branch(h, gdn_conv0[j], gdn_s0[j], gdn_w_in[j], gdn_conv_w[j],
                                      gdn_a_log[j], gdn_dt_bias[j], gdn_norm_g[j], gdn_w_out[j])
            gdn_conv_out.append(hist)
            gdn_s_out.append(S)
        x = x + gate * out
    y = rms_norm(x, final_g)
    return (y, jnp.stack(s5_re_out), jnp.stack(s5_im_out), jnp.stack(gdn_s_out),
            jnp.stack(gdn_conv_out))


def setup_inputs(seed: int = 0) -> dict:
    key = jax.random.key(seed)
    ks = iter(jax.random.split(key, 40))

    def nrm(shape, s):
        return jax.random.normal(next(ks), shape, F32) * s

    def unif(shape, lo, hi):
        return jax.random.uniform(next(ks), shape, F32, minval=lo, maxval=hi)

    D, E, G, P = D_MODEL, S5_WIDTH, S5_GROUPS, S5_STATE
    lam_im = jnp.broadcast_to(jnp.pi * jnp.arange(P, dtype=F32), (N_S5, G, P))
    dt = jnp.exp(unif((N_GDN, GDN_V_HEADS), math.log(1e-3), math.log(1e-1)))
    return {
        "x_prompt": nrm((BATCH, SEQ, D), 1.0),
        "x_sample": nrm((DEC_BATCH, DEC_SEQ, D), 1.0),
        "c_prompt": nrm((BATCH, D), 1.0),
        "c_sample": nrm((DEC_BATCH, D), 1.0),
        "state_s5_re": nrm((N_S5, DEC_BATCH, G, P), 0.1),
        "state_s5_im": nrm((N_S5, DEC_BATCH, G, P), 0.1),
        "state_gdn": nrm((N_GDN, DEC_BATCH, GDN_V_HEADS, GDN_DK, GDN_DV), 0.1),
        "state_gdn_conv": nrm((N_GDN, DEC_BATCH, GDN_CONV - 1, GDN_CONV_CH), 1.0),
        "norm_g": 1.0 + nrm((DEPTH, D), 0.02),
        "w_ada": nrm((DEPTH, D, 3 * D), 0.5 * D ** -0.5),
        "b_ada": nrm((DEPTH, 3 * D), 0.02),
        "s5_w_in": nrm((N_S5, D, 2 * E), D ** -0.5),
        "s5_log_step": unif((N_S5, G), math.log(1e-3), math.log(1e-1)),
        "s5_lambda_re": -0.5 + nrm((N_S5, G, P), 0.01),
        "s5_lambda_im": lam_im + nrm((N_S5, G, P), 0.01),
        "s5_b_re": nrm((N_S5, G, P, S5_GROUP), (2 * S5_GROUP) ** -0.5),
        "s5_b_im": nrm((N_S5, G, P, S5_GROUP), (2 * S5_GROUP) ** -0.5),
        "s5_c_re": nrm((N_S5, G, S5_GROUP, P), 0.5),
        "s5_c_im": nrm((N_S5, G, S5_GROUP, P), 0.5),
        "s5_d": nrm((N_S5, E), 1.0),
        "s5_w_glu": nrm((N_S5, E, E), E ** -0.5),
        "s5_b_glu": nrm((N_S5, E), 0.02),
        "s5_w_out": nrm((N_S5, E, D), E ** -0.5),
        "gdn_w_in": nrm((N_GDN, D, GDN_IN), D ** -0.5),
        "gdn_conv_w": nrm((N_GDN, GDN_CONV, GDN_CONV_CH), 0.5),
        "gdn_a_log": jnp.log(unif((N_GDN, GDN_V_HEADS), 1.0, 16.0)),
        "gdn_dt_bias": dt + jnp.log(-jnp.expm1(-dt)),
        "gdn_norm_g": 1.0 + nrm((N_GDN, GDN_DV), 0.02),
        "gdn_w_out": nrm((N_GDN, GDN_V_WIDTH, D), GDN_V_WIDTH ** -0.5),
        "final_g": 1.0 + nrm((D,), 0.02),
    }


def reference(x_prompt, x_sample, c_prompt, c_sample, state_s5_re, state_s5_im, state_gdn,
              state_gdn_conv, norm_g, w_ada, b_ada, s5_w_in, s5_log_step, s5_lambda_re,
              s5_lambda_im, s5_b_re, s5_b_im, s5_c_re, s5_c_im, s5_d, s5_w_glu, s5_b_glu,
              s5_w_out, gdn_w_in, gdn_conv_w, gdn_a_log, gdn_dt_bias, gdn_norm_g, gdn_w_out,
              final_g):
    weights = (norm_g, w_ada, b_ada, s5_w_in, s5_log_step, s5_lambda_re, s5_lambda_im, s5_b_re,
               s5_b_im, s5_c_re, s5_c_im, s5_d, s5_w_glu, s5_b_glu, s5_w_out, gdn_w_in,
               gdn_conv_w, gdn_a_log, gdn_dt_bias, gdn_norm_g, gdn_w_out, final_g)
    bp = x_prompt.shape[0]
    z_s5 = jnp.zeros((N_S5, bp, S5_GROUPS, S5_STATE), F32)
    z_gdn = jnp.zeros((N_GDN, bp, GDN_V_HEADS, GDN_DK, GDN_DV), F32)
    z_conv = jnp.zeros((N_GDN, bp, GDN_CONV - 1, GDN_CONV_CH), x_prompt.dtype)
    y_prompt, s5r_p, s5i_p, gdn_p, conv_p = trunk(x_prompt, c_prompt, z_s5, z_s5, z_gdn,
                                                  z_conv, weights)
    y_sample, s5r_s, s5i_s, gdn_s, conv_s = trunk(x_sample, c_sample, state_s5_re, state_s5_im,
                                                  state_gdn, state_gdn_conv, weights)
    return (y_prompt, y_sample, s5r_p, s5i_p, gdn_p, conv_p, s5r_s, s5i_s, gdn_s, conv_s)
```

```python
import functools
import math

import jax
import jax.numpy as jnp
from jax import lax
from jax.experimental import pallas as pl
from jax.experimental.pallas import tpu as pltpu

F32 = jnp.float32
BF16 = jnp.bfloat16
EPS = 1e-6

S5_GROUP = 16
S5_STATE = 64
S5_BLOCK_GROUPS = 8
GDN_DK = 128
GDN_DV = 128
GDN_CONV = 4
GDN_CHUNK = 64
INV_BLOCK = 16
HIST_ROWS = 8
LANES = 128
VMEM_LIMIT = 56 * 1024 * 1024
HI = lax.Precision.HIGHEST
NT_DIMS = (((1,), (1,)), ((), ()))


def _params(*sem):
    return pltpu.CompilerParams(dimension_semantics=sem, vmem_limit_bytes=VMEM_LIMIT)


def _dot(a, b):
    return jnp.dot(a.astype(BF16), b.astype(BF16), preferred_element_type=F32)


def _row_tile(L):
    return min(512, L)


def _ada_kernel(c_ref, w_ref, b_ref, o_ref):
    c = c_ref[...]
    o_ref[0] = _dot(jax.nn.silu(c), w_ref[0]) + b_ref[0]


def _ada_mod(c_all, w_ada, b_ada):
    depth, d, d3 = w_ada.shape
    r = c_all.shape[0]
    tn = 768
    return pl.pallas_call(
        _ada_kernel,
        grid=(depth, d3 // tn),
        in_specs=[
            pl.BlockSpec((r, d), lambda i, j: (0, 0)),
            pl.BlockSpec((1, d, tn), lambda i, j: (i, 0, j)),
            pl.BlockSpec((1, 1, tn), lambda i, j: (i, 0, j)),
        ],
        out_specs=pl.BlockSpec((1, r, tn), lambda i, j: (i, 0, j)),
        out_shape=jax.ShapeDtypeStruct((depth, r, d3), F32),
        compiler_params=_params("parallel", "parallel"),
        name="ada_mod",
    )(c_all, w_ada.astype(BF16), b_ada.reshape(depth, 1, d3))


def _norm_mod(x, g, m):
    h = x * lax.rsqrt(jnp.mean(x * x, axis=-1, keepdims=True) + EPS) * g
    return h * (1.0 + m[1:2]) + m[0:1]


def _s5_in_kernel(x_ref, mod_ref, g_ref, w_ref, u_ref, z_ref):
    e = u_ref.shape[-1]
    h = _norm_mod(x_ref[0], g_ref[...], mod_ref[0])
    p = _dot(h, w_ref[...])
    u_ref[...] = p[:, :e]
    z_ref[0] = p[:, e:]


def _s5_in(x, mod, g, w_in):
    B, L, D = x.shape
    E = w_in.shape[1] // 2
    tm = _row_tile(L)
    return pl.pallas_call(
        _s5_in_kernel,
        grid=(B, L // tm),
        in_specs=[
            pl.BlockSpec((1, tm, D), lambda b, i: (b, i, 0)),
            pl.BlockSpec((1, 3, D), lambda b, i: (b, 0, 0)),
            pl.BlockSpec((1, D), lambda b, i: (0, 0)),
            pl.BlockSpec((D, 2 * E), lambda b, i: (0, 0)),
        ],
        out_specs=[
            pl.BlockSpec((tm, E), lambda b, i: (i, b)),
            pl.BlockSpec((1, tm, E), lambda b, i: (b, i, 0)),
        ],
        out_shape=[
            jax.ShapeDtypeStruct((L, B * E), F32),
            jax.ShapeDtypeStruct((B, L, E), F32),
        ],
        compiler_params=_params("parallel", "parallel"),
        name="s5_in",
    )(x, mod, g.reshape(1, D), w_in.astype(BF16))


def _s5_disc_kernel(ls_ref, lr_ref, li_ref, br_ref, bi_ref, ar_ref, ai_ref, bbr_ref, bbi_ref):
    step = jnp.exp(ls_ref[...])
    lr = lr_ref[...]
    li = li_ref[...]
    mag = jnp.exp(lr * step)
    ar = mag * jnp.cos(li * step)
    ai = mag * jnp.sin(li * step)
    den = lr * lr + li * li
    xr = ar - 1.0
    nr = (xr * lr + ai * li) / den
    ni = (ai * lr - xr * li) / den
    br = br_ref[...]
    bi = bi_ref[...]
    ar_ref[...] = ar
    ai_ref[...] = ai
    bbr_ref[...] = nr * br - ni * bi
    bbi_ref[...] = nr * bi + ni * br


def _s5_disc(log_step, lam_re, lam_im, b_re, b_im):
    G, P = lam_re.shape
    n = G * P
    c = b_re.shape[-1]
    ls = jnp.broadcast_to(log_step[:, None], (G, P)).reshape(1, n)
    brT = b_re.transpose(2, 0, 1).reshape(c, n)
    biT = b_im.transpose(2, 0, 1).reshape(c, n)
    vec = jax.ShapeDtypeStruct((1, n), F32)
    mat = jax.ShapeDtypeStruct((c, n), F32)
    return pl.pallas_call(
        _s5_disc_kernel,
        out_shape=[vec, vec, mat, mat],
        name="s5_disc",
    )(ls, lam_re.reshape(1, n), lam_im.reshape(1, n), brT, biT)


def _s5_block_weights(bbrT, bbiT, c_re, c_im):
    nb = S5_BLOCK_GROUPS
    G = c_re.shape[0]
    nblk = G // nb
    eye = jnp.eye(nb, dtype=F32)

    def bdiag_b(bbT):
        t = bbT.reshape(S5_GROUP, nblk, nb, S5_STATE)
        w = jnp.einsum("cjhp,gh->jgchp", t, eye)
        return w.reshape(nblk, nb * S5_GROUP, nb * S5_STATE)

    def bdiag_c(c):
        t = c.reshape(nblk, nb, S5_GROUP, S5_STATE)
        w = jnp.einsum("jgop,gh->jhpgo", t, eye)
        return w.reshape(nblk, nb * S5_STATE, nb * S5_GROUP)

    wb = jnp.concatenate([bdiag_b(bbrT), bdiag_b(bbiT)], axis=2).astype(BF16)
    wc = jnp.concatenate([bdiag_c(c_re), -bdiag_c(c_im)], axis=1).astype(BF16)
    return wb, wc


def _s5_scan_kernel(u_ref, ar_ref, ai_ref, wb_ref, wc_ref, hr0_ref, hi0_ref,
                    y_ref, hr_ref, hi_ref, bu_scr, xr_scr, xi_scr, *, steps, batch):
    i = pl.program_id(0)
    nblk = wb_ref.shape[0]
    half = wb_ref.shape[2] // 2
    ulanes = wb_ref.shape[1]

    @pl.when(i == 0)
    def _():
        xr_scr[...] = hr0_ref[...]
        xi_scr[...] = hi0_ref[...]

    for j in range(nblk):
        sl = slice(j * half, (j + 1) * half)
        bu_scr[...] = _dot(u_ref[:, j * ulanes:(j + 1) * ulanes], wb_ref[j])
        ar = jnp.broadcast_to(ar_ref[:, sl], (batch, half))
        ai = jnp.broadcast_to(ai_ref[:, sl], (batch, half))

        def body(t, carry):
            xr, xi = carry
            r0 = pl.multiple_of(t * batch, batch)
            bur = bu_scr[pl.ds(r0, batch), 0:half]
            bui = bu_scr[pl.ds(r0, batch), half:2 * half]
            nxr = ar * xr - ai * xi + bur
            nxi = ar * xi + ai * xr + bui
            bu_scr[pl.ds(r0, batch), 0:half] = nxr
            bu_scr[pl.ds(r0, batch), half:2 * half] = nxi
            return nxr, nxi

        xr, xi = lax.fori_loop(0, steps, body, (xr_scr[:, sl], xi_scr[:, sl]), unroll=8)
        xr_scr[:, sl] = xr
        xi_scr[:, sl] = xi
        y_ref[:, j * ulanes:(j + 1) * ulanes] = _dot(bu_scr[...], wc_ref[j])

    @pl.when(i == pl.num_programs(0) - 1)
    def _():
        hr_ref[...] = xr_scr[...]
        hi_ref[...] = xi_scr[...]


def _s5_scan(u_tm, ar, ai, wb, wc, hr0, hi0, B, L):
    E = u_tm.shape[1]
    n = ar.shape[1]
    steps = min(128, L)
    rows = steps * B
    const = lambda i: (0, 0)
    return pl.pallas_call(
        functools.partial(_s5_scan_kernel, steps=steps, batch=B),
        grid=(L // steps,),
        in_specs=[
            pl.BlockSpec((rows, E), lambda i: (i, 0)),
            pl.BlockSpec((1, n), const),
            pl.BlockSpec((1, n), const),
            pl.BlockSpec(wb.shape, lambda i: (0, 0, 0)),
            pl.BlockSpec(wc.shape, lambda i: (0, 0, 0)),
            pl.BlockSpec((B, n), const),
            pl.BlockSpec((B, n), const),
        ],
        out_specs=[
            pl.BlockSpec((rows, E), lambda i: (i, 0)),
            pl.BlockSpec((B, n), const),
            pl.BlockSpec((B, n), const),
        ],
        out_shape=[
            jax.ShapeDtypeStruct((L * B, E), F32),
            jax.ShapeDtypeStruct((B, n), F32),
            jax.ShapeDtypeStruct((B, n), F32),
        ],
        scratch_shapes=[
            pltpu.VMEM((rows, wb.shape[2]), F32),
            pltpu.VMEM((B, n), F32),
            pltpu.VMEM((B, n), F32),
        ],
        compiler_params=_params("arbitrary"),
        name="s5_scan",
    )(u_tm, ar, ai, wb, wc, hr0, hi0)


def _s5_out_kernel(y_ref, u_ref, z_ref, x_ref, mod_ref, d_ref, wg_ref, bg_ref, wo_ref, o_ref):
    y = jax.nn.gelu(y_ref[...] + d_ref[...] * u_ref[...])
    y = y * jax.nn.sigmoid(_dot(y, wg_ref[...]) + bg_ref[...])
    y = y * jax.nn.silu(z_ref[0])
    o_ref[0] = x_ref[0] + mod_ref[0][2:3] * _dot(y, wo_ref[...])


def _s5_out(y_tm, u_tm, z, x, mod, d, w_glu, b_glu, w_out):
    B, L, D = x.shape
    E = z.shape[-1]
    tm = _row_tile(L)
    const = lambda b, i: (0, 0)
    return pl.pallas_call(
        _s5_out_kernel,
        grid=(B, L // tm),
        in_specs=[
            pl.BlockSpec((tm, E), lambda b, i: (i, b)),
            pl.BlockSpec((tm, E), lambda b, i: (i, b)),
            pl.BlockSpec((1, tm, E), lambda b, i: (b, i, 0)),
            pl.BlockSpec((1, tm, D), lambda b, i: (b, i, 0)),
            pl.BlockSpec((1, 3, D), lambda b, i: (b, 0, 0)),
            pl.BlockSpec((1, E), const),
            pl.BlockSpec((E, E), const),
            pl.BlockSpec((1, E), const),
            pl.BlockSpec((E, D), const),
        ],
        out_specs=pl.BlockSpec((1, tm, D), lambda b, i: (b, i, 0)),
        out_shape=jax.ShapeDtypeStruct((B, L, D), F32),
        compiler_params=_params("parallel", "parallel"),
        name="s5_out",
    )(y_tm, u_tm, z, x, mod, d.reshape(1, E), w_glu.astype(BF16), b_glu.reshape(1, E),
      w_out.astype(BF16))


def _gdn_in_kernel(x_ref, mod_ref, g_ref, wqkv_ref, wz_ref, wba_ref, alog_ref, dtb_ref,
                   qkv_ref, z_ref, bg_ref, *, heads):
    h = _norm_mod(x_ref[0], g_ref[...], mod_ref[0]).astype(BF16)
    qkv_ref[0] = jnp.dot(h, wqkv_ref[...], preferred_element_type=F32)
    z_ref[0] = jnp.dot(h, wz_ref[...], preferred_element_type=F32)
    ba = jnp.dot(h, wba_ref[...], preferred_element_type=F32)
    beta = jax.nn.sigmoid(ba)
    a = ba + dtb_ref[...]
    softplus = jnp.maximum(a, 0.0) + jnp.log1p(jnp.exp(-jnp.abs(a)))
    g = -jnp.exp(alog_ref[...]) * softplus
    lane = lax.broadcasted_iota(jnp.int32, ba.shape, 1)
    bg_ref[0] = jnp.where(lane < heads, beta, g)


def _gdn_in(x, mod, g, w_in, a_log, dt_bias, conv_ch, v_width):
    B, L, D = x.shape
    H = a_log.shape[0]
    tm = _row_tile(L)
    o1, o2 = conv_ch, conv_ch + v_width
    wba = jnp.zeros((D, LANES), F32).at[:, :2 * H].set(w_in[:, o2:o2 + 2 * H])
    alog = jnp.zeros((1, LANES), F32).at[0, H:2 * H].set(a_log)
    dtb = jnp.zeros((1, LANES), F32).at[0, H:2 * H].set(dt_bias)
    const = lambda b, i: (0, 0)
    return pl.pallas_call(
        functools.partial(_gdn_in_kernel, heads=H),
        grid=(B, L // tm),
        in_specs=[
            pl.BlockSpec((1, tm, D), lambda b, i: (b, i, 0)),
            pl.BlockSpec((1, 3, D), lambda b, i: (b, 0, 0)),
            pl.BlockSpec((1, D), const),
            pl.BlockSpec((D, conv_ch), const),
            pl.BlockSpec((D, v_width), const),
            pl.BlockSpec((D, LANES), const),
            pl.BlockSpec((1, LANES), const),
            pl.BlockSpec((1, LANES), const),
        ],
        out_specs=[
            pl.BlockSpec((1, tm, conv_ch), lambda b, i: (b, i, 0)),
            pl.BlockSpec((1, tm, v_width), lambda b, i: (b, i, 0)),
            pl.BlockSpec((1, tm, LANES), lambda b, i: (b, i, 0)),
        ],
        out_shape=[
            jax.ShapeDtypeStruct((B, L, conv_ch), F32),
            jax.ShapeDtypeStruct((B, L, v_width), F32),
            jax.ShapeDtypeStruct((B, L, LANES), F32),
        ],
        compiler_params=_params("parallel", "parallel"),
        name="gdn_in",
    )(x, mod, g.reshape(1, D), w_in[:, :o1].astype(BF16), w_in[:, o1:o2].astype(BF16),
      wba.astype(BF16), alog, dtb)


def _l2norm(x):
    return x * lax.rsqrt(jnp.sum(x * x, axis=-1, keepdims=True) + EPS)


def _gdn_core_kernel(qkv_ref, bg_ref, cw_ref, hist_ref, s0_ref, o_ref, sout_ref,
                     ext_scr, s_scr, *, T, qk_heads, v_heads):
    c = pl.program_id(1)
    qk_width = qk_heads * GDN_DK
    rep = v_heads // qk_heads

    @pl.when(c == 0)
    def _():
        s_scr[...] = s0_ref[0]
        ext_scr[0:HIST_ROWS, :] = hist_ref[0]

    ext_scr[HIST_ROWS:HIST_ROWS + T, :] = qkv_ref[0]
    cw = cw_ref[...]
    base = HIST_ROWS - (GDN_CONV - 1)
    conv = ext_scr[base:base + T, :] * cw[0:1]
    for j in range(1, GDN_CONV):
        conv = conv + ext_scr[base + j:base + j + T, :] * cw[j:j + 1]
    ext_scr[0:HIST_ROWS, :] = ext_scr[T:T + HIST_ROWS, :]
    act = jax.nn.silu(conv)

    bg = bg_ref[0]
    row = lax.broadcasted_iota(jnp.int32, (T, T), 0)
    col = lax.broadcasted_iota(jnp.int32, (T, T), 1)
    causal = row >= col
    strict = row > col
    gc = jnp.dot(causal.astype(F32), bg, precision=HI, preferred_element_type=F32)
    eye = (lax.broadcasted_iota(jnp.int32, (LANES, LANES), 0)
           == lax.broadcasted_iota(jnp.int32, (LANES, LANES), 1))
    gct = lax.dot_general(eye.astype(F32), gc, NT_DIMS, precision=HI, preferred_element_type=F32)
    sh = int(math.log2(INV_BLOCK))
    diag_blk = (row >> sh) == (col >> sh)
    merges = []
    while (1 << sh) < T:
        merges.append(((row >> (sh + 1)) == (col >> (sh + 1))) & ((row >> sh) > (col >> sh)))
        sh += 1

    for hq in range(qk_heads):
        q = _l2norm(act[:, hq * GDN_DK:(hq + 1) * GDN_DK]) * (GDN_DK ** -0.5)
        k = _l2norm(act[:, qk_width + hq * GDN_DK:qk_width + (hq + 1) * GDN_DK])
        kb = k.astype(BF16)
        kk = lax.dot_general(kb, kb, NT_DIMS, preferred_element_type=F32)
        qk = lax.dot_general(q.astype(BF16), kb, NT_DIMS, preferred_element_type=F32)
        for r in range(rep):
            h = hq * rep + r
            v = act[:, 2 * qk_width + h * GDN_DV:2 * qk_width + (h + 1) * GDN_DV]
            beta = bg[:, h:h + 1]
            gcol = gc[:, v_heads + h:v_heads + h + 1]
            grow = gct[v_heads + h:v_heads + h + 1, :]
            decay = jnp.where(causal, jnp.exp(jnp.where(causal, gcol - grow, 0.0)), 0.0)
            egc = jnp.exp(gcol)
            m = jnp.where(strict, (beta * kk) * decay, 0.0)
            p = jnp.where(diag_blk, -m, 0.0)
            e = p
            for _ in range(int(math.log2(INV_BLOCK)) - 1):
                p = _dot(p, p)
                e = e + p + _dot(e, p)
            for blk in merges:
                m21 = jnp.where(blk, m, 0.0)
                y = m21 + _dot(m21, e)
                e = e - (y + _dot(e, y))
            rhs = jnp.concatenate([v * beta, k * (beta * egc)], axis=1)
            sol = rhs + _dot(e, rhs)
            u = sol[:, :GDN_DV]
            w = sol[:, GDN_DV:]
            s = s_scr[h]
            v_new = u - _dot(w, s)
            o_ref[0, :, h * GDN_DV:(h + 1) * GDN_DV] = _dot(q * egc, s) + _dot(qk * decay, v_new)
            g_last = grow[:, T - 1:T]
            kd = k * jnp.exp(g_last - gcol)
            s_scr[h] = s * jnp.exp(g_last) + lax.dot_general(
                kd.astype(BF16), v_new.astype(BF16), (((0,), (0,)), ((), ())),
                preferred_element_type=F32)

    @pl.when(c == pl.num_programs(1) - 1)
    def _():
        sout_ref[0] = s_scr[...]


def _gdn_core(qkv, bg, conv_w, hist, s0):
    B, L, C = qkv.shape
    H = s0.shape[1]
    T = GDN_CHUNK if L % GDN_CHUNK == 0 else L
    hist8 = jnp.zeros((B, HIST_ROWS, C), F32).at[:, HIST_ROWS - (GDN_CONV - 1):, :].set(hist)
    return pl.pallas_call(
        functools.partial(_gdn_core_kernel, T=T, qk_heads=H // 2, v_heads=H),
        grid=(B, L // T),
        in_specs=[
            pl.BlockSpec((1, T, C), lambda b, c: (b, c, 0)),
            pl.BlockSpec((1, T, LANES), lambda b, c: (b, c, 0)),
            pl.BlockSpec((GDN_CONV, C), lambda b, c: (0, 0)),
            pl.BlockSpec((1, HIST_ROWS, C), lambda b, c: (b, 0, 0)),
            pl.BlockSpec((1, H, GDN_DK, GDN_DV), lambda b, c: (b, 0, 0, 0)),
        ],
        out_specs=[
            pl.BlockSpec((1, T, H * GDN_DV), lambda b, c: (b, c, 0)),
            pl.BlockSpec((1, H, GDN_DK, GDN_DV), lambda b, c: (b, 0, 0, 0)),
        ],
        out_shape=[
            jax.ShapeDtypeStruct((B, L, H * GDN_DV), F32),
            jax.ShapeDtypeStruct((B, H, GDN_DK, GDN_DV), F32),
        ],
        scratch_shapes=[
            pltpu.VMEM((HIST_ROWS + T, C), F32),
            pltpu.VMEM((H, GDN_DK, GDN_DV), F32),
        ],
        compiler_params=_params("parallel", "arbitrary"),
        name="gdn_core",
    )(qkv, bg, conv_w, hist8, s0)


def _gdn_out_kernel(o_ref, z_ref, x_ref, mod_ref, ng_ref, wo_ref, fg_ref, y_ref, *, heads):
    o = o_ref[0]
    parts = []
    for h in range(heads):
        oh = o[:, h * GDN_DV:(h + 1) * GDN_DV]
        parts.append(oh * lax.rsqrt(jnp.mean(oh * oh, axis=-1, keepdims=True) + EPS))
    on = jnp.concatenate(parts, axis=1) * ng_ref[...]
    out = _dot(on * jax.nn.silu(z_ref[0]), wo_ref[...])
    x2 = x_ref[0] + mod_ref[0][2:3] * out
    y_ref[0] = x2 * lax.rsqrt(jnp.mean(x2 * x2, axis=-1, keepdims=True) + EPS) * fg_ref[...]


def _gdn_out(o, z, x, mod, norm_g, w_out, final_g):
    B, L, D = x.shape
    V = o.shape[-1]
    H = V // GDN_DV
    tm = _row_tile(L)
    const = lambda b, i: (0, 0)
    return pl.pallas_call(
        functools.partial(_gdn_out_kernel, heads=H),
        grid=(B, L // tm),
        in_specs=[
            pl.BlockSpec((1, tm, V), lambda b, i: (b, i, 0)),
            pl.BlockSpec((1, tm, V), lambda b, i: (b, i, 0)),
            pl.BlockSpec((1, tm, D), lambda b, i: (b, i, 0)),
            pl.BlockSpec((1, 3, D), lambda b, i: (b, 0, 0)),
            pl.BlockSpec((1, V), const),
            pl.BlockSpec((V, D), const),
            pl.BlockSpec((1, D), const),
        ],
        out_specs=pl.BlockSpec((1, tm, D), lambda b, i: (b, i, 0)),
        out_shape=jax.ShapeDtypeStruct((B, L, D), F32),
        compiler_params=_params("parallel", "parallel"),
        name="gdn_out",
    )(o, z, x, mod, jnp.tile(norm_g, H).reshape(1, V), w_out.astype(BF16), final_g.reshape(1, D))


def _trunk(x, mod, s5_re0, s5_im0, gdn_s0, gdn_conv0, w):
    B, L, D = x.shape
    G, P = w["s5_lambda_re"].shape[1:]
    n = G * P

    u_tm, z = _s5_in(x, mod[0], w["norm_g"][0], w["s5_w_in"][0])
    E = z.shape[-1]
    ar, ai, bbrT, bbiT = _s5_disc(w["s5_log_step"][0], w["s5_lambda_re"][0], w["s5_lambda_im"][0],
                                  w["s5_b_re"][0], w["s5_b_im"][0])
    wb, wc = _s5_block_weights(bbrT, bbiT, w["s5_c_re"][0], w["s5_c_im"][0])
    y_tm, hr, hi = _s5_scan(u_tm.reshape(L * B, E), ar, ai, wb, wc,
                            s5_re0[0].reshape(B, n), s5_im0[0].reshape(B, n), B, L)
    x1 = _s5_out(y_tm.reshape(L, B * E), u_tm, z, x, mod[0], w["s5_d"][0], w["s5_w_glu"][0],
                 w["s5_b_glu"][0], w["s5_w_out"][0])

    conv_ch = w["gdn_conv_w"].shape[-1]
    v_width = w["gdn_w_out"].shape[1]
    qkv, z2, bg = _gdn_in(x1, mod[1], w["norm_g"][1], w["gdn_w_in"][0], w["gdn_a_log"][0],
                          w["gdn_dt_bias"][0], conv_ch, v_width)
    o, s_new = _gdn_core(qkv, bg, w["gdn_conv_w"][0], gdn_conv0[0], gdn_s0[0])
    y = _gdn_out(o, z2, x1, mod[1], w["gdn_norm_g"][0], w["gdn_w_out"][0], w["final_g"])
    new_hist = qkv[:, L - (GDN_CONV - 1):, :]
    return (y, hr.reshape(1, B, G, P), hi.reshape(1, B, G, P), s_new[None], new_hist[None])


def kernel(x_prompt, x_sample, c_prompt, c_sample, state_s5_re, state_s5_im, state_gdn, state_gdn_conv, norm_g, w_ada, b_ada, s5_w_in, s5_log_step, s5_lambda_re, s5_lambda_im, s5_b_re, s5_b_im, s5_c_re, s5_c_im, s5_d, s5_w_glu, s5_b_glu, s5_w_out, gdn_w_in, gdn_conv_w, gdn_a_log, gdn_dt_bias, gdn_norm_g, gdn_w_out, final_g):
    w = dict(norm_g=norm_g, s5_w_in=s5_w_in, s5_log_step=s5_log_step, s5_lambda_re=s5_lambda_re,
             s5_lambda_im=s5_lambda_im, s5_b_re=s5_b_re, s5_b_im=s5_b_im, s5_c_re=s5_c_re,
             s5_c_im=s5_c_im, s5_d=s5_d, s5_w_glu=s5_w_glu, s5_b_glu=s5_b_glu, s5_w_out=s5_w_out,
             gdn_w_in=gdn_w_in, gdn_conv_w=gdn_conv_w, gdn_a_log=gdn_a_log,
             gdn_dt_bias=gdn_dt_bias, gdn_norm_g=gdn_norm_g, gdn_w_out=gdn_w_out, final_g=final_g)
    bp, _, d = x_prompt.shape
    bs = x_sample.shape[0]
    depth = w_ada.shape[0]
    mod = _ada_mod(jnp.concatenate([c_prompt, c_sample], axis=0), w_ada, b_ada)
    mod = mod.reshape(depth, bp + bs, 3, d)

    z_s5 = jnp.zeros((state_s5_re.shape[0], bp) + state_s5_re.shape[2:], F32)
    z_gdn = jnp.zeros((state_gdn.shape[0], bp) + state_gdn.shape[2:], F32)
    z_conv = jnp.zeros((state_gdn_conv.shape[0], bp) + state_gdn_conv.shape[2:], F32)
    yp, s5r_p, s5i_p, gdn_p, conv_p = _trunk(x_prompt, mod[:, :bp], z_s5, z_s5, z_gdn, z_conv, w)
    ys, s5r_s, s5i_s, gdn_s, conv_s = _trunk(x_sample, mod[:, bp:], state_s5_re, state_s5_im,
                                             state_gdn, state_gdn_conv, w)
    return (yp, ys, s5r_p, s5i_p, gdn_p, conv_p, s5r_s, s5i_s, gdn_s, conv_s)
```

```python
import functools
import math

import jax
import jax.numpy as jnp
from jax import lax
from jax.experimental import pallas as pl
from jax.experimental.pallas import tpu as pltpu

F32 = jnp.float32
BF16 = jnp.bfloat16
EPS = 1e-6

S5_GROUP = 16
S5_STATE = 64
S5_BLOCK_GROUPS = 8
GDN_DK = 128
GDN_DV = 128
GDN_CONV = 4
GDN_CHUNK = 64
INV_BLOCK = 16
HIST_ROWS = 8
LANES = 128
VMEM_LIMIT = 56 * 1024 * 1024
HI = lax.Precision.HIGHEST
NT_DIMS = (((1,), (1,)), ((), ()))


def _params(*sem):
    return pltpu.CompilerParams(dimension_semantics=sem, vmem_limit_bytes=VMEM_LIMIT)


def _dot(a, b):
    return jnp.dot(a.astype(BF16), b.astype(BF16), preferred_element_type=F32)


def _row_tile(L):
    return min(512, L)


def _ada_kernel(c_ref, w_ref, b_ref, o_ref):
    c = c_ref[...]
    o_ref[0] = _dot(jax.nn.silu(c), w_ref[0]) + b_ref[0]


def _ada_mod(c_all, w_ada, b_ada):
    depth, d, d3 = w_ada.shape
    r = c_all.shape[0]
    tn = 768
    return pl.pallas_call(
        _ada_kernel,
        grid=(depth, d3 // tn),
        in_specs=[
            pl.BlockSpec((r, d), lambda i, j: (0, 0)),
            pl.BlockSpec((1, d, tn), lambda i, j: (i, 0, j)),
            pl.BlockSpec((1, 1, tn), lambda i, j: (i, 0, j)),
        ],
        out_specs=pl.BlockSpec((1, r, tn), lambda i, j: (i, 0, j)),
        out_shape=jax.ShapeDtypeStruct((depth, r, d3), F32),
        compiler_params=_params("parallel", "parallel"),
        name="ada_mod",
    )(c_all, w_ada.astype(BF16), b_ada.reshape(depth, 1, d3))


def _norm_mod(x, g, m):
    h = x * lax.rsqrt(jnp.mean(x * x, axis=-1, keepdims=True) + EPS) * g
    return h * (1.0 + m[1:2]) + m[0:1]


def _s5_in_kernel(x_ref, mod_ref, g_ref, w_ref, u_ref, z_ref):
    e = u_ref.shape[-1]
    h = _norm_mod(x_ref[0], g_ref[...], mod_ref[0])
    p = _dot(h, w_ref[...])
    u_ref[...] = p[:, :e]
    z_ref[0] = p[:, e:]


def _s5_in(x, mod, g, w_in):
    B, L, D = x.shape
    E = w_in.shape[1] // 2
    tm = _row_tile(L)
    return pl.pallas_call(
        _s5_in_kernel,
        grid=(B, L // tm),
        in_specs=[
            pl.BlockSpec((1, tm, D), lambda b, i: (b, i, 0)),
            pl.BlockSpec((1, 3, D), lambda b, i: (b, 0, 0)),
            pl.BlockSpec((1, D), lambda b, i: (0, 0)),
            pl.BlockSpec((D, 2 * E), lambda b, i: (0, 0)),
        ],
        out_specs=[
            pl.BlockSpec((tm, E), lambda b, i: (i, b)),
            pl.BlockSpec((1, tm, E), lambda b, i: (b, i, 0)),
        ],
        out_shape=[
            jax.ShapeDtypeStruct((L, B * E), F32),
            jax.ShapeDtypeStruct((B, L, E), F32),
        ],
        compiler_params=_params("parallel", "parallel"),
        name="s5_in",
    )(x, mod, g.reshape(1, D), w_in.astype(BF16))


def _s5_disc_kernel(ls_ref, lr_ref, li_ref, br_ref, bi_ref, ar_ref, ai_ref, bbr_ref, bbi_ref):
    step = jnp.exp(ls_ref[...])
    lr = lr_ref[...]
    li = li_ref[...]
    mag = jnp.exp(lr * step)
    ar = mag * jnp.cos(li * step)
    ai = mag * jnp.sin(li * step)
    den = lr * lr + li * li
    xr = ar - 1.0
    nr = (xr * lr + ai * li) / den
    ni = (ai * lr - xr * li) / den
    br = br_ref[...]
    bi = bi_ref[...]
    ar_ref[...] = ar
    ai_ref[...] = ai
    bbr_ref[...] = nr * br - ni * bi
    bbi_ref[...] = nr * bi + ni * br


def _s5_disc(log_step, lam_re, lam_im, b_re, b_im):
    G, P = lam_re.shape
    n = G * P
    c = b_re.shape[-1]
    ls = jnp.broadcast_to(log_step[:, None], (G, P)).reshape(1, n)
    brT = b_re.transpose(2, 0, 1).reshape(c, n)
    biT = b_im.transpose(2, 0, 1).reshape(c, n)
    vec = jax.ShapeDtypeStruct((1, n), F32)
    mat = jax.ShapeDtypeStruct((c, n), F32)
    return pl.pallas_call(
        _s5_disc_kernel,
        out_shape=[vec, vec, mat, mat],
        name="s5_disc",
    )(ls, lam_re.reshape(1, n), lam_im.reshape(1, n), brT, biT)


def _s5_block_weights(bbrT, bbiT, c_re, c_im):
    nb = S5_BLOCK_GROUPS
    G = c_re.shape[0]
    nblk = G // nb
    eye = jnp.eye(nb, dtype=F32)

    def bdiag_b(bbT):
        t = bbT.reshape(S5_GROUP, nblk, nb, S5_STATE)
        w = jnp.einsum("cjhp,gh->jgchp", t, eye)
        return w.reshape(nblk, nb * S5_GROUP, nb * S5_STATE)

    def bdiag_c(c):
        t = c.reshape(nblk, nb, S5_GROUP, S5_STATE)
        w = jnp.einsum("jgop,gh->jhpgo", t, eye)
        return w.reshape(nblk, nb * S5_STATE, nb * S5_GROUP)

    wb = jnp.concatenate([bdiag_b(bbrT), bdiag_b(bbiT)], axis=2).astype(BF16)
    wc = jnp.concatenate([bdiag_c(c_re), -bdiag_c(c_im)], axis=1).astype(BF16)
    return wb, wc


def _s5_scan_kernel(u_ref, ar_ref, ai_ref, wb_ref, wc_ref, hr0_ref, hi0_ref,
                    y_ref, hr_ref, hi_ref, bu_scr, xr_scr, xi_scr, *, steps, batch):
    i = pl.program_id(0)
    nblk = wb_ref.shape[0]
    half = wb_ref.shape[2] // 2
    ulanes = wb_ref.shape[1]

    @pl.when(i == 0)
    def _():
        xr_scr[...] = hr0_ref[...]
        xi_scr[...] = hi0_ref[...]

    for j in range(nblk):
        sl = slice(j * half, (j + 1) * half)
        bu_scr[...] = _dot(u_ref[:, j * ulanes:(j + 1) * ulanes], wb_ref[j])
        ar = jnp.broadcast_to(ar_ref[:, sl], (batch, half))
        ai = jnp.broadcast_to(ai_ref[:, sl], (batch, half))

        def body(t, carry):
            xr, xi = carry
            r0 = pl.multiple_of(t * batch, batch)
            bur = bu_scr[pl.ds(r0, batch), 0:half]
            bui = bu_scr[pl.ds(r0, batch), half:2 * half]
            nxr = ar * xr - ai * xi + bur
            nxi = ar * xi + ai * xr + bui
            bu_scr[pl.ds(r0, batch), 0:half] = nxr
            bu_scr[pl.ds(r0, batch), half:2 * half] = nxi
            return nxr, nxi

        xr, xi = lax.fori_loop(0, steps, body, (xr_scr[:, sl], xi_scr[:, sl]), unroll=8)
        xr_scr[:, sl] = xr
        xi_scr[:, sl] = xi
        y_ref[:, j * ulanes:(j + 1) * ulanes] = _dot(bu_scr[...], wc_ref[j])

    @pl.when(i == pl.num_programs(0) - 1)
    def _():
        hr_ref[...] = xr_scr[...]
        hi_ref[...] = xi_scr[...]


def _s5_scan(u_tm, ar, ai, wb, wc, hr0, hi0, B, L):
    E = u_tm.shape[1]
    n = ar.shape[1]
    steps = min(128, L)
    rows = steps * B
    const = lambda i: (0, 0)
    return pl.pallas_call(
        functools.partial(_s5_scan_kernel, steps=steps, batch=B),
        grid=(L // steps,),
        in_specs=[
            pl.BlockSpec((rows, E), lambda i: (i, 0)),
            pl.BlockSpec((1, n), const),
            pl.BlockSpec((1, n), const),
            pl.BlockSpec(wb.shape, lambda i: (0, 0, 0)),
            pl.BlockSpec(wc.shape, lambda i: (0, 0, 0)),
            pl.BlockSpec((B, n), const),
            pl.BlockSpec((B, n), const),
        ],
        out_specs=[
            pl.BlockSpec((rows, E), lambda i: (i, 0)),
            pl.BlockSpec((B, n), const),
            pl.BlockSpec((B, n), const),
        ],
        out_shape=[
            jax.ShapeDtypeStruct((L * B, E), F32),
            jax.ShapeDtypeStruct((B, n), F32),
            jax.ShapeDtypeStruct((B, n), F32),
        ],
        scratch_shapes=[
            pltpu.VMEM((rows, wb.shape[2]), F32),
            pltpu.VMEM((B, n), F32),
            pltpu.VMEM((B, n), F32),
        ],
        compiler_params=_params("arbitrary"),
        name="s5_scan",
    )(u_tm, ar, ai, wb, wc, hr0, hi0)


def _s5_out_kernel(y_ref, u_ref, z_ref, x_ref, mod_ref, d_ref, wg_ref, bg_ref, wo_ref, o_ref):
    y = jax.nn.gelu(y_ref[...] + d_ref[...] * u_ref[...])
    y = y * jax.nn.sigmoid(_dot(y, wg_ref[...]) + bg_ref[...])
    y = y * jax.nn.silu(z_ref[0])
    o_ref[0] = x_ref[0] + mod_ref[0][2:3] * _dot(y, wo_ref[...])


def _s5_out(y_tm, u_tm, z, x, mod, d, w_glu, b_glu, w_out):
    B, L, D = x.shape
    E = z.shape[-1]
    tm = _row_tile(L)
    const = lambda b, i: (0, 0)
    return pl.pallas_call(
        _s5_out_kernel,
        grid=(B, L // tm),
        in_specs=[
            pl.BlockSpec((tm, E), lambda b, i: (i, b)),
            pl.BlockSpec((tm, E), lambda b, i: (i, b)),
            pl.BlockSpec((1, tm, E), lambda b, i: (b, i, 0)),
            pl.BlockSpec((1, tm, D), lambda b, i: (b, i, 0)),
            pl.BlockSpec((1, 3, D), lambda b, i: (b, 0, 0)),
            pl.BlockSpec((1, E), const),
            pl.BlockSpec((E, E), const),
            pl.BlockSpec((1, E), const),
            pl.BlockSpec((E, D), const),
        ],
        out_specs=pl.BlockSpec((1, tm, D), lambda b, i: (b, i, 0)),
        out_shape=jax.ShapeDtypeStruct((B, L, D), F32),
        compiler_params=_params("parallel", "parallel"),
        name="s5_out",
    )(y_tm, u_tm, z, x, mod, d.reshape(1, E), w_glu.astype(BF16), b_glu.reshape(1, E),
      w_out.astype(BF16))


def _gdn_in_kernel(x_ref, mod_ref, g_ref, wqkv_ref, wz_ref, wba_ref, alog_ref, dtb_ref,
                   qkv_ref, z_ref, bg_ref, *, heads):
    h = _norm_mod(x_ref[0], g_ref[...], mod_ref[0]).astype(BF16)
    qkv_ref[0] = jnp.dot(h, wqkv_ref[...], preferred_element_type=F32)
    z_ref[0] = jnp.dot(h, wz_ref[...], preferred_element_type=F32)
    ba = jnp.dot(h, wba_ref[...], preferred_element_type=F32)
    beta = jax.nn.sigmoid(ba)
    a = ba + dtb_ref[...]
    softplus = jnp.maximum(a, 0.0) + jnp.log1p(jnp.exp(-jnp.abs(a)))
    g = -jnp.exp(alog_ref[...]) * softplus
    lane = lax.broadcasted_iota(jnp.int32, ba.shape, 1)
    bg_ref[0] = jnp.where(lane < heads, beta, g)


def _gdn_in(x, mod, g, w_in, a_log, dt_bias, conv_ch, v_width):
    B, L, D = x.shape
    H = a_log.shape[0]
    tm = _row_tile(L)
    o1, o2 = conv_ch, conv_ch + v_width
    wba = jnp.zeros((D, LANES), F32).at[:, :2 * H].set(w_in[:, o2:o2 + 2 * H])
    alog = jnp.zeros((1, LANES), F32).at[0, H:2 * H].set(a_log)
    dtb = jnp.zeros((1, LANES), F32).at[0, H:2 * H].set(dt_bias)
    const = lambda b, i: (0, 0)
    return pl.pallas_call(
        functools.partial(_gdn_in_kernel, heads=H),
        grid=(B, L // tm),
        in_specs=[
            pl.BlockSpec((1, tm, D), lambda b, i: (b, i, 0)),
            pl.BlockSpec((1, 3, D), lambda b, i: (b, 0, 0)),
            pl.BlockSpec((1, D), const),
            pl.BlockSpec((D, conv_ch), const),
            pl.BlockSpec((D, v_width), const),
            pl.BlockSpec((D, LANES), const),
            pl.BlockSpec((1, LANES), const),
            pl.BlockSpec((1, LANES), const),
        ],
        out_specs=[
            pl.BlockSpec((1, tm, conv_ch), lambda b, i: (b, i, 0)),
            pl.BlockSpec((1, tm, v_width), lambda b, i: (b, i, 0)),
            pl.BlockSpec((1, tm, LANES), lambda b, i: (b, i, 0)),
        ],
        out_shape=[
            jax.ShapeDtypeStruct((B, L, conv_ch), F32),
            jax.ShapeDtypeStruct((B, L, v_width), F32),
            jax.ShapeDtypeStruct((B, L, LANES), F32),
        ],
        compiler_params=_params("parallel", "parallel"),
        name="gdn_in",
    )(x, mod, g.reshape(1, D), w_in[:, :o1].astype(BF16), w_in[:, o1:o2].astype(BF16),
      wba.astype(BF16), alog, dtb)


def _l2norm(x):
    return x * lax.rsqrt(jnp.sum(x * x, axis=-1, keepdims=True) + EPS)


def _gdn_core_kernel(qkv_ref, bg_ref, cw_ref, hist_ref, s0_ref, o_ref, sout_ref,
                     ext_scr, s_scr, *, T, qk_heads, v_heads):
    c = pl.program_id(1)
    qk_width = qk_heads * GDN_DK
    rep = v_heads // qk_heads

    @pl.when(c == 0)
    def _():
        s_scr[...] = s0_ref[0]
        ext_scr[0:HIST_ROWS, :] = hist_ref[0]

    ext_scr[HIST_ROWS:HIST_ROWS + T, :] = qkv_ref[0]
    cw = cw_ref[...]
    base = HIST_ROWS - (GDN_CONV - 1)
    conv = ext_scr[base:base + T, :] * cw[0:1]
    for j in range(1, GDN_CONV):
        conv = conv + ext_scr[base + j:base + j + T, :] * cw[j:j + 1]
    ext_scr[0:HIST_ROWS, :] = ext_scr[T:T + HIST_ROWS, :]
    act = jax.nn.silu(conv)

    bg = bg_ref[0]
    row = lax.broadcasted_iota(jnp.int32, (T, T), 0)
    col = lax.broadcasted_iota(jnp.int32, (T, T), 1)
    causal = row >= col
    strict = row > col
    gc = jnp.dot(causal.astype(F32), bg, precision=HI, preferred_element_type=F32)
    eye = (lax.broadcasted_iota(jnp.int32, (LANES, LANES), 0)
           == lax.broadcasted_iota(jnp.int32, (LANES, LANES), 1))
    gct = lax.dot_general(eye.astype(F32), gc, NT_DIMS, precision=HI, preferred_element_type=F32)
    sh = int(math.log2(INV_BLOCK))
    diag_blk = (row >> sh) == (col >> sh)
    merges = []
    while (1 << sh) < T:
        merges.append(((row >> (sh + 1)) == (col >> (sh + 1))) & ((row >> sh) > (col >> sh)))
        sh += 1

    heads = range(v_heads)
    qs, ks, kks, qks = [], [], [], []
    for hq in range(qk_heads):
        qs.append(_l2norm(act[:, hq * GDN_DK:(hq + 1) * GDN_DK]) * (GDN_DK ** -0.5))
        ks.append(_l2norm(act[:, qk_width + hq * GDN_DK:qk_width + (hq + 1) * GDN_DK]))
    for hq in range(qk_heads):
        kb = ks[hq].astype(BF16)
        kks.append(lax.dot_general(kb, kb, NT_DIMS, preferred_element_type=F32))
        qks.append(lax.dot_general(qs[hq].astype(BF16), kb, NT_DIMS, preferred_element_type=F32))
    beta = [bg[:, h:h + 1] for h in heads]
    gcol = [gc[:, v_heads + h:v_heads + h + 1] for h in heads]
    grow = [gct[v_heads + h:v_heads + h + 1, :] for h in heads]
    decay = [jnp.where(causal, jnp.exp(jnp.where(causal, gcol[h] - grow[h], 0.0)), 0.0)
             for h in heads]
    egc = [jnp.exp(gcol[h]) for h in heads]
    m = [jnp.where(strict, (beta[h] * kks[h // rep]) * decay[h], 0.0) for h in heads]
    p = [jnp.where(diag_blk, -m[h], 0.0) for h in heads]
    e = list(p)
    for _ in range(int(math.log2(INV_BLOCK)) - 1):
        p = [_dot(p[h], p[h]) for h in heads]
        e = [e[h] + p[h] + _dot(e[h], p[h]) for h in heads]
    for blk in merges:
        m21 = [jnp.where(blk, m[h], 0.0) for h in heads]
        y = [m21[h] + _dot(m21[h], e[h]) for h in heads]
        e = [e[h] - (y[h] + _dot(e[h], y[h])) for h in heads]
    rhs = [jnp.concatenate(
        [act[:, 2 * qk_width + h * GDN_DV:2 * qk_width + (h + 1) * GDN_DV] * beta[h],
         ks[h // rep] * (beta[h] * egc[h])], axis=1) for h in heads]
    sol = [rhs[h] + _dot(e[h], rhs[h]) for h in heads]
    s = [s_scr[h] for h in heads]
    ws = [_dot(jnp.concatenate([sol[h][:, GDN_DV:], qs[h // rep] * egc[h]], axis=0), s[h])
          for h in heads]
    v_new = [sol[h][:, :GDN_DV] - ws[h][:T] for h in heads]
    for h in heads:
        o_ref[0, :, h * GDN_DV:(h + 1) * GDN_DV] = (
            ws[h][T:] + _dot(qks[h // rep] * decay[h], v_new[h]))
    g_last = [grow[h][:, T - 1:T] for h in heads]
    for h in heads:
        kd = ks[h // rep] * jnp.exp(g_last[h] - gcol[h])
        s_scr[h] = s[h] * jnp.exp(g_last[h]) + lax.dot_general(
            kd.astype(BF16), v_new[h].astype(BF16), (((0,), (0,)), ((), ())),
            preferred_element_type=F32)

    @pl.when(c == pl.num_programs(1) - 1)
    def _():
        sout_ref[0] = s_scr[...]


def _gdn_core(qkv, bg, conv_w, hist, s0):
    B, L, C = qkv.shape
    H = s0.shape[1]
    T = GDN_CHUNK if L % GDN_CHUNK == 0 else L
    hist8 = jnp.zeros((B, HIST_ROWS, C), F32).at[:, HIST_ROWS - (GDN_CONV - 1):, :].set(hist)
    return pl.pallas_call(
        functools.partial(_gdn_core_kernel, T=T, qk_heads=H // 2, v_heads=H),
        grid=(B, L // T),
        in_specs=[
            pl.BlockSpec((1, T, C), lambda b, c: (b, c, 0)),
            pl.BlockSpec((1, T, LANES), lambda b, c: (b, c, 0)),
            pl.BlockSpec((GDN_CONV, C), lambda b, c: (0, 0)),
            pl.BlockSpec((1, HIST_ROWS, C), lambda b, c: (b, 0, 0)),
            pl.BlockSpec((1, H, GDN_DK, GDN_DV), lambda b, c: (b, 0, 0, 0)),
        ],
        out_specs=[
            pl.BlockSpec((1, T, H * GDN_DV), lambda b, c: (b, c, 0)),
            pl.BlockSpec((1, H, GDN_DK, GDN_DV), lambda b, c: (b, 0, 0, 0)),
        ],
        out_shape=[
            jax.ShapeDtypeStruct((B, L, H * GDN_DV), F32),
            jax.ShapeDtypeStruct((B, H, GDN_DK, GDN_DV), F32),
        ],
        scratch_shapes=[
            pltpu.VMEM((HIST_ROWS + T, C), F32),
            pltpu.VMEM((H, GDN_DK, GDN_DV), F32),
        ],
        compiler_params=_params("parallel", "arbitrary"),
        name="gdn_core",
    )(qkv, bg, conv_w, hist8, s0)


def _gdn_out_kernel(o_ref, z_ref, x_ref, mod_ref, ng_ref, wo_ref, fg_ref, y_ref, *, heads):
    o = o_ref[0]
    parts = []
    for h in range(heads):
        oh = o[:, h * GDN_DV:(h + 1) * GDN_DV]
        parts.append(oh * lax.rsqrt(jnp.mean(oh * oh, axis=-1, keepdims=True) + EPS))
    on = jnp.concatenate(parts, axis=1) * ng_ref[...]
    out = _dot(on * jax.nn.silu(z_ref[0]), wo_ref[...])
    x2 = x_ref[0] + mod_ref[0][2:3] * out
    y_ref[0] = x2 * lax.rsqrt(jnp.mean(x2 * x2, axis=-1, keepdims=True) + EPS) * fg_ref[...]


def _gdn_out(o, z, x, mod, norm_g, w_out, final_g):
    B, L, D = x.shape
    V = o.shape[-1]
    H = V // GDN_DV
    tm = _row_tile(L)
    const = lambda b, i: (0, 0)
    return pl.pallas_call(
        functools.partial(_gdn_out_kernel, heads=H),
        grid=(B, L // tm),
        in_specs=[
            pl.BlockSpec((1, tm, V), lambda b, i: (b, i, 0)),
            pl.BlockSpec((1, tm, V), lambda b, i: (b, i, 0)),
            pl.BlockSpec((1, tm, D), lambda b, i: (b, i, 0)),
            pl.BlockSpec((1, 3, D), lambda b, i: (b, 0, 0)),
            pl.BlockSpec((1, V), const),
            pl.BlockSpec((V, D), const),
            pl.BlockSpec((1, D), const),
        ],
        out_specs=pl.BlockSpec((1, tm, D), lambda b, i: (b, i, 0)),
        out_shape=jax.ShapeDtypeStruct((B, L, D), F32),
        compiler_params=_params("parallel", "parallel"),
        name="gdn_out",
    )(o, z, x, mod, jnp.tile(norm_g, H).reshape(1, V), w_out.astype(BF16), final_g.reshape(1, D))


def _trunk(x, mod, s5_re0, s5_im0, gdn_s0, gdn_conv0, w):
    B, L, D = x.shape
    G, P = w["s5_lambda_re"].shape[1:]
    n = G * P

    u_tm, z = _s5_in(x, mod[0], w["norm_g"][0], w["s5_w_in"][0])
    E = z.shape[-1]
    ar, ai, bbrT, bbiT = _s5_disc(w["s5_log_step"][0], w["s5_lambda_re"][0], w["s5_lambda_im"][0],
                                  w["s5_b_re"][0], w["s5_b_im"][0])
    wb, wc = _s5_block_weights(bbrT, bbiT, w["s5_c_re"][0], w["s5_c_im"][0])
    y_tm, hr, hi = _s5_scan(u_tm.reshape(L * B, E), ar, ai, wb, wc,
                            s5_re0[0].reshape(B, n), s5_im0[0].reshape(B, n), B, L)
    x1 = _s5_out(y_tm.reshape(L, B * E), u_tm, z, x, mod[0], w["s5_d"][0], w["s5_w_glu"][0],
                 w["s5_b_glu"][0], w["s5_w_out"][0])

    conv_ch = w["gdn_conv_w"].shape[-1]
    v_width = w["gdn_w_out"].shape[1]
    qkv, z2, bg = _gdn_in(x1, mod[1], w["norm_g"][1], w["gdn_w_in"][0], w["gdn_a_log"][0],
                          w["gdn_dt_bias"][0], conv_ch, v_width)
    o, s_new = _gdn_core(qkv, bg, w["gdn_conv_w"][0], gdn_conv0[0], gdn_s0[0])
    y = _gdn_out(o, z2, x1, mod[1], w["gdn_norm_g"][0], w["gdn_w_out"][0], w["final_g"])
    new_hist = qkv[:, L - (GDN_CONV - 1):, :]
    return (y, hr.reshape(1, B, G, P), hi.reshape(1, B, G, P), s_new[None], new_hist[None])


def kernel(x_prompt, x_sample, c_prompt, c_sample, state_s5_re, state_s5_im, state_gdn, state_gdn_conv, norm_g, w_ada, b_ada, s5_w_in, s5_log_step, s5_lambda_re, s5_lambda_im, s5_b_re, s5_b_im, s5_c_re, s5_c_im, s5_d, s5_w_glu, s5_b_glu, s5_w_out, gdn_w_in, gdn_conv_w, gdn_a_log, gdn_dt_bias, gdn_norm_g, gdn_w_out, final_g):
    w = dict(norm_g=norm_g, s5_w_in=s5_w_in, s5_log_step=s5_log_step, s5_lambda_re=s5_lambda_re,
             s5_lambda_im=s5_lambda_im, s5_b_re=s5_b_re, s5_b_im=s5_b_im, s5_c_re=s5_c_re,
             s5_c_im=s5_c_im, s5_d=s5_d, s5_w_glu=s5_w_glu, s5_b_glu=s5_b_glu, s5_w_out=s5_w_out,
             gdn_w_in=gdn_w_in, gdn_conv_w=gdn_conv_w, gdn_a_log=gdn_a_log,
             gdn_dt_bias=gdn_dt_bias, gdn_norm_g=gdn_norm_g, gdn_w_out=gdn_w_out, final_g=final_g)
    bp, _, d = x_prompt.shape
    bs = x_sample.shape[0]
    depth = w_ada.shape[0]
    mod = _ada_mod(jnp.concatenate([c_prompt, c_sample], axis=0), w_ada, b_ada)
    mod = mod.reshape(depth, bp + bs, 3, d)

    z_s5 = jnp.zeros((state_s5_re.shape[0], bp) + state_s5_re.shape[2:], F32)
    z_gdn = jnp.zeros((state_gdn.shape[0], bp) + state_gdn.shape[2:], F32)
    z_conv = jnp.zeros((state_gdn_conv.shape[0], bp) + state_gdn_conv.shape[2:], F32)
    yp, s5r_p, s5i_p, gdn_p, conv_p = _trunk(x_prompt, mod[:, :bp], z_s5, z_s5, z_gdn, z_conv, w)
    ys, s5r_s, s5i_s, gdn_s, conv_s = _trunk(x_sample, mod[:, bp:], state_s5_re, state_s5_im,
                                             state_gdn, state_gdn_conv, w)
    return (yp, ys, s5r_p, s5i_p, gdn_p, conv_p, s5r_s, s5i_s, gdn_s, conv_s)
```

```python
import functools
import math

import jax
import jax.numpy as jnp
from jax import lax
from jax.experimental import pallas as pl
from jax.experimental.pallas import tpu as pltpu

F32 = jnp.float32
BF16 = jnp.bfloat16
EPS = 1e-6

S5_GROUP = 16
S5_STATE = 64
S5_BLOCK_GROUPS = 8
GDN_DK = 128
GDN_DV = 128
GDN_CONV = 4
GDN_CHUNK = 64
INV_BLOCK = 16
HIST_ROWS = 8
LANES = 128
VMEM_LIMIT = 56 * 1024 * 1024
HI = lax.Precision.HIGHEST
NT_DIMS = (((1,), (1,)), ((), ()))


def _params(*sem):
    return pltpu.CompilerParams(dimension_semantics=sem, vmem_limit_bytes=VMEM_LIMIT)


def _dot(a, b):
    return jnp.dot(a.astype(BF16), b.astype(BF16), preferred_element_type=F32)


def _row_tile(L):
    return min(512, L)


def _ada_kernel(c_ref, w_ref, b_ref, o_ref):
    c = c_ref[...]
    o_ref[0] = _dot(jax.nn.silu(c), w_ref[0]) + b_ref[0]


def _ada_mod(c_all, w_ada, b_ada):
    depth, d, d3 = w_ada.shape
    r = c_all.shape[0]
    tn = 768
    return pl.pallas_call(
        _ada_kernel,
        grid=(depth, d3 // tn),
        in_specs=[
            pl.BlockSpec((r, d), lambda i, j: (0, 0)),
            pl.BlockSpec((1, d, tn), lambda i, j: (i, 0, j)),
            pl.BlockSpec((1, 1, tn), lambda i, j: (i, 0, j)),
        ],
        out_specs=pl.BlockSpec((1, r, tn), lambda i, j: (i, 0, j)),
        out_shape=jax.ShapeDtypeStruct((depth, r, d3), F32),
        compiler_params=_params("parallel", "parallel"),
        name="ada_mod",
    )(c_all, w_ada.astype(BF16), b_ada.reshape(depth, 1, d3))


def _norm_mod(x, g, m):
    h = x * lax.rsqrt(jnp.mean(x * x, axis=-1, keepdims=True) + EPS) * g
    return h * (1.0 + m[1:2]) + m[0:1]


def _s5_disc_kernel(ls_ref, lr_ref, li_ref, br_ref, bi_ref, ar_ref, ai_ref, bbr_ref, bbi_ref):
    step = jnp.exp(ls_ref[...])
    lr = lr_ref[...]
    li = li_ref[...]
    mag = jnp.exp(lr * step)
    ar = mag * jnp.cos(li * step)
    ai = mag * jnp.sin(li * step)
    den = lr * lr + li * li
    xr = ar - 1.0
    nr = (xr * lr + ai * li) / den
    ni = (ai * lr - xr * li) / den
    br = br_ref[...]
    bi = bi_ref[...]
    ar_ref[...] = ar
    ai_ref[...] = ai
    bbr_ref[...] = nr * br - ni * bi
    bbi_ref[...] = nr * bi + ni * br


def _s5_disc(log_step, lam_re, lam_im, b_re, b_im):
    G, P = lam_re.shape
    n = G * P
    c = b_re.shape[-1]
    ls = jnp.broadcast_to(log_step[:, None], (G, P)).reshape(1, n)
    brT = b_re.transpose(2, 0, 1).reshape(c, n)
    biT = b_im.transpose(2, 0, 1).reshape(c, n)
    vec = jax.ShapeDtypeStruct((1, n), F32)
    mat = jax.ShapeDtypeStruct((c, n), F32)
    return pl.pallas_call(
        _s5_disc_kernel,
        out_shape=[vec, vec, mat, mat],
        name="s5_disc",
    )(ls, lam_re.reshape(1, n), lam_im.reshape(1, n), brT, biT)


def _s5_block_weights(bbrT, bbiT, c_re, c_im):
    nb = S5_BLOCK_GROUPS
    G = c_re.shape[0]
    nblk = G // nb
    eye = jnp.eye(nb, dtype=F32)

    def bdiag_b(bbT):
        t = bbT.reshape(S5_GROUP, nblk, nb, S5_STATE)
        w = jnp.einsum("cjhp,gh->jgchp", t, eye)
        return w.reshape(nblk, nb * S5_GROUP, nb * S5_STATE)

    def bdiag_c(c):
        t = c.reshape(nblk, nb, S5_GROUP, S5_STATE)
        w = jnp.einsum("jgop,gh->jhpgo", t, eye)
        return w.reshape(nblk, nb * S5_STATE, nb * S5_GROUP)

    wb = jnp.concatenate([bdiag_b(bbrT), bdiag_b(bbiT)], axis=2).astype(BF16)
    wc = jnp.concatenate([bdiag_c(c_re), -bdiag_c(c_im)], axis=1).astype(BF16)
    return wb, wc


def _s5_layer_kernel(x_ref, mod_ref, g_ref, win_ref, ar_ref, ai_ref, wb_ref, wc_ref, hr0_ref,
                     hi0_ref, d_ref, wg_ref, bglu_ref, wo_ref, o_ref, hr_ref, hi_ref,
                     u_scr, z_scr, y_scr, bu_scr, xr_scr, xi_scr, *, steps, batch):
    i = pl.program_id(0)
    rows = steps * batch
    d_model = x_ref.shape[-1]
    e = u_scr.shape[-1]
    nblk = wb_ref.shape[0]
    half = wb_ref.shape[2] // 2
    ulanes = wb_ref.shape[1]

    @pl.when(i == 0)
    def _():
        xr_scr[...] = hr0_ref[...]
        xi_scr[...] = hi0_ref[...]

    x3 = pltpu.einshape("btk->tbk", x_ref[...])
    m = mod_ref[...]
    h3 = x3 * lax.rsqrt(jnp.mean(x3 * x3, axis=-1, keepdims=True) + EPS) * g_ref[...]
    h3 = h3 * (1.0 + m[1]) + m[0]
    p = _dot(h3.reshape(rows, d_model), win_ref[...])
    u_scr[...] = p[:, :e]
    z_scr[...] = p[:, e:]

    for j in range(nblk):
        sl = slice(j * half, (j + 1) * half)
        bu_scr[...] = _dot(u_scr[:, j * ulanes:(j + 1) * ulanes], wb_ref[j])
        ar = jnp.broadcast_to(ar_ref[:, sl], (batch, half))
        ai = jnp.broadcast_to(ai_ref[:, sl], (batch, half))

        def body(t, carry):
            xr, xi = carry
            r0 = pl.multiple_of(t * batch, batch)
            bur = bu_scr[pl.ds(r0, batch), 0:half]
            bui = bu_scr[pl.ds(r0, batch), half:2 * half]
            nxr = ar * xr - ai * xi + bur
            nxi = ar * xi + ai * xr + bui
            bu_scr[pl.ds(r0, batch), 0:half] = nxr
            bu_scr[pl.ds(r0, batch), half:2 * half] = nxi
            return nxr, nxi

        xr, xi = lax.fori_loop(0, steps, body, (xr_scr[:, sl], xi_scr[:, sl]), unroll=8)
        xr_scr[:, sl] = xr
        xi_scr[:, sl] = xi
        y_scr[:, j * ulanes:(j + 1) * ulanes] = _dot(bu_scr[...], wc_ref[j])

    y = jax.nn.gelu(y_scr[...] + d_ref[...] * u_scr[...])
    y = y * jax.nn.sigmoid(_dot(y, wg_ref[...]) + bglu_ref[...])
    y = y * jax.nn.silu(z_scr[...])
    out3 = _dot(y, wo_ref[...]).reshape(steps, batch, d_model)
    o_ref[...] = pltpu.einshape("tbk->btk", x3 + m[2] * out3)

    @pl.when(i == pl.num_programs(0) - 1)
    def _():
        hr_ref[...] = xr_scr[...]
        hi_ref[...] = xi_scr[...]


def _s5_layer(x, mod, g, w_in, ar, ai, wb, wc, hr0, hi0, d, w_glu, b_glu, w_out):
    B, L, D = x.shape
    E = w_in.shape[1] // 2
    n = ar.shape[1]
    steps = min(64, L)
    rows = steps * B
    const2 = lambda i: (0, 0)
    const3 = lambda i: (0, 0, 0)

    def resident(shape):
        return pl.BlockSpec(shape, const2 if len(shape) == 2 else const3,
                            pipeline_mode=pl.Buffered(1))

    return pl.pallas_call(
        functools.partial(_s5_layer_kernel, steps=steps, batch=B),
        grid=(L // steps,),
        in_specs=[
            pl.BlockSpec((B, steps, D), lambda i: (0, i, 0)),
            resident((3, B, D)),
            resident((1, D)),
            resident((D, 2 * E)),
            resident((1, n)),
            resident((1, n)),
            resident(wb.shape),
            resident(wc.shape),
            resident((B, n)),
            resident((B, n)),
            resident((1, E)),
            resident((E, E)),
            resident((1, E)),
            resident((E, D)),
        ],
        out_specs=[
            pl.BlockSpec((B, steps, D), lambda i: (0, i, 0)),
            pl.BlockSpec((B, n), const2),
            pl.BlockSpec((B, n), const2),
        ],
        out_shape=[
            jax.ShapeDtypeStruct((B, L, D), F32),
            jax.ShapeDtypeStruct((B, n), F32),
            jax.ShapeDtypeStruct((B, n), F32),
        ],
        scratch_shapes=[
            pltpu.VMEM((rows, E), F32),
            pltpu.VMEM((rows, E), F32),
            pltpu.VMEM((rows, E), F32),
            pltpu.VMEM((rows, wb.shape[2]), F32),
            pltpu.VMEM((B, n), F32),
            pltpu.VMEM((B, n), F32),
        ],
        compiler_params=_params("arbitrary"),
        name="s5_layer",
    )(x, mod, g.reshape(1, D), w_in.astype(BF16), ar, ai, wb, wc, hr0, hi0, d.reshape(1, E),
      w_glu.astype(BF16), b_glu.reshape(1, E), w_out.astype(BF16))


def _gdn_in_kernel(x_ref, mod_ref, g_ref, wqkv_ref, wz_ref, wba_ref, alog_ref, dtb_ref,
                   qkv_ref, z_ref, bg_ref, *, heads):
    h = _norm_mod(x_ref[0], g_ref[...], mod_ref[0]).astype(BF16)
    qkv_ref[0] = jnp.dot(h, wqkv_ref[...], preferred_element_type=F32)
    z_ref[0] = jnp.dot(h, wz_ref[...], preferred_element_type=F32)
    ba = jnp.dot(h, wba_ref[...], preferred_element_type=F32)
    beta = jax.nn.sigmoid(ba)
    a = ba + dtb_ref[...]
    softplus = jnp.maximum(a, 0.0) + jnp.log1p(jnp.exp(-jnp.abs(a)))
    g = -jnp.exp(alog_ref[...]) * softplus
    lane = lax.broadcasted_iota(jnp.int32, ba.shape, 1)
    bg_ref[0] = jnp.where(lane < heads, beta, g)


def _gdn_in(x, mod, g, w_in, a_log, dt_bias, conv_ch, v_width):
    B, L, D = x.shape
    H = a_log.shape[0]
    tm = _row_tile(L)
    o1, o2 = conv_ch, conv_ch + v_width
    wba = jnp.zeros((D, LANES), F32).at[:, :2 * H].set(w_in[:, o2:o2 + 2 * H])
    alog = jnp.zeros((1, LANES), F32).at[0, H:2 * H].set(a_log)
    dtb = jnp.zeros((1, LANES), F32).at[0, H:2 * H].set(dt_bias)
    const = lambda b, i: (0, 0)
    return pl.pallas_call(
        functools.partial(_gdn_in_kernel, heads=H),
        grid=(B, L // tm),
        in_specs=[
            pl.BlockSpec((1, tm, D), lambda b, i: (b, i, 0)),
            pl.BlockSpec((1, 3, D), lambda b, i: (b, 0, 0)),
            pl.BlockSpec((1, D), const),
            pl.BlockSpec((D, conv_ch), const),
            pl.BlockSpec((D, v_width), const),
            pl.BlockSpec((D, LANES), const),
            pl.BlockSpec((1, LANES), const),
            pl.BlockSpec((1, LANES), const),
        ],
        out_specs=[
            pl.BlockSpec((1, tm, conv_ch), lambda b, i: (b, i, 0)),
            pl.BlockSpec((1, tm, v_width), lambda b, i: (b, i, 0)),
            pl.BlockSpec((1, tm, LANES), lambda b, i: (b, i, 0)),
        ],
        out_shape=[
            jax.ShapeDtypeStruct((B, L, conv_ch), F32),
            jax.ShapeDtypeStruct((B, L, v_width), F32),
            jax.ShapeDtypeStruct((B, L, LANES), F32),
        ],
        compiler_params=_params("parallel", "parallel"),
        name="gdn_in",
    )(x, mod, g.reshape(1, D), w_in[:, :o1].astype(BF16), w_in[:, o1:o2].astype(BF16),
      wba.astype(BF16), alog, dtb)


def _l2norm(x):
    return x * lax.rsqrt(jnp.sum(x * x, axis=-1, keepdims=True) + EPS)


def _gdn_core_kernel(qkv_ref, bg_ref, cw_ref, hist_ref, s0_ref, o_ref, sout_ref,
                     ext_scr, s_scr, *, T, qk_heads, v_heads):
    c = pl.program_id(1)
    qk_width = qk_heads * GDN_DK
    rep = v_heads // qk_heads

    @pl.when(c == 0)
    def _():
        s_scr[...] = s0_ref[0]
        ext_scr[0:HIST_ROWS, :] = hist_ref[0]

    ext_scr[HIST_ROWS:HIST_ROWS + T, :] = qkv_ref[0]
    cw = cw_ref[...]
    base = HIST_ROWS - (GDN_CONV - 1)
    conv = ext_scr[base:base + T, :] * cw[0:1]
    for j in range(1, GDN_CONV):
        conv = conv + ext_scr[base + j:base + j + T, :] * cw[j:j + 1]
    ext_scr[0:HIST_ROWS, :] = ext_scr[T:T + HIST_ROWS, :]
    act = jax.nn.silu(conv)

    bg = bg_ref[0]
    row = lax.broadcasted_iota(jnp.int32, (T, T), 0)
    col = lax.broadcasted_iota(jnp.int32, (T, T), 1)
    causal = row >= col
    strict = row > col
    gc = jnp.dot(causal.astype(F32), bg, precision=HI, preferred_element_type=F32)
    eye = (lax.broadcasted_iota(jnp.int32, (LANES, LANES), 0)
           == lax.broadcasted_iota(jnp.int32, (LANES, LANES), 1))
    gct = lax.dot_general(eye.astype(F32), gc, NT_DIMS, precision=HI, preferred_element_type=F32)
    sh = int(math.log2(INV_BLOCK))
    diag_blk = (row >> sh) == (col >> sh)
    merges = []
    while (1 << sh) < T:
        merges.append(((row >> (sh + 1)) == (col >> (sh + 1))) & ((row >> sh) > (col >> sh)))
        sh += 1

    heads = range(v_heads)
    qs, ks, kks, qks = [], [], [], []
    for hq in range(qk_heads):
        qs.append(_l2norm(act[:, hq * GDN_DK:(hq + 1) * GDN_DK]) * (GDN_DK ** -0.5))
        ks.append(_l2norm(act[:, qk_width + hq * GDN_DK:qk_width + (hq + 1) * GDN_DK]))
    for hq in range(qk_heads):
        kb = ks[hq].astype(BF16)
        kks.append(lax.dot_general(kb, kb, NT_DIMS, preferred_element_type=F32))
        qks.append(lax.dot_general(qs[hq].astype(BF16), kb, NT_DIMS, preferred_element_type=F32))
    beta = [bg[:, h:h + 1] for h in heads]
    gcol = [gc[:, v_heads + h:v_heads + h + 1] for h in heads]
    grow = [gct[v_heads + h:v_heads + h + 1, :] for h in heads]
    decay = [jnp.where(causal, jnp.exp(jnp.where(causal, gcol[h] - grow[h], 0.0)), 0.0)
             for h in heads]
    egc = [jnp.exp(gcol[h]) for h in heads]
    m = [jnp.where(strict, (beta[h] * kks[h // rep]) * decay[h], 0.0) for h in heads]
    p = [jnp.where(diag_blk, -m[h], 0.0) for h in heads]
    e = list(p)
    for _ in range(int(math.log2(INV_BLOCK)) - 1):
        p = [_dot(p[h], p[h]) for h in heads]
        e = [e[h] + p[h] + _dot(e[h], p[h]) for h in heads]
    for blk in merges:
        m21 = [jnp.where(blk, m[h], 0.0) for h in heads]
        y = [m21[h] + _dot(m21[h], e[h]) for h in heads]
        e = [e[h] - (y[h] + _dot(e[h], y[h])) for h in heads]
    rhs = [jnp.concatenate(
        [act[:, 2 * qk_width + h * GDN_DV:2 * qk_width + (h + 1) * GDN_DV] * beta[h],
         ks[h // rep] * (beta[h] * egc[h])], axis=1) for h in heads]
    sol = [rhs[h] + _dot(e[h], rhs[h]) for h in heads]
    s = [s_scr[h] for h in heads]
    ws = [_dot(jnp.concatenate([sol[h][:, GDN_DV:], qs[h // rep] * egc[h]], axis=0), s[h])
          for h in heads]
    v_new = [sol[h][:, :GDN_DV] - ws[h][:T] for h in heads]
    for h in heads:
        o_ref[0, :, h * GDN_DV:(h + 1) * GDN_DV] = (
            ws[h][T:] + _dot(qks[h // rep] * decay[h], v_new[h]))
    g_last = [grow[h][:, T - 1:T] for h in heads]
    for h in heads:
        kd = ks[h // rep] * jnp.exp(g_last[h] - gcol[h])
        s_scr[h] = s[h] * jnp.exp(g_last[h]) + lax.dot_general(
            kd.astype(BF16), v_new[h].astype(BF16), (((0,), (0,)), ((), ())),
            preferred_element_type=F32)

    @pl.when(c == pl.num_programs(1) - 1)
    def _():
        sout_ref[0] = s_scr[...]


def _gdn_core(qkv, bg, conv_w, hist, s0):
    B, L, C = qkv.shape
    H = s0.shape[1]
    T = GDN_CHUNK if L % GDN_CHUNK == 0 else L
    hist8 = jnp.zeros((B, HIST_ROWS, C), F32).at[:, HIST_ROWS - (GDN_CONV - 1):, :].set(hist)
    return pl.pallas_call(
        functools.partial(_gdn_core_kernel, T=T, qk_heads=H // 2, v_heads=H),
        grid=(B, L // T),
        in_specs=[
            pl.BlockSpec((1, T, C), lambda b, c: (b, c, 0)),
            pl.BlockSpec((1, T, LANES), lambda b, c: (b, c, 0)),
            pl.BlockSpec((GDN_CONV, C), lambda b, c: (0, 0)),
            pl.BlockSpec((1, HIST_ROWS, C), lambda b, c: (b, 0, 0)),
            pl.BlockSpec((1, H, GDN_DK, GDN_DV), lambda b, c: (b, 0, 0, 0)),
        ],
        out_specs=[
            pl.BlockSpec((1, T, H * GDN_DV), lambda b, c: (b, c, 0)),
            pl.BlockSpec((1, H, GDN_DK, GDN_DV), lambda b, c: (b, 0, 0, 0)),
        ],
        out_shape=[
            jax.ShapeDtypeStruct((B, L, H * GDN_DV), F32),
            jax.ShapeDtypeStruct((B, H, GDN_DK, GDN_DV), F32),
        ],
        scratch_shapes=[
            pltpu.VMEM((HIST_ROWS + T, C), F32),
            pltpu.VMEM((H, GDN_DK, GDN_DV), F32),
        ],
        compiler_params=_params("parallel", "arbitrary"),
        name="gdn_core",
    )(qkv, bg, conv_w, hist8, s0)


def _gdn_out_kernel(o_ref, z_ref, x_ref, mod_ref, ng_ref, wo_ref, fg_ref, y_ref, *, heads):
    o = o_ref[0]
    parts = []
    for h in range(heads):
        oh = o[:, h * GDN_DV:(h + 1) * GDN_DV]
        parts.append(oh * lax.rsqrt(jnp.mean(oh * oh, axis=-1, keepdims=True) + EPS))
    on = jnp.concatenate(parts, axis=1) * ng_ref[...]
    out = _dot(on * jax.nn.silu(z_ref[0]), wo_ref[...])
    x2 = x_ref[0] + mod_ref[0][2:3] * out
    y_ref[0] = x2 * lax.rsqrt(jnp.mean(x2 * x2, axis=-1, keepdims=True) + EPS) * fg_ref[...]


def _gdn_out(o, z, x, mod, norm_g, w_out, final_g):
    B, L, D = x.shape
    V = o.shape[-1]
    H = V // GDN_DV
    tm = _row_tile(L)
    const = lambda b, i: (0, 0)
    return pl.pallas_call(
        functools.partial(_gdn_out_kernel, heads=H),
        grid=(B, L // tm),
        in_specs=[
            pl.BlockSpec((1, tm, V), lambda b, i: (b, i, 0)),
            pl.BlockSpec((1, tm, V), lambda b, i: (b, i, 0)),
            pl.BlockSpec((1, tm, D), lambda b, i: (b, i, 0)),
            pl.BlockSpec((1, 3, D), lambda b, i: (b, 0, 0)),
            pl.BlockSpec((1, V), const),
            pl.BlockSpec((V, D), const),
            pl.BlockSpec((1, D), const),
        ],
        out_specs=pl.BlockSpec((1, tm, D), lambda b, i: (b, i, 0)),
        out_shape=jax.ShapeDtypeStruct((B, L, D), F32),
        compiler_params=_params("parallel", "parallel"),
        name="gdn_out",
    )(o, z, x, mod, jnp.tile(norm_g, H).reshape(1, V), w_out.astype(BF16), final_g.reshape(1, D))


def _trunk(x, mod, s5_re0, s5_im0, gdn_s0, gdn_conv0, w):
    B, L, D = x.shape
    G, P = w["s5_lambda_re"].shape[1:]
    n = G * P

    ar, ai, bbrT, bbiT = _s5_disc(w["s5_log_step"][0], w["s5_lambda_re"][0], w["s5_lambda_im"][0],
                                  w["s5_b_re"][0], w["s5_b_im"][0])
    wb, wc = _s5_block_weights(bbrT, bbiT, w["s5_c_re"][0], w["s5_c_im"][0])
    x1, hr, hi = _s5_layer(x, mod[0].transpose(1, 0, 2), w["norm_g"][0], w["s5_w_in"][0], ar, ai,
                           wb, wc, s5_re0[0].reshape(B, n), s5_im0[0].reshape(B, n),
                           w["s5_d"][0], w["s5_w_glu"][0], w["s5_b_glu"][0], w["s5_w_out"][0])

    conv_ch = w["gdn_conv_w"].shape[-1]
    v_width = w["gdn_w_out"].shape[1]
    qkv, z2, bg = _gdn_in(x1, mod[1], w["norm_g"][1], w["gdn_w_in"][0], w["gdn_a_log"][0],
                          w["gdn_dt_bias"][0], conv_ch, v_width)
    o, s_new = _gdn_core(qkv, bg, w["gdn_conv_w"][0], gdn_conv0[0], gdn_s0[0])
    y = _gdn_out(o, z2, x1, mod[1], w["gdn_norm_g"][0], w["gdn_w_out"][0], w["final_g"])
    new_hist = qkv[:, L - (GDN_CONV - 1):, :]
    return (y, hr.reshape(1, B, G, P), hi.reshape(1, B, G, P), s_new[None], new_hist[None])


def kernel(x_prompt, x_sample, c_prompt, c_sample, state_s5_re, state_s5_im, state_gdn, state_gdn_conv, norm_g, w_ada, b_ada, s5_w_in, s5_log_step, s5_lambda_re, s5_lambda_im, s5_b_re, s5_b_im, s5_c_re, s5_c_im, s5_d, s5_w_glu, s5_b_glu, s5_w_out, gdn_w_in, gdn_conv_w, gdn_a_log, gdn_dt_bias, gdn_norm_g, gdn_w_out, final_g):
    w = dict(norm_g=norm_g, s5_w_in=s5_w_in, s5_log_step=s5_log_step, s5_lambda_re=s5_lambda_re,
             s5_lambda_im=s5_lambda_im, s5_b_re=s5_b_re, s5_b_im=s5_b_im, s5_c_re=s5_c_re,
             s5_c_im=s5_c_im, s5_d=s5_d, s5_w_glu=s5_w_glu, s5_b_glu=s5_b_glu, s5_w_out=s5_w_out,
             gdn_w_in=gdn_w_in, gdn_conv_w=gdn_conv_w, gdn_a_log=gdn_a_log,
             gdn_dt_bias=gdn_dt_bias, gdn_norm_g=gdn_norm_g, gdn_w_out=gdn_w_out, final_g=final_g)
    bp, _, d = x_prompt.shape
    bs = x_sample.shape[0]
    depth = w_ada.shape[0]
    mod = _ada_mod(jnp.concatenate([c_prompt, c_sample], axis=0), w_ada, b_ada)
    mod = mod.reshape(depth, bp + bs, 3, d)

    z_s5 = jnp.zeros((state_s5_re.shape[0], bp) + state_s5_re.shape[2:], F32)
    z_gdn = jnp.zeros((state_gdn.shape[0], bp) + state_gdn.shape[2:], F32)
    z_conv = jnp.zeros((state_gdn_conv.shape[0], bp) + state_gdn_conv.shape[2:], F32)
    yp, s5r_p, s5i_p, gdn_p, conv_p = _trunk(x_prompt, mod[:, :bp], z_s5, z_s5, z_gdn, z_conv, w)
    ys, s5r_s, s5i_s, gdn_s, conv_s = _trunk(x_sample, mod[:, bp:], state_s5_re, state_s5_im,
                                             state_gdn, state_gdn_conv, w)
    return (yp, ys, s5r_p, s5i_p, gdn_p, conv_p, s5r_s, s5i_s, gdn_s, conv_s)
```

```python
import functools
import math

import jax
import jax.numpy as jnp
from jax import lax
from jax.experimental import pallas as pl
from jax.experimental.pallas import tpu as pltpu

F32 = jnp.float32
BF16 = jnp.bfloat16
EPS = 1e-6

S5_GROUP = 16
S5_STATE = 64
S5_BLOCK_GROUPS = 8
GDN_DK = 128
GDN_DV = 128
GDN_CONV = 4
GDN_CHUNK = 64
GDN_ROWS_PER_STEP = 2
INV_BLOCK = 16
HIST_ROWS = 8
LANES = 128
VMEM_LIMIT = 56 * 1024 * 1024
HI = lax.Precision.HIGHEST
NT_DIMS = (((1,), (1,)), ((), ()))


def _params(*sem):
    return pltpu.CompilerParams(dimension_semantics=sem, vmem_limit_bytes=VMEM_LIMIT)


def _dot(a, b):
    return jnp.dot(a.astype(BF16), b.astype(BF16), preferred_element_type=F32)


def _row_tile(L):
    return min(512, L)


def _ada_kernel(c_ref, w_ref, b_ref, o_ref):
    c = c_ref[...]
    o_ref[0] = _dot(jax.nn.silu(c), w_ref[0]) + b_ref[0]


def _ada_mod(c_all, w_ada, b_ada):
    depth, d, d3 = w_ada.shape
    r = c_all.shape[0]
    tn = 768
    return pl.pallas_call(
        _ada_kernel,
        grid=(depth, d3 // tn),
        in_specs=[
            pl.BlockSpec((r, d), lambda i, j: (0, 0)),
            pl.BlockSpec((1, d, tn), lambda i, j: (i, 0, j)),
            pl.BlockSpec((1, 1, tn), lambda i, j: (i, 0, j)),
        ],
        out_specs=pl.BlockSpec((1, r, tn), lambda i, j: (i, 0, j)),
        out_shape=jax.ShapeDtypeStruct((depth, r, d3), F32),
        compiler_params=_params("parallel", "parallel"),
        name="ada_mod",
    )(c_all, w_ada.astype(BF16), b_ada.reshape(depth, 1, d3))


def _norm_mod(x, g, m):
    h = x * lax.rsqrt(jnp.mean(x * x, axis=-1, keepdims=True) + EPS) * g
    return h * (1.0 + m[1:2]) + m[0:1]


def _s5_disc_kernel(ls_ref, lr_ref, li_ref, br_ref, bi_ref, ar_ref, ai_ref, bbr_ref, bbi_ref):
    step = jnp.exp(ls_ref[...])
    lr = lr_ref[...]
    li = li_ref[...]
    mag = jnp.exp(lr * step)
    ar = mag * jnp.cos(li * step)
    ai = mag * jnp.sin(li * step)
    den = lr * lr + li * li
    xr = ar - 1.0
    nr = (xr * lr + ai * li) / den
    ni = (ai * lr - xr * li) / den
    br = br_ref[...]
    bi = bi_ref[...]
    ar_ref[...] = ar
    ai_ref[...] = ai
    bbr_ref[...] = nr * br - ni * bi
    bbi_ref[...] = nr * bi + ni * br


def _s5_disc(log_step, lam_re, lam_im, b_re, b_im):
    G, P = lam_re.shape
    n = G * P
    c = b_re.shape[-1]
    ls = jnp.broadcast_to(log_step[:, None], (G, P)).reshape(1, n)
    brT = b_re.transpose(2, 0, 1).reshape(c, n)
    biT = b_im.transpose(2, 0, 1).reshape(c, n)
    vec = jax.ShapeDtypeStruct((1, n), F32)
    mat = jax.ShapeDtypeStruct((c, n), F32)
    return pl.pallas_call(
        _s5_disc_kernel,
        out_shape=[vec, vec, mat, mat],
        name="s5_disc",
    )(ls, lam_re.reshape(1, n), lam_im.reshape(1, n), brT, biT)


def _s5_block_weights(bbrT, bbiT, c_re, c_im):
    nb = S5_BLOCK_GROUPS
    G = c_re.shape[0]
    nblk = G // nb
    eye = jnp.eye(nb, dtype=F32)

    def bdiag_b(bbT):
        t = bbT.reshape(S5_GROUP, nblk, nb, S5_STATE)
        w = jnp.einsum("cjhp,gh->jgchp", t, eye)
        return w.reshape(nblk, nb * S5_GROUP, nb * S5_STATE)

    def bdiag_c(c):
        t = c.reshape(nblk, nb, S5_GROUP, S5_STATE)
        w = jnp.einsum("jgop,gh->jhpgo", t, eye)
        return w.reshape(nblk, nb * S5_STATE, nb * S5_GROUP)

    wb = jnp.concatenate([bdiag_b(bbrT), bdiag_b(bbiT)], axis=2).astype(BF16)
    wc = jnp.concatenate([bdiag_c(c_re), -bdiag_c(c_im)], axis=1).astype(BF16)
    return wb, wc


def _s5_layer_kernel(x_ref, mod_ref, g_ref, win_ref, ar_ref, ai_ref, wb_ref, wc_ref, hr0_ref,
                     hi0_ref, d_ref, wg_ref, bglu_ref, wo_ref, o_ref, hr_ref, hi_ref,
                     u_scr, z_scr, y_scr, bu_scr, xr_scr, xi_scr, *, steps, batch):
    i = pl.program_id(0)
    rows = steps * batch
    d_model = x_ref.shape[-1]
    e = u_scr.shape[-1]
    nblk = wb_ref.shape[0]
    half = wb_ref.shape[2] // 2
    ulanes = wb_ref.shape[1]

    @pl.when(i == 0)
    def _():
        xr_scr[...] = hr0_ref[...]
        xi_scr[...] = hi0_ref[...]

    x3 = jnp.swapaxes(x_ref[...], 0, 1)
    m = mod_ref[...]
    h3 = x3 * lax.rsqrt(jnp.mean(x3 * x3, axis=-1, keepdims=True) + EPS) * g_ref[...]
    h3 = h3 * (1.0 + m[1]) + m[0]
    p = _dot(h3.reshape(rows, d_model), win_ref[...])
    u_scr[...] = p[:, :e]
    z_scr[...] = p[:, e:]

    for j in range(nblk):
        sl = slice(j * half, (j + 1) * half)
        bu_scr[...] = _dot(u_scr[:, j * ulanes:(j + 1) * ulanes], wb_ref[j])
        ar = jnp.broadcast_to(ar_ref[:, sl], (batch, half))
        ai = jnp.broadcast_to(ai_ref[:, sl], (batch, half))

        def body(t, carry):
            xr, xi = carry
            r0 = pl.multiple_of(t * batch, batch)
            bur = bu_scr[pl.ds(r0, batch), 0:half]
            bui = bu_scr[pl.ds(r0, batch), half:2 * half]
            nxr = ar * xr - ai * xi + bur
            nxi = ar * xi + ai * xr + bui
            bu_scr[pl.ds(r0, batch), 0:half] = nxr
            bu_scr[pl.ds(r0, batch), half:2 * half] = nxi
            return nxr, nxi

        xr, xi = lax.fori_loop(0, steps, body, (xr_scr[:, sl], xi_scr[:, sl]), unroll=8)
        xr_scr[:, sl] = xr
        xi_scr[:, sl] = xi
        y_scr[:, j * ulanes:(j + 1) * ulanes] = _dot(bu_scr[...], wc_ref[j])

    y = jax.nn.gelu(y_scr[...] + d_ref[...] * u_scr[...])
    y = y * jax.nn.sigmoid(_dot(y, wg_ref[...]) + bglu_ref[...])
    y = y * jax.nn.silu(z_scr[...])
    out3 = _dot(y, wo_ref[...]).reshape(steps, batch, d_model)
    o_ref[...] = jnp.swapaxes(x3 + m[2] * out3, 0, 1)

    @pl.when(i == pl.num_programs(0) - 1)
    def _():
        hr_ref[...] = xr_scr[...]
        hi_ref[...] = xi_scr[...]


def _s5_layer(x, mod, g, w_in, ar, ai, wb, wc, hr0, hi0, d, w_glu, b_glu, w_out):
    B, L, D = x.shape
    E = w_in.shape[1] // 2
    n = ar.shape[1]
    steps = min(64, L)
    rows = steps * B
    const2 = lambda i: (0, 0)
    const3 = lambda i: (0, 0, 0)

    def resident(shape):
        return pl.BlockSpec(shape, const2 if len(shape) == 2 else const3,
                            pipeline_mode=pl.Buffered(1))

    return pl.pallas_call(
        functools.partial(_s5_layer_kernel, steps=steps, batch=B),
        grid=(L // steps,),
        in_specs=[
            pl.BlockSpec((B, steps, D), lambda i: (0, i, 0)),
            resident((3, B, D)),
            resident((1, D)),
            resident((D, 2 * E)),
            resident((1, n)),
            resident((1, n)),
            resident(wb.shape),
            resident(wc.shape),
            resident((B, n)),
            resident((B, n)),
            resident((1, E)),
            resident((E, E)),
            resident((1, E)),
            resident((E, D)),
        ],
        out_specs=[
            pl.BlockSpec((B, steps, D), lambda i: (0, i, 0)),
            pl.BlockSpec((B, n), const2),
            pl.BlockSpec((B, n), const2),
        ],
        out_shape=[
            jax.ShapeDtypeStruct((B, L, D), F32),
            jax.ShapeDtypeStruct((B, n), F32),
            jax.ShapeDtypeStruct((B, n), F32),
        ],
        scratch_shapes=[
            pltpu.VMEM((rows, E), F32),
            pltpu.VMEM((rows, E), F32),
            pltpu.VMEM((rows, E), F32),
            pltpu.VMEM((rows, wb.shape[2]), F32),
            pltpu.VMEM((B, n), F32),
            pltpu.VMEM((B, n), F32),
        ],
        compiler_params=_params("arbitrary"),
        name="s5_layer",
    )(x, mod, g.reshape(1, D), w_in.astype(BF16), ar, ai, wb, wc, hr0, hi0, d.reshape(1, E),
      w_glu.astype(BF16), b_glu.reshape(1, E), w_out.astype(BF16))


def _l2norm(x):
    return x * lax.rsqrt(jnp.sum(x * x, axis=-1, keepdims=True) + EPS)


def _gdn_chunk(L):
    return GDN_CHUNK if L % GDN_CHUNK == 0 else L


def _split3(x):
    hi = x.astype(BF16)
    r = x - hi.astype(F32)
    mid = r.astype(BF16)
    lo = (r - mid.astype(F32)).astype(BF16)
    return hi, mid, lo


def _gdn_in_kernel(x_ref, mod_ref, g_ref, w_ref, wba_ref, alog_ref, dtb_ref, cw_ref,
                   hist_ref, act_ref, z_ref, bgc_ref, hist_out_ref, ext_scr, tri_scr,
                   *, heads, qk_heads, chunk):
    i = pl.program_id(1)
    tm = x_ref.shape[1]
    qk_width = qk_heads * GDN_DK
    conv_ch = act_ref.shape[-1]
    cb = qk_width

    @pl.when(i == 0)
    def _():
        ext_scr[0:HIST_ROWS, :] = hist_ref[0]
        row = lax.broadcasted_iota(jnp.int32, (tm, tm), 0)
        col = lax.broadcasted_iota(jnp.int32, (tm, tm), 1)
        sh = int(math.log2(chunk))
        tri_scr[...] = ((row >= col) & ((row >> sh) == (col >> sh))).astype(BF16)

    h = _norm_mod(x_ref[0], g_ref[...], mod_ref[0]).astype(BF16)
    cw = cw_ref[...]
    base = HIST_ROWS - (GDN_CONV - 1)

    nblk = conv_ch // cb
    zb = z_ref.shape[-1] // nblk

    def project(nb):
        r = jnp.dot(h, w_ref[:, nb * (cb + zb):(nb + 1) * (cb + zb)], preferred_element_type=F32)
        ext_scr[HIST_ROWS:HIST_ROWS + tm, nb * cb:(nb + 1) * cb] = r[:, :cb]
        z_ref[0, :, nb * zb:(nb + 1) * zb] = r[:, cb:]

    def conv_act(nb):
        cols = slice(nb * cb, (nb + 1) * cb)
        conv = ext_scr[base:base + tm, cols] * cw[0:1, cols]
        for j in range(1, GDN_CONV):
            conv = conv + ext_scr[base + j:base + j + tm, cols] * cw[j:j + 1, cols]
        ext_scr[0:HIST_ROWS, cols] = ext_scr[tm:tm + HIST_ROWS, cols]
        act = jax.nn.silu(conv)
        for hh in range(cb // GDN_DK):
            a = act[:, hh * GDN_DK:(hh + 1) * GDN_DK]
            if nb == 0:
                a = _l2norm(a) * (GDN_DK ** -0.5)
            elif nb == 1:
                a = _l2norm(a)
            act_ref[0, :, nb * cb + hh * GDN_DK:nb * cb + (hh + 1) * GDN_DK] = a

    for nb in range(nblk):
        project(nb)
        conv_act(nb)

    ba = jnp.dot(h, wba_ref[...], preferred_element_type=F32)
    beta = jax.nn.sigmoid(ba)
    a = ba + dtb_ref[...]
    softplus = jnp.maximum(a, 0.0) + jnp.log1p(jnp.exp(-jnp.abs(a)))
    g = -jnp.exp(alog_ref[...]) * softplus
    lane = lax.broadcasted_iota(jnp.int32, ba.shape, 1)
    tri = tri_scr[...]
    g_hi, g_mid, g_lo = _split3(g)
    gc = (jnp.dot(tri, g_hi, preferred_element_type=F32)
          + jnp.dot(tri, g_mid, preferred_element_type=F32)
          + jnp.dot(tri, g_lo, preferred_element_type=F32))
    bgc_ref[0] = jnp.where(lane < heads, beta, gc)

    @pl.when(i == pl.num_programs(1) - 1)
    def _():
        hist_out_ref[0] = ext_scr[0:HIST_ROWS, :]


def _gdn_in(x, mod, g, w_in, a_log, dt_bias, conv_w, hist, v_width):
    B, L, D = x.shape
    H = a_log.shape[0]
    conv_ch = conv_w.shape[-1]
    tm = _row_tile(L)
    o1, o2 = conv_ch, conv_ch + v_width
    wba = jnp.zeros((D, LANES), F32).at[:, :2 * H].set(w_in[:, o2:o2 + 2 * H])
    alog = jnp.zeros((1, LANES), F32).at[0, H:2 * H].set(a_log)
    dtb = jnp.zeros((1, LANES), F32).at[0, H:2 * H].set(dt_bias)
    hist8 = jnp.zeros((B, HIST_ROWS, conv_ch), F32).at[:, HIST_ROWS - (GDN_CONV - 1):, :].set(hist)
    cb = (H // 2) * GDN_DK
    nblk = conv_ch // cb
    zb = v_width // nblk
    w_cat = jnp.concatenate(
        [jnp.concatenate([w_in[:, nb * cb:(nb + 1) * cb], w_in[:, o1 + nb * zb:o1 + (nb + 1) * zb]],
                         axis=1) for nb in range(nblk)], axis=1)
    const = lambda b, i: (0, 0)

    def resident(shape):
        return pl.BlockSpec(shape, const, pipeline_mode=pl.Buffered(1))

    return pl.pallas_call(
        functools.partial(_gdn_in_kernel, heads=H, qk_heads=H // 2, chunk=_gdn_chunk(L)),
        grid=(B, L // tm),
        in_specs=[
            pl.BlockSpec((1, tm, D), lambda b, i: (b, i, 0)),
            pl.BlockSpec((1, 3, D), lambda b, i: (b, 0, 0)),
            resident((1, D)),
            resident((D, conv_ch + v_width)),
            resident((D, LANES)),
            resident((1, LANES)),
            resident((1, LANES)),
            resident((GDN_CONV, conv_ch)),
            pl.BlockSpec((1, HIST_ROWS, conv_ch), lambda b, i: (b, 0, 0)),
        ],
        out_specs=[
            pl.BlockSpec((1, tm, conv_ch), lambda b, i: (b, i, 0)),
            pl.BlockSpec((1, tm, v_width), lambda b, i: (b, i, 0)),
            pl.BlockSpec((1, tm, LANES), lambda b, i: (b, i, 0)),
            pl.BlockSpec((1, HIST_ROWS, conv_ch), lambda b, i: (b, 0, 0)),
        ],
        out_shape=[
            jax.ShapeDtypeStruct((B, L, conv_ch), F32),
            jax.ShapeDtypeStruct((B, L, v_width), F32),
            jax.ShapeDtypeStruct((B, L, LANES), F32),
            jax.ShapeDtypeStruct((B, HIST_ROWS, conv_ch), F32),
        ],
        scratch_shapes=[pltpu.VMEM((HIST_ROWS + tm, conv_ch), F32), pltpu.VMEM((tm, tm), BF16)],
        compiler_params=_params("parallel", "arbitrary"),
        name="gdn_in",
    )(x, mod, g.reshape(1, D), w_cat.astype(BF16), wba.astype(BF16), alog, dtb, conv_w, hist8)


def _gdn_core_kernel(act_ref, bgc_ref, s0_ref, o_ref, sout_ref, s_scr, *, T, qk_heads, v_heads):
    c = pl.program_id(1)
    nrows = act_ref.shape[0]
    qk_width = qk_heads * GDN_DK
    rep = v_heads // qk_heads

    @pl.when(c == 0)
    def _():
        s_scr[...] = s0_ref[...]

    row = lax.broadcasted_iota(jnp.int32, (T, T), 0)
    col = lax.broadcasted_iota(jnp.int32, (T, T), 1)
    causal = row >= col
    strict = row > col
    eye = (lax.broadcasted_iota(jnp.int32, (LANES, LANES), 0)
           == lax.broadcasted_iota(jnp.int32, (LANES, LANES), 1)).astype(BF16)
    bgs, gcts = [], []
    for r in range(nrows):
        bg = bgc_ref[r]
        gc_hi, gc_mid, gc_lo = _split3(bg)
        bgs.append(bg)
        gcts.append(lax.dot_general(eye, gc_hi, NT_DIMS, preferred_element_type=F32)
                    + lax.dot_general(eye, gc_mid, NT_DIMS, preferred_element_type=F32)
                    + lax.dot_general(eye, gc_lo, NT_DIMS, preferred_element_type=F32))
    sh = int(math.log2(INV_BLOCK))
    diag_blk = (row >> sh) == (col >> sh)
    merges = []
    while (1 << sh) < T:
        merges.append(((row >> (sh + 1)) == (col >> (sh + 1))) & ((row >> sh) > (col >> sh)))
        sh += 1

    qkp = [(r, hq) for r in range(nrows) for hq in range(qk_heads)]
    units = [(r, h) for r in range(nrows) for h in range(v_heads)]
    n = range(len(units))
    qk_of = [r * qk_heads + h // rep for r, h in units]
    qs = [act_ref[r, :, hq * GDN_DK:(hq + 1) * GDN_DK] for r, hq in qkp]
    ks = [act_ref[r, :, qk_width + hq * GDN_DK:qk_width + (hq + 1) * GDN_DK] for r, hq in qkp]
    kq = [lax.dot_general(jnp.concatenate([ks[i], qs[i]], axis=0).astype(BF16),
                          ks[i].astype(BF16), NT_DIMS, preferred_element_type=F32)
          for i in range(len(qkp))]
    kks = [kq[i][:T] for i in qk_of]
    qks = [kq[i][T:] for i in qk_of]
    qu = [qs[i] for i in qk_of]
    ku = [ks[i] for i in qk_of]
    beta = [bgs[r][:, h:h + 1] for r, h in units]
    gcol = [bgs[r][:, v_heads + h:v_heads + h + 1] for r, h in units]
    grow = [gcts[r][v_heads + h:v_heads + h + 1, :] for r, h in units]
    decay = [jnp.where(causal, jnp.exp(jnp.where(causal, gcol[u] - grow[u], 0.0)), 0.0) for u in n]
    egc = [jnp.exp(gcol[u]) for u in n]
    m = [jnp.where(strict, (beta[u] * kks[u]) * decay[u], 0.0) for u in n]
    p = [jnp.where(diag_blk, -m[u], 0.0) for u in n]
    e = list(p)
    for _ in range(int(math.log2(INV_BLOCK)) - 1):
        p = [_dot(p[u], p[u]) for u in n]
        e = [e[u] + p[u] + _dot(e[u], p[u]) for u in n]
    for blk in merges:
        m21 = [jnp.where(blk, m[u], 0.0) for u in n]
        y = [m21[u] + _dot(m21[u], e[u]) for u in n]
        e = [e[u] - (y[u] + _dot(e[u], y[u])) for u in n]
    rhs = [jnp.concatenate(
        [act_ref[r, :, 2 * qk_width + h * GDN_DV:2 * qk_width + (h + 1) * GDN_DV] * beta[u],
         ku[u] * (beta[u] * egc[u])], axis=1) for u, (r, h) in enumerate(units)]
    sol = [rhs[u] + _dot(e[u], rhs[u]) for u in n]
    s = [s_scr[r, h] for r, h in units]
    ws = [_dot(jnp.concatenate([sol[u][:, GDN_DV:], qu[u] * egc[u]], axis=0), s[u]) for u in n]
    v_new = [sol[u][:, :GDN_DV] - ws[u][:T] for u in n]
    for u, (r, h) in enumerate(units):
        o_ref[r, :, h * GDN_DV:(h + 1) * GDN_DV] = ws[u][T:] + _dot(qks[u] * decay[u], v_new[u])
    g_last = [grow[u][:, T - 1:T] for u in n]
    for u, (r, h) in enumerate(units):
        kd = ku[u] * jnp.exp(g_last[u] - gcol[u])
        s_scr[r, h] = s[u] * jnp.exp(g_last[u]) + lax.dot_general(
            kd.astype(BF16), v_new[u].astype(BF16), (((0,), (0,)), ((), ())),
            preferred_element_type=F32)

    @pl.when(c == pl.num_programs(1) - 1)
    def _():
        sout_ref[...] = s_scr[...]


def _gdn_core(act, bgc, s0):
    B, L, C = act.shape
    H = s0.shape[1]
    T = _gdn_chunk(L)
    nr = GDN_ROWS_PER_STEP
    return pl.pallas_call(
        functools.partial(_gdn_core_kernel, T=T, qk_heads=H // 2, v_heads=H),
        grid=(B // nr, L // T),
        in_specs=[
            pl.BlockSpec((nr, T, C), lambda b, c: (b, c, 0)),
            pl.BlockSpec((nr, T, LANES), lambda b, c: (b, c, 0)),
            pl.BlockSpec((nr, H, GDN_DK, GDN_DV), lambda b, c: (b, 0, 0, 0)),
        ],
        out_specs=[
            pl.BlockSpec((nr, T, H * GDN_DV), lambda b, c: (b, c, 0)),
            pl.BlockSpec((nr, H, GDN_DK, GDN_DV), lambda b, c: (b, 0, 0, 0)),
        ],
        out_shape=[
            jax.ShapeDtypeStruct((B, L, H * GDN_DV), F32),
            jax.ShapeDtypeStruct((B, H, GDN_DK, GDN_DV), F32),
        ],
        scratch_shapes=[pltpu.VMEM((nr, H, GDN_DK, GDN_DV), F32)],
        compiler_params=_params("parallel", "arbitrary"),
        name="gdn_core",
    )(act, bgc, s0)


def _gdn_out_kernel(o_ref, z_ref, x_ref, mod_ref, ng_ref, wo_ref, fg_ref, y_ref, *, heads):
    o = o_ref[0]
    parts = []
    for h in range(heads):
        oh = o[:, h * GDN_DV:(h + 1) * GDN_DV]
        parts.append(oh * lax.rsqrt(jnp.mean(oh * oh, axis=-1, keepdims=True) + EPS))
    on = jnp.concatenate(parts, axis=1) * ng_ref[...]
    out = _dot(on * jax.nn.silu(z_ref[0]), wo_ref[...])
    x2 = x_ref[0] + mod_ref[0][2:3] * out
    y_ref[0] = x2 * lax.rsqrt(jnp.mean(x2 * x2, axis=-1, keepdims=True) + EPS) * fg_ref[...]


def _gdn_out(o, z, x, mod, norm_g, w_out, final_g):
    B, L, D = x.shape
    V = o.shape[-1]
    H = V // GDN_DV
    tm = _row_tile(L)
    const = lambda b, i: (0, 0)
    return pl.pallas_call(
        functools.partial(_gdn_out_kernel, heads=H),
        grid=(B, L // tm),
        in_specs=[
            pl.BlockSpec((1, tm, V), lambda b, i: (b, i, 0)),
            pl.BlockSpec((1, tm, V), lambda b, i: (b, i, 0)),
            pl.BlockSpec((1, tm, D), lambda b, i: (b, i, 0)),
            pl.BlockSpec((1, 3, D), lambda b, i: (b, 0, 0)),
            pl.BlockSpec((1, V), const),
            pl.BlockSpec((V, D), const),
            pl.BlockSpec((1, D), const),
        ],
        out_specs=pl.BlockSpec((1, tm, D), lambda b, i: (b, i, 0)),
        out_shape=jax.ShapeDtypeStruct((B, L, D), F32),
        compiler_params=_params("parallel", "parallel"),
        name="gdn_out",
    )(o, z, x, mod, jnp.tile(norm_g, H).reshape(1, V), w_out.astype(BF16), final_g.reshape(1, D))


def _trunk(x, mod, s5_re0, s5_im0, gdn_s0, gdn_conv0, w):
    B, L, D = x.shape
    G, P = w["s5_lambda_re"].shape[1:]
    n = G * P

    ar, ai, bbrT, bbiT = _s5_disc(w["s5_log_step"][0], w["s5_lambda_re"][0], w["s5_lambda_im"][0],
                                  w["s5_b_re"][0], w["s5_b_im"][0])
    wb, wc = _s5_block_weights(bbrT, bbiT, w["s5_c_re"][0], w["s5_c_im"][0])
    x1, hr, hi = _s5_layer(x, mod[0].transpose(1, 0, 2), w["norm_g"][0], w["s5_w_in"][0], ar, ai,
                           wb, wc, s5_re0[0].reshape(B, n), s5_im0[0].reshape(B, n),
                           w["s5_d"][0], w["s5_w_glu"][0], w["s5_b_glu"][0], w["s5_w_out"][0])

    v_width = w["gdn_w_out"].shape[1]
    act, z2, bgc, hist8 = _gdn_in(x1, mod[1], w["norm_g"][1], w["gdn_w_in"][0], w["gdn_a_log"][0],
                                  w["gdn_dt_bias"][0], w["gdn_conv_w"][0], gdn_conv0[0], v_width)
    o, s_new = _gdn_core(act, bgc, gdn_s0[0])
    y = _gdn_out(o, z2, x1, mod[1], w["gdn_norm_g"][0], w["gdn_w_out"][0], w["final_g"])
    new_hist = hist8[:, HIST_ROWS - (GDN_CONV - 1):, :]
    return (y, hr.reshape(1, B, G, P), hi.reshape(1, B, G, P), s_new[None], new_hist[None])


def kernel(x_prompt, x_sample, c_prompt, c_sample, state_s5_re, state_s5_im, state_gdn, state_gdn_conv, norm_g, w_ada, b_ada, s5_w_in, s5_log_step, s5_lambda_re, s5_lambda_im, s5_b_re, s5_b_im, s5_c_re, s5_c_im, s5_d, s5_w_glu, s5_b_glu, s5_w_out, gdn_w_in, gdn_conv_w, gdn_a_log, gdn_dt_bias, gdn_norm_g, gdn_w_out, final_g):
    w = dict(norm_g=norm_g, s5_w_in=s5_w_in, s5_log_step=s5_log_step, s5_lambda_re=s5_lambda_re,
             s5_lambda_im=s5_lambda_im, s5_b_re=s5_b_re, s5_b_im=s5_b_im, s5_c_re=s5_c_re,
             s5_c_im=s5_c_im, s5_d=s5_d, s5_w_glu=s5_w_glu, s5_b_glu=s5_b_glu, s5_w_out=s5_w_out,
             gdn_w_in=gdn_w_in, gdn_conv_w=gdn_conv_w, gdn_a_log=gdn_a_log,
             gdn_dt_bias=gdn_dt_bias, gdn_norm_g=gdn_norm_g, gdn_w_out=gdn_w_out, final_g=final_g)
    bp, _, d = x_prompt.shape
    bs = x_sample.shape[0]
    depth = w_ada.shape[0]
    mod = _ada_mod(jnp.concatenate([c_prompt, c_sample], axis=0), w_ada, b_ada)
    mod = mod.reshape(depth, bp + bs, 3, d)

    z_s5 = jnp.zeros((state_s5_re.shape[0], bp) + state_s5_re.shape[2:], F32)
    z_gdn = jnp.zeros((state_gdn.shape[0], bp) + state_gdn.shape[2:], F32)
    z_conv = jnp.zeros((state_gdn_conv.shape[0], bp) + state_gdn_conv.shape[2:], F32)
    yp, s5r_p, s5i_p, gdn_p, conv_p = _trunk(x_prompt, mod[:, :bp], z_s5, z_s5, z_gdn, z_conv, w)
    ys, s5r_s, s5i_s, gdn_s, conv_s = _trunk(x_sample, mod[:, bp:], state_s5_re, state_s5_im,
                                             state_gdn, state_gdn_conv, w)
    return (yp, ys, s5r_p, s5i_p, gdn_p, conv_p, s5r_s, s5i_s, gdn_s, conv_s)
```

```python
import functools
import math

import jax
import jax.numpy as jnp
from jax import lax
from jax.experimental import pallas as pl
from jax.experimental.pallas import tpu as pltpu

F32 = jnp.float32
BF16 = jnp.bfloat16
EPS = 1e-6

S5_GROUP = 16
S5_STATE = 64
S5_BLOCK_GROUPS = 8
GDN_DK = 128
GDN_DV = 128
GDN_CONV = 4
GDN_CHUNK = 64
GDN_ROWS_PER_STEP = 2
INV_BLOCK = 16
HIST_ROWS = 8
LANES = 128
VMEM_LIMIT = 56 * 1024 * 1024
HI = lax.Precision.HIGHEST
NT_DIMS = (((1,), (1,)), ((), ()))


def _params(*sem):
    return pltpu.CompilerParams(dimension_semantics=sem, vmem_limit_bytes=VMEM_LIMIT)


def _dot(a, b):
    return jnp.dot(a.astype(BF16), b.astype(BF16), preferred_element_type=F32)


def _row_tile(L):
    return min(512, L)


def _ada_kernel(c_ref, w_ref, b_ref, o_ref):
    c = c_ref[...]
    o_ref[0] = _dot(jax.nn.silu(c), w_ref[0]) + b_ref[0]


def _ada_mod(c_all, w_ada, b_ada):
    depth, d, d3 = w_ada.shape
    r = c_all.shape[0]
    tn = 768
    return pl.pallas_call(
        _ada_kernel,
        grid=(depth, d3 // tn),
        in_specs=[
            pl.BlockSpec((r, d), lambda i, j: (0, 0)),
            pl.BlockSpec((1, d, tn), lambda i, j: (i, 0, j)),
            pl.BlockSpec((1, 1, tn), lambda i, j: (i, 0, j)),
        ],
        out_specs=pl.BlockSpec((1, r, tn), lambda i, j: (i, 0, j)),
        out_shape=jax.ShapeDtypeStruct((depth, r, d3), F32),
        compiler_params=_params("parallel", "parallel"),
        name="ada_mod",
    )(c_all, w_ada.astype(BF16), b_ada.reshape(depth, 1, d3))


def _norm_mod(x, g, m):
    h = x * lax.rsqrt(jnp.mean(x * x, axis=-1, keepdims=True) + EPS) * g
    return h * (1.0 + m[1:2]) + m[0:1]


def _s5_disc_kernel(ls_ref, lr_ref, li_ref, br_ref, bi_ref, ar_ref, ai_ref, bbr_ref, bbi_ref):
    step = jnp.exp(ls_ref[...])
    lr = lr_ref[...]
    li = li_ref[...]
    mag = jnp.exp(lr * step)
    ar = mag * jnp.cos(li * step)
    ai = mag * jnp.sin(li * step)
    den = lr * lr + li * li
    xr = ar - 1.0
    nr = (xr * lr + ai * li) / den
    ni = (ai * lr - xr * li) / den
    br = br_ref[...]
    bi = bi_ref[...]
    ar_ref[...] = ar
    ai_ref[...] = ai
    bbr_ref[...] = nr * br - ni * bi
    bbi_ref[...] = nr * bi + ni * br


def _s5_disc(log_step, lam_re, lam_im, b_re, b_im):
    G, P = lam_re.shape
    n = G * P
    c = b_re.shape[-1]
    ls = jnp.broadcast_to(log_step[:, None], (G, P)).reshape(1, n)
    brT = b_re.transpose(2, 0, 1).reshape(c, n)
    biT = b_im.transpose(2, 0, 1).reshape(c, n)
    vec = jax.ShapeDtypeStruct((1, n), F32)
    mat = jax.ShapeDtypeStruct((c, n), F32)
    return pl.pallas_call(
        _s5_disc_kernel,
        out_shape=[vec, vec, mat, mat],
        name="s5_disc",
    )(ls, lam_re.reshape(1, n), lam_im.reshape(1, n), brT, biT)


def _s5_block_weights(bbrT, bbiT, c_re, c_im):
    nb = S5_BLOCK_GROUPS
    G = c_re.shape[0]
    nblk = G // nb
    eye = jnp.eye(nb, dtype=F32)

    def bdiag_b(bbT):
        t = bbT.reshape(S5_GROUP, nblk, nb, S5_STATE)
        w = jnp.einsum("cjhp,gh->jgchp", t, eye)
        return w.reshape(nblk, nb * S5_GROUP, nb * S5_STATE)

    def bdiag_c(c):
        t = c.reshape(nblk, nb, S5_GROUP, S5_STATE)
        w = jnp.einsum("jgop,gh->jhpgo", t, eye)
        return w.reshape(nblk, nb * S5_STATE, nb * S5_GROUP)

    wb = jnp.concatenate([bdiag_b(bbrT), bdiag_b(bbiT)], axis=2).astype(BF16)
    wc = jnp.concatenate([bdiag_c(c_re), -bdiag_c(c_im)], axis=1).astype(BF16)
    return wb, wc


def _s5_layer_kernel(x_ref, mod_ref, g_ref, win_ref, ar_ref, ai_ref, wb_ref, wc_ref, hr0_ref,
                     hi0_ref, d_ref, wg_ref, bglu_ref, wo_ref, o_ref, hr_ref, hi_ref,
                     u_scr, z_scr, y_scr, bu_scr, xr_scr, xi_scr, *, steps, batch):
    i = pl.program_id(0)
    rows = steps * batch
    d_model = x_ref.shape[-1]
    e = u_scr.shape[-1]
    nblk = wb_ref.shape[0]
    half = wb_ref.shape[2] // 2
    ulanes = wb_ref.shape[1]

    @pl.when(i == 0)
    def _():
        xr_scr[...] = hr0_ref[...]
        xi_scr[...] = hi0_ref[...]

    x3 = jnp.swapaxes(x_ref[...], 0, 1)
    m = mod_ref[...]
    h3 = x3 * lax.rsqrt(jnp.mean(x3 * x3, axis=-1, keepdims=True) + EPS) * g_ref[...]
    h3 = h3 * (1.0 + m[1]) + m[0]
    p = _dot(h3.reshape(rows, d_model), win_ref[...])
    u_scr[...] = p[:, :e]
    z_scr[...] = p[:, e:]

    for j in range(nblk):
        sl = slice(j * half, (j + 1) * half)
        buf = bu_scr.at[j % 2]
        buf[...] = _dot(u_scr[:, j * ulanes:(j + 1) * ulanes], wb_ref[j])
        ar = jnp.broadcast_to(ar_ref[:, sl], (batch, half))
        ai = jnp.broadcast_to(ai_ref[:, sl], (batch, half))
        xr = xr_scr[:, sl]
        xi = xi_scr[:, sl]
        for t in range(steps):
            rows_t = slice(t * batch, (t + 1) * batch)
            xr, xi = (ar * xr - ai * xi + buf[rows_t, 0:half],
                      ar * xi + ai * xr + buf[rows_t, half:2 * half])
            buf[rows_t, 0:half] = xr
            buf[rows_t, half:2 * half] = xi
        xr_scr[:, sl] = xr
        xi_scr[:, sl] = xi
        y_scr[:, j * ulanes:(j + 1) * ulanes] = _dot(buf[...], wc_ref[j])

    y = jax.nn.gelu(y_scr[...] + d_ref[...] * u_scr[...])
    y = y * jax.nn.sigmoid(_dot(y, wg_ref[...]) + bglu_ref[...])
    y = y * jax.nn.silu(z_scr[...])
    out3 = _dot(y, wo_ref[...]).reshape(steps, batch, d_model)
    o_ref[...] = jnp.swapaxes(x3 + m[2] * out3, 0, 1)

    @pl.when(i == pl.num_programs(0) - 1)
    def _():
        hr_ref[...] = xr_scr[...]
        hi_ref[...] = xi_scr[...]


def _s5_layer(x, mod, g, w_in, ar, ai, wb, wc, hr0, hi0, d, w_glu, b_glu, w_out):
    B, L, D = x.shape
    E = w_in.shape[1] // 2
    n = ar.shape[1]
    steps = min(64, L)
    rows = steps * B
    const2 = lambda i: (0, 0)
    const3 = lambda i: (0, 0, 0)

    def resident(shape):
        return pl.BlockSpec(shape, const2 if len(shape) == 2 else const3,
                            pipeline_mode=pl.Buffered(1))

    return pl.pallas_call(
        functools.partial(_s5_layer_kernel, steps=steps, batch=B),
        grid=(L // steps,),
        in_specs=[
            pl.BlockSpec((B, steps, D), lambda i: (0, i, 0)),
            resident((3, B, D)),
            resident((1, D)),
            resident((D, 2 * E)),
            resident((1, n)),
            resident((1, n)),
            resident(wb.shape),
            resident(wc.shape),
            resident((B, n)),
            resident((B, n)),
            resident((1, E)),
            resident((E, E)),
            resident((1, E)),
            resident((E, D)),
        ],
        out_specs=[
            pl.BlockSpec((B, steps, D), lambda i: (0, i, 0)),
            pl.BlockSpec((B, n), const2),
            pl.BlockSpec((B, n), const2),
        ],
        out_shape=[
            jax.ShapeDtypeStruct((B, L, D), F32),
            jax.ShapeDtypeStruct((B, n), F32),
            jax.ShapeDtypeStruct((B, n), F32),
        ],
        scratch_shapes=[
            pltpu.VMEM((rows, E), F32),
            pltpu.VMEM((rows, E), F32),
            pltpu.VMEM((rows, E), F32),
            pltpu.VMEM((2, rows, wb.shape[2]), F32),
            pltpu.VMEM((B, n), F32),
            pltpu.VMEM((B, n), F32),
        ],
        compiler_params=_params("arbitrary"),
        name="s5_layer",
    )(x, mod, g.reshape(1, D), w_in.astype(BF16), ar, ai, wb, wc, hr0, hi0, d.reshape(1, E),
      w_glu.astype(BF16), b_glu.reshape(1, E), w_out.astype(BF16))


def _l2norm(x):
    return x * lax.rsqrt(jnp.sum(x * x, axis=-1, keepdims=True) + EPS)


def _gdn_chunk(L):
    return GDN_CHUNK if L % GDN_CHUNK == 0 else L


def _split3(x):
    hi = x.astype(BF16)
    r = x - hi.astype(F32)
    mid = r.astype(BF16)
    lo = (r - mid.astype(F32)).astype(BF16)
    return hi, mid, lo


def _gdn_in_kernel(x_ref, mod_ref, g_ref, w_ref, wba_ref, alog_ref, dtb_ref, cw_ref,
                   hist_ref, act_ref, z_ref, bgc_ref, hist_out_ref, ext_scr, tri_scr,
                   *, heads, qk_heads, chunk):
    i = pl.program_id(1)
    tm = x_ref.shape[1]
    qk_width = qk_heads * GDN_DK
    conv_ch = act_ref.shape[-1]
    cb = qk_width

    @pl.when(i == 0)
    def _():
        ext_scr[0:HIST_ROWS, :] = hist_ref[0]
        row = lax.broadcasted_iota(jnp.int32, (tm, tm), 0)
        col = lax.broadcasted_iota(jnp.int32, (tm, tm), 1)
        sh = int(math.log2(chunk))
        tri_scr[...] = ((row >= col) & ((row >> sh) == (col >> sh))).astype(BF16)

    h = _norm_mod(x_ref[0], g_ref[...], mod_ref[0]).astype(BF16)
    cw = cw_ref[...]
    base = HIST_ROWS - (GDN_CONV - 1)

    nblk = conv_ch // cb
    zb = z_ref.shape[-1] // nblk

    def project(nb):
        r = jnp.dot(h, w_ref[:, nb * (cb + zb):(nb + 1) * (cb + zb)], preferred_element_type=F32)
        ext_scr[HIST_ROWS:HIST_ROWS + tm, nb * cb:(nb + 1) * cb] = r[:, :cb]
        z_ref[0, :, nb * zb:(nb + 1) * zb] = r[:, cb:]

    def conv_act(nb):
        cols = slice(nb * cb, (nb + 1) * cb)
        conv = ext_scr[base:base + tm, cols] * cw[0:1, cols]
        for j in range(1, GDN_CONV):
            conv = conv + ext_scr[base + j:base + j + tm, cols] * cw[j:j + 1, cols]
        ext_scr[0:HIST_ROWS, cols] = ext_scr[tm:tm + HIST_ROWS, cols]
        act = jax.nn.silu(conv)
        for hh in range(cb // GDN_DK):
            a = act[:, hh * GDN_DK:(hh + 1) * GDN_DK]
            if nb == 0:
                a = _l2norm(a) * (GDN_DK ** -0.5)
            elif nb == 1:
                a = _l2norm(a)
            act_ref[0, :, nb * cb + hh * GDN_DK:nb * cb + (hh + 1) * GDN_DK] = a

    for nb in range(nblk):
        project(nb)
        conv_act(nb)

    ba = jnp.dot(h, wba_ref[...], preferred_element_type=F32)
    beta = jax.nn.sigmoid(ba)
    a = ba + dtb_ref[...]
    softplus = jnp.maximum(a, 0.0) + jnp.log1p(jnp.exp(-jnp.abs(a)))
    g = -jnp.exp(alog_ref[...]) * softplus
    lane = lax.broadcasted_iota(jnp.int32, ba.shape, 1)
    tri = tri_scr[...]
    g_hi, g_mid, g_lo = _split3(g)
    gc = (jnp.dot(tri, g_hi, preferred_element_type=F32)
          + jnp.dot(tri, g_mid, preferred_element_type=F32)
          + jnp.dot(tri, g_lo, preferred_element_type=F32))
    bgc_ref[0] = jnp.where(lane < heads, beta, gc)

    @pl.when(i == pl.num_programs(1) - 1)
    def _():
        hist_out_ref[0] = ext_scr[0:HIST_ROWS, :]


def _gdn_in(x, mod, g, w_in, a_log, dt_bias, conv_w, hist, v_width):
    B, L, D = x.shape
    H = a_log.shape[0]
    conv_ch = conv_w.shape[-1]
    tm = _row_tile(L)
    o1, o2 = conv_ch, conv_ch + v_width
    wba = jnp.zeros((D, LANES), F32).at[:, :2 * H].set(w_in[:, o2:o2 + 2 * H])
    alog = jnp.zeros((1, LANES), F32).at[0, H:2 * H].set(a_log)
    dtb = jnp.zeros((1, LANES), F32).at[0, H:2 * H].set(dt_bias)
    hist8 = jnp.zeros((B, HIST_ROWS, conv_ch), F32).at[:, HIST_ROWS - (GDN_CONV - 1):, :].set(hist)
    cb = (H // 2) * GDN_DK
    nblk = conv_ch // cb
    zb = v_width // nblk
    w_cat = jnp.concatenate(
        [jnp.concatenate([w_in[:, nb * cb:(nb + 1) * cb], w_in[:, o1 + nb * zb:o1 + (nb + 1) * zb]],
                         axis=1) for nb in range(nblk)], axis=1)
    const = lambda b, i: (0, 0)

    def resident(shape):
        return pl.BlockSpec(shape, const, pipeline_mode=pl.Buffered(1))

    return pl.pallas_call(
        functools.partial(_gdn_in_kernel, heads=H, qk_heads=H // 2, chunk=_gdn_chunk(L)),
        grid=(B, L // tm),
        in_specs=[
            pl.BlockSpec((1, tm, D), lambda b, i: (b, i, 0)),
            pl.BlockSpec((1, 3, D), lambda b, i: (b, 0, 0)),
            resident((1, D)),
            resident((D, conv_ch + v_width)),
            resident((D, LANES)),
            resident((1, LANES)),
            resident((1, LANES)),
            resident((GDN_CONV, conv_ch)),
            pl.BlockSpec((1, HIST_ROWS, conv_ch), lambda b, i: (b, 0, 0)),
        ],
        out_specs=[
            pl.BlockSpec((1, tm, conv_ch), lambda b, i: (b, i, 0)),
            pl.BlockSpec((1, tm, v_width), lambda b, i: (b, i, 0)),
            pl.BlockSpec((1, tm, LANES), lambda b, i: (b, i, 0)),
            pl.BlockSpec((1, HIST_ROWS, conv_ch), lambda b, i: (b, 0, 0)),
        ],
        out_shape=[
            jax.ShapeDtypeStruct((B, L, conv_ch), F32),
            jax.ShapeDtypeStruct((B, L, v_width), F32),
            jax.ShapeDtypeStruct((B, L, LANES), F32),
            jax.ShapeDtypeStruct((B, HIST_ROWS, conv_ch), F32),
        ],
        scratch_shapes=[pltpu.VMEM((HIST_ROWS + tm, conv_ch), F32), pltpu.VMEM((tm, tm), BF16)],
        compiler_params=_params("parallel", "arbitrary"),
        name="gdn_in",
    )(x, mod, g.reshape(1, D), w_cat.astype(BF16), wba.astype(BF16), alog, dtb, conv_w, hist8)


def _gdn_core_kernel(act_ref, bgc_ref, s0_ref, o_ref, sout_ref, s_scr, *, T, qk_heads, v_heads):
    c = pl.program_id(1)
    nrows = act_ref.shape[0]
    qk_width = qk_heads * GDN_DK
    rep = v_heads // qk_heads

    @pl.when(c == 0)
    def _():
        s_scr[...] = s0_ref[...]

    row = lax.broadcasted_iota(jnp.int32, (T, T), 0)
    col = lax.broadcasted_iota(jnp.int32, (T, T), 1)
    causal = row >= col
    strict = row > col
    eye = (lax.broadcasted_iota(jnp.int32, (LANES, LANES), 0)
           == lax.broadcasted_iota(jnp.int32, (LANES, LANES), 1)).astype(BF16)
    bgs, gcts = [], []
    for r in range(nrows):
        bg = bgc_ref[r]
        gc_hi, gc_mid, gc_lo = _split3(bg)
        bgs.append(bg)
        gcts.append(lax.dot_general(eye, gc_hi, NT_DIMS, preferred_element_type=F32)
                    + lax.dot_general(eye, gc_mid, NT_DIMS, preferred_element_type=F32)
                    + lax.dot_general(eye, gc_lo, NT_DIMS, preferred_element_type=F32))
    sh = int(math.log2(INV_BLOCK))
    diag_blk = (row >> sh) == (col >> sh)
    merges = []
    while (1 << sh) < T:
        merges.append(((row >> (sh + 1)) == (col >> (sh + 1))) & ((row >> sh) > (col >> sh)))
        sh += 1

    qkp = [(r, hq) for r in range(nrows) for hq in range(qk_heads)]
    units = [(r, h) for r in range(nrows) for h in range(v_heads)]
    n = range(len(units))
    qk_of = [r * qk_heads + h // rep for r, h in units]
    qs = [act_ref[r, :, hq * GDN_DK:(hq + 1) * GDN_DK] for r, hq in qkp]
    ks = [act_ref[r, :, qk_width + hq * GDN_DK:qk_width + (hq + 1) * GDN_DK] for r, hq in qkp]
    kq = [lax.dot_general(jnp.concatenate([ks[i], qs[i]], axis=0).astype(BF16),
                          ks[i].astype(BF16), NT_DIMS, preferred_element_type=F32)
          for i in range(len(qkp))]
    kks = [kq[i][:T] for i in qk_of]
    qks = [kq[i][T:] for i in qk_of]
    qu = [qs[i] for i in qk_of]
    ku = [ks[i] for i in qk_of]
    beta = [bgs[r][:, h:h + 1] for r, h in units]
    gcol = [bgs[r][:, v_heads + h:v_heads + h + 1] for r, h in units]
    grow = [gcts[r][v_heads + h:v_heads + h + 1, :] for r, h in units]
    decay = [jnp.where(causal, jnp.exp(jnp.where(causal, gcol[u] - grow[u], 0.0)), 0.0) for u in n]
    egc = [jnp.exp(gcol[u]) for u in n]
    m = [jnp.where(strict, (beta[u] * kks[u]) * decay[u], 0.0) for u in n]
    p = [jnp.where(diag_blk, -m[u], 0.0) for u in n]
    e = list(p)
    for _ in range(int(math.log2(INV_BLOCK)) - 1):
        p = [_dot(p[u], p[u]) for u in n]
        e = [e[u] + p[u] + _dot(e[u], p[u]) for u in n]
    for blk in merges:
        m21 = [jnp.where(blk, m[u], 0.0) for u in n]
        y = [m21[u] + _dot(m21[u], e[u]) for u in n]
        e = [e[u] - (y[u] + _dot(e[u], y[u])) for u in n]
    rhs = [jnp.concatenate(
        [act_ref[r, :, 2 * qk_width + h * GDN_DV:2 * qk_width + (h + 1) * GDN_DV] * beta[u],
         ku[u] * (beta[u] * egc[u])], axis=1) for u, (r, h) in enumerate(units)]
    sol = [rhs[u] + _dot(e[u], rhs[u]) for u in n]
    s = [s_scr[r, h] for r, h in units]
    ws = [_dot(jnp.concatenate([sol[u][:, GDN_DV:], qu[u] * egc[u]], axis=0), s[u]) for u in n]
    v_new = [sol[u][:, :GDN_DV] - ws[u][:T] for u in n]
    for u, (r, h) in enumerate(units):
        o_ref[r, :, h * GDN_DV:(h + 1) * GDN_DV] = ws[u][T:] + _dot(qks[u] * decay[u], v_new[u])
    g_last = [grow[u][:, T - 1:T] for u in n]
    for u, (r, h) in enumerate(units):
        kd = ku[u] * jnp.exp(g_last[u] - gcol[u])
        s_scr[r, h] = s[u] * jnp.exp(g_last[u]) + lax.dot_general(
            kd.astype(BF16), v_new[u].astype(BF16), (((0,), (0,)), ((), ())),
            preferred_element_type=F32)

    @pl.when(c == pl.num_programs(1) - 1)
    def _():
        sout_ref[...] = s_scr[...]


def _gdn_core(act, bgc, s0):
    B, L, C = act.shape
    H = s0.shape[1]
    T = _gdn_chunk(L)
    nr = GDN_ROWS_PER_STEP
    return pl.pallas_call(
        functools.partial(_gdn_core_kernel, T=T, qk_heads=H // 2, v_heads=H),
        grid=(B // nr, L // T),
        in_specs=[
            pl.BlockSpec((nr, T, C), lambda b, c: (b, c, 0)),
            pl.BlockSpec((nr, T, LANES), lambda b, c: (b, c, 0)),
            pl.BlockSpec((nr, H, GDN_DK, GDN_DV), lambda b, c: (b, 0, 0, 0)),
        ],
        out_specs=[
            pl.BlockSpec((nr, T, H * GDN_DV), lambda b, c: (b, c, 0)),
            pl.BlockSpec((nr, H, GDN_DK, GDN_DV), lambda b, c: (b, 0, 0, 0)),
        ],
        out_shape=[
            jax.ShapeDtypeStruct((B, L, H * GDN_DV), F32),
            jax.ShapeDtypeStruct((B, H, GDN_DK, GDN_DV), F32),
        ],
        scratch_shapes=[pltpu.VMEM((nr, H, GDN_DK, GDN_DV), F32)],
        compiler_params=_params("parallel", "arbitrary"),
        name="gdn_core",
    )(act, bgc, s0)


def _gdn_out_kernel(o_ref, z_ref, x_ref, mod_ref, ng_ref, wo_ref, fg_ref, y_ref, *, heads):
    o = o_ref[0]
    parts = []
    for h in range(heads):
        oh = o[:, h * GDN_DV:(h + 1) * GDN_DV]
        parts.append(oh * lax.rsqrt(jnp.mean(oh * oh, axis=-1, keepdims=True) + EPS))
    on = jnp.concatenate(parts, axis=1) * ng_ref[...]
    out = _dot(on * jax.nn.silu(z_ref[0]), wo_ref[...])
    x2 = x_ref[0] + mod_ref[0][2:3] * out
    y_ref[0] = x2 * lax.rsqrt(jnp.mean(x2 * x2, axis=-1, keepdims=True) + EPS) * fg_ref[...]


def _gdn_out(o, z, x, mod, norm_g, w_out, final_g):
    B, L, D = x.shape
    V = o.shape[-1]
    H = V // GDN_DV
    tm = _row_tile(L)
    const = lambda b, i: (0, 0)
    return pl.pallas_call(
        functools.partial(_gdn_out_kernel, heads=H),
        grid=(B, L // tm),
        in_specs=[
            pl.BlockSpec((1, tm, V), lambda b, i: (b, i, 0)),
            pl.BlockSpec((1, tm, V), lambda b, i: (b, i, 0)),
            pl.BlockSpec((1, tm, D), lambda b, i: (b, i, 0)),
            pl.BlockSpec((1, 3, D), lambda b, i: (b, 0, 0)),
            pl.BlockSpec((1, V), const),
            pl.BlockSpec((V, D), const),
            pl.BlockSpec((1, D), const),
        ],
        out_specs=pl.BlockSpec((1, tm, D), lambda b, i: (b, i, 0)),
        out_shape=jax.ShapeDtypeStruct((B, L, D), F32),
        compiler_params=_params("parallel", "parallel"),
        name="gdn_out",
    )(o, z, x, mod, jnp.tile(norm_g, H).reshape(1, V), w_out.astype(BF16), final_g.reshape(1, D))


def _trunk(x, mod, s5_re0, s5_im0, gdn_s0, gdn_conv0, w):
    B, L, D = x.shape
    G, P = w["s5_lambda_re"].shape[1:]
    n = G * P

    ar, ai, bbrT, bbiT = _s5_disc(w["s5_log_step"][0], w["s5_lambda_re"][0], w["s5_lambda_im"][0],
                                  w["s5_b_re"][0], w["s5_b_im"][0])
    wb, wc = _s5_block_weights(bbrT, bbiT, w["s5_c_re"][0], w["s5_c_im"][0])
    x1, hr, hi = _s5_layer(x, mod[0].transpose(1, 0, 2), w["norm_g"][0], w["s5_w_in"][0], ar, ai,
                           wb, wc, s5_re0[0].reshape(B, n), s5_im0[0].reshape(B, n),
                           w["s5_d"][0], w["s5_w_glu"][0], w["s5_b_glu"][0], w["s5_w_out"][0])

    v_width = w["gdn_w_out"].shape[1]
    act, z2, bgc, hist8 = _gdn_in(x1, mod[1], w["norm_g"][1], w["gdn_w_in"][0], w["gdn_a_log"][0],
                                  w["gdn_dt_bias"][0], w["gdn_conv_w"][0], gdn_conv0[0], v_width)
    o, s_new = _gdn_core(act, bgc, gdn_s0[0])
    y = _gdn_out(o, z2, x1, mod[1], w["gdn_norm_g"][0], w["gdn_w_out"][0], w["final_g"])
    new_hist = hist8[:, HIST_ROWS - (GDN_CONV - 1):, :]
    return (y, hr.reshape(1, B, G, P), hi.reshape(1, B, G, P), s_new[None], new_hist[None])


def kernel(x_prompt, x_sample, c_prompt, c_sample, state_s5_re, state_s5_im, state_gdn, state_gdn_conv, norm_g, w_ada, b_ada, s5_w_in, s5_log_step, s5_lambda_re, s5_lambda_im, s5_b_re, s5_b_im, s5_c_re, s5_c_im, s5_d, s5_w_glu, s5_b_glu, s5_w_out, gdn_w_in, gdn_conv_w, gdn_a_log, gdn_dt_bias, gdn_norm_g, gdn_w_out, final_g):
    w = dict(norm_g=norm_g, s5_w_in=s5_w_in, s5_log_step=s5_log_step, s5_lambda_re=s5_lambda_re,
             s5_lambda_im=s5_lambda_im, s5_b_re=s5_b_re, s5_b_im=s5_b_im, s5_c_re=s5_c_re,
             s5_c_im=s5_c_im, s5_d=s5_d, s5_w_glu=s5_w_glu, s5_b_glu=s5_b_glu, s5_w_out=s5_w_out,
             gdn_w_in=gdn_w_in, gdn_conv_w=gdn_conv_w, gdn_a_log=gdn_a_log,
             gdn_dt_bias=gdn_dt_bias, gdn_norm_g=gdn_norm_g, gdn_w_out=gdn_w_out, final_g=final_g)
    bp, _, d = x_prompt.shape
    bs = x_sample.shape[0]
    depth = w_ada.shape[0]
    mod = _ada_mod(jnp.concatenate([c_prompt, c_sample], axis=0), w_ada, b_ada)
    mod = mod.reshape(depth, bp + bs, 3, d)

    z_s5 = jnp.zeros((state_s5_re.shape[0], bp) + state_s5_re.shape[2:], F32)
    z_gdn = jnp.zeros((state_gdn.shape[0], bp) + state_gdn.shape[2:], F32)
    z_conv = jnp.zeros((state_gdn_conv.shape[0], bp) + state_gdn_conv.shape[2:], F32)
    yp, s5r_p, s5i_p, gdn_p, conv_p = _trunk(x_prompt, mod[:, :bp], z_s5, z_s5, z_gdn, z_conv, w)
    ys, s5r_s, s5i_s, gdn_s, conv_s = _trunk(x_sample, mod[:, bp:], state_s5_re, state_s5_im,
                                             state_gdn, state_gdn_conv, w)
    return (yp, ys, s5r_p, s5i_p, gdn_p, conv_p, s5r_s, s5i_s, gdn_s, conv_s)
```

```python
import functools
import math

import jax
import jax.numpy as jnp
from jax import lax
from jax.experimental import pallas as pl
from jax.experimental.pallas import tpu as pltpu

F32 = jnp.float32
BF16 = jnp.bfloat16
EPS = 1e-6

S5_GROUP = 16
S5_STATE = 64
S5_BLOCK_GROUPS = 8
GDN_DK = 128
GDN_DV = 128
GDN_CONV = 4
GDN_CHUNK = 64
GDN_ROWS_PER_STEP = 2
INV_BLOCK = 16
HIST_ROWS = 8
LANES = 128
VMEM_LIMIT = 56 * 1024 * 1024
HI = lax.Precision.HIGHEST
NT_DIMS = (((1,), (1,)), ((), ()))


def _params(*sem):
    return pltpu.CompilerParams(dimension_semantics=sem, vmem_limit_bytes=VMEM_LIMIT)


def _dot(a, b):
    return jnp.dot(a.astype(BF16), b.astype(BF16), preferred_element_type=F32)


def _row_tile(L):
    return min(512, L)


def _ada_kernel(c_ref, w_ref, b_ref, o_ref):
    c = c_ref[...]
    o_ref[0] = _dot(jax.nn.silu(c), w_ref[0]) + b_ref[0]


def _ada_mod(c_all, w_ada, b_ada):
    depth, d, d3 = w_ada.shape
    r = c_all.shape[0]
    tn = 768
    return pl.pallas_call(
        _ada_kernel,
        grid=(depth, d3 // tn),
        in_specs=[
            pl.BlockSpec((r, d), lambda i, j: (0, 0)),
            pl.BlockSpec((1, d, tn), lambda i, j: (i, 0, j)),
            pl.BlockSpec((1, 1, tn), lambda i, j: (i, 0, j)),
        ],
        out_specs=pl.BlockSpec((1, r, tn), lambda i, j: (i, 0, j)),
        out_shape=jax.ShapeDtypeStruct((depth, r, d3), F32),
        compiler_params=_params("parallel", "parallel"),
        name="ada_mod",
    )(c_all, w_ada.astype(BF16), b_ada.reshape(depth, 1, d3))


def _norm_mod(x, g, m):
    h = x * lax.rsqrt(jnp.mean(x * x, axis=-1, keepdims=True) + EPS) * g
    return h * (1.0 + m[1:2]) + m[0:1]


def _s5_disc_kernel(ls_ref, lr_ref, li_ref, br_ref, bi_ref, ar_ref, ai_ref, bbr_ref, bbi_ref):
    step = jnp.exp(ls_ref[...])
    lr = lr_ref[...]
    li = li_ref[...]
    mag = jnp.exp(lr * step)
    ar = mag * jnp.cos(li * step)
    ai = mag * jnp.sin(li * step)
    den = lr * lr + li * li
    xr = ar - 1.0
    nr = (xr * lr + ai * li) / den
    ni = (ai * lr - xr * li) / den
    br = br_ref[...]
    bi = bi_ref[...]
    ar_ref[...] = ar
    ai_ref[...] = ai
    bbr_ref[...] = nr * br - ni * bi
    bbi_ref[...] = nr * bi + ni * br


def _s5_disc(log_step, lam_re, lam_im, b_re, b_im):
    G, P = lam_re.shape
    n = G * P
    c = b_re.shape[-1]
    ls = jnp.broadcast_to(log_step[:, None], (G, P)).reshape(1, n)
    brT = b_re.transpose(2, 0, 1).reshape(c, n)
    biT = b_im.transpose(2, 0, 1).reshape(c, n)
    vec = jax.ShapeDtypeStruct((1, n), F32)
    mat = jax.ShapeDtypeStruct((c, n), F32)
    return pl.pallas_call(
        _s5_disc_kernel,
        out_shape=[vec, vec, mat, mat],
        name="s5_disc",
    )(ls, lam_re.reshape(1, n), lam_im.reshape(1, n), brT, biT)


def _s5_block_weights(bbrT, bbiT, c_re, c_im):
    nb = S5_BLOCK_GROUPS
    G = c_re.shape[0]
    nblk = G // nb
    eye = jnp.eye(nb, dtype=F32)

    def bdiag_b(bbT):
        t = bbT.reshape(S5_GROUP, nblk, nb, S5_STATE)
        w = jnp.einsum("cjhp,gh->jgchp", t, eye)
        return w.reshape(nblk, nb * S5_GROUP, nb * S5_STATE)

    def bdiag_c(c):
        t = c.reshape(nblk, nb, S5_GROUP, S5_STATE)
        w = jnp.einsum("jgop,gh->jhpgo", t, eye)
        return w.reshape(nblk, nb * S5_STATE, nb * S5_GROUP)

    wb = jnp.concatenate([bdiag_b(bbrT), bdiag_b(bbiT)], axis=2).astype(BF16)
    wc = jnp.concatenate([bdiag_c(c_re), -bdiag_c(c_im)], axis=1).astype(BF16)
    return wb, wc


def _s5_layer_kernel(x_ref, mod_ref, g_ref, win_ref, ar_ref, ai_ref, wb_ref, wc_ref, hr0_ref,
                     hi0_ref, d_ref, wg_ref, bglu_ref, wo_ref, o_ref, hr_ref, hi_ref,
                     u_scr, z_scr, y_scr, bu_scr, xr_scr, xi_scr, *, steps, batch):
    i = pl.program_id(0)
    rows = steps * batch
    d_model = x_ref.shape[-1]
    e = u_scr.shape[-1]
    nblk = wb_ref.shape[0]
    half = wb_ref.shape[2] // 2
    ulanes = wb_ref.shape[1]

    @pl.when(i == 0)
    def _():
        xr_scr[...] = hr0_ref[...]
        xi_scr[...] = hi0_ref[...]

    x3 = jnp.swapaxes(x_ref[...], 0, 1)
    m = mod_ref[...]
    h3 = x3 * lax.rsqrt(jnp.mean(x3 * x3, axis=-1, keepdims=True) + EPS) * g_ref[...]
    h3 = h3 * (1.0 + m[1]) + m[0]
    p = _dot(h3.reshape(rows, d_model), win_ref[...])
    u_scr[...] = p[:, :e]
    z_scr[...] = p[:, e:]

    for j in range(nblk):
        sl = slice(j * half, (j + 1) * half)
        buf = bu_scr.at[j % 2]
        buf[...] = _dot(u_scr[:, j * ulanes:(j + 1) * ulanes], wb_ref[j])
        ar = jnp.broadcast_to(ar_ref[:, sl], (batch, half))
        ai = jnp.broadcast_to(ai_ref[:, sl], (batch, half))
        xr = xr_scr[:, sl]
        xi = xi_scr[:, sl]
        for t in range(steps):
            rows_t = slice(t * batch, (t + 1) * batch)
            xr, xi = (ar * xr - ai * xi + buf[rows_t, 0:half],
                      ar * xi + ai * xr + buf[rows_t, half:2 * half])
            buf[rows_t, 0:half] = xr
            buf[rows_t, half:2 * half] = xi
        xr_scr[:, sl] = xr
        xi_scr[:, sl] = xi
        y_scr[:, j * ulanes:(j + 1) * ulanes] = _dot(buf[...], wc_ref[j])

    y = jax.nn.gelu(y_scr[...] + d_ref[...] * u_scr[...])
    y = y * jax.nn.sigmoid(_dot(y, wg_ref[...]) + bglu_ref[...])
    y = y * jax.nn.silu(z_scr[...])
    out3 = _dot(y, wo_ref[...]).reshape(steps, batch, d_model)
    o_ref[...] = jnp.swapaxes(x3 + m[2] * out3, 0, 1)

    @pl.when(i == pl.num_programs(0) - 1)
    def _():
        hr_ref[...] = xr_scr[...]
        hi_ref[...] = xi_scr[...]


def _s5_layer(x, mod, g, w_in, ar, ai, wb, wc, hr0, hi0, d, w_glu, b_glu, w_out):
    B, L, D = x.shape
    E = w_in.shape[1] // 2
    n = ar.shape[1]
    steps = min(64, L)
    rows = steps * B
    const2 = lambda i: (0, 0)
    const3 = lambda i: (0, 0, 0)

    def resident(shape):
        return pl.BlockSpec(shape, const2 if len(shape) == 2 else const3,
                            pipeline_mode=pl.Buffered(1))

    return pl.pallas_call(
        functools.partial(_s5_layer_kernel, steps=steps, batch=B),
        grid=(L // steps,),
        in_specs=[
            pl.BlockSpec((B, steps, D), lambda i: (0, i, 0)),
            resident((3, B, D)),
            resident((1, D)),
            resident((D, 2 * E)),
            resident((1, n)),
            resident((1, n)),
            resident(wb.shape),
            resident(wc.shape),
            resident((B, n)),
            resident((B, n)),
            resident((1, E)),
            resident((E, E)),
            resident((1, E)),
            resident((E, D)),
        ],
        out_specs=[
            pl.BlockSpec((B, steps, D), lambda i: (0, i, 0)),
            pl.BlockSpec((B, n), const2),
            pl.BlockSpec((B, n), const2),
        ],
        out_shape=[
            jax.ShapeDtypeStruct((B, L, D), F32),
            jax.ShapeDtypeStruct((B, n), F32),
            jax.ShapeDtypeStruct((B, n), F32),
        ],
        scratch_shapes=[
            pltpu.VMEM((rows, E), F32),
            pltpu.VMEM((rows, E), F32),
            pltpu.VMEM((rows, E), F32),
            pltpu.VMEM((2, rows, wb.shape[2]), F32),
            pltpu.VMEM((B, n), F32),
            pltpu.VMEM((B, n), F32),
        ],
        compiler_params=_params("arbitrary"),
        name="s5_layer",
    )(x, mod, g.reshape(1, D), w_in.astype(BF16), ar, ai, wb, wc, hr0, hi0, d.reshape(1, E),
      w_glu.astype(BF16), b_glu.reshape(1, E), w_out.astype(BF16))


def _l2norm(x):
    return x * lax.rsqrt(jnp.sum(x * x, axis=-1, keepdims=True) + EPS)


def _gdn_chunk(L):
    return GDN_CHUNK if L % GDN_CHUNK == 0 else L


def _split3(x):
    hi = x.astype(BF16)
    r = x - hi.astype(F32)
    mid = r.astype(BF16)
    lo = (r - mid.astype(F32)).astype(BF16)
    return hi, mid, lo


def _gdn_in_kernel(x_ref, mod_ref, g_ref, w_ref, alog_ref, dtb_ref, cw_ref,
                   hist_ref, act_ref, z_ref, bgc_ref, hist_out_ref, ext_scr, tri_scr,
                   *, heads, qk_heads, chunk):
    i = pl.program_id(1)
    tm = x_ref.shape[1]
    qk_width = qk_heads * GDN_DK
    conv_ch = act_ref.shape[-1]
    cb = qk_width

    @pl.when(i == 0)
    def _():
        ext_scr[0:HIST_ROWS, :] = hist_ref[0]
        row = lax.broadcasted_iota(jnp.int32, (tm, tm), 0)
        col = lax.broadcasted_iota(jnp.int32, (tm, tm), 1)
        sh = int(math.log2(chunk))
        tri_scr[...] = ((row >= col) & ((row >> sh) == (col >> sh))).astype(BF16)

    h = _norm_mod(x_ref[0], g_ref[...], mod_ref[0]).astype(BF16)
    cw = cw_ref[...]
    assert GDN_CONV == 4 and HIST_ROWS >= GDN_CONV - 1

    nblk = conv_ch // cb
    zb = z_ref.shape[-1] // nblk
    wcols = cb + zb

    def project(nb):
        width = wcols + (LANES if nb == 0 else 0)
        start = nb * wcols + (LANES if nb > 0 else 0)
        r = jnp.dot(h, w_ref[:, start:start + width], preferred_element_type=F32)
        ext_scr[HIST_ROWS:HIST_ROWS + tm, nb * cb:(nb + 1) * cb] = r[:, :cb]
        z_ref[0, :, nb * zb:(nb + 1) * zb] = r[:, cb:wcols]
        return r[:, wcols:]

    def conv_act(nb):
        cols = slice(nb * cb, (nb + 1) * cb)
        e = ext_scr[:, cols]
        e1 = pltpu.roll(e, 1, 0)
        wj = cw[:, cols]
        near = e * wj[3:4] + e1 * wj[2:3]
        far = pltpu.roll(e * wj[1:2] + e1 * wj[0:1], 2, 0)
        conv = (near + far)[HIST_ROWS:]
        ext_scr[0:HIST_ROWS, cols] = ext_scr[tm:tm + HIST_ROWS, cols]
        act = jax.nn.silu(conv)
        for hh in range(cb // GDN_DK):
            a = act[:, hh * GDN_DK:(hh + 1) * GDN_DK]
            if nb == 0:
                a = _l2norm(a) * (GDN_DK ** -0.5)
            elif nb == 1:
                a = _l2norm(a)
            act_ref[0, :, nb * cb + hh * GDN_DK:nb * cb + (hh + 1) * GDN_DK] = a

    for nb in range(nblk):
        tail = project(nb)
        if nb == 0:
            ba = tail
        conv_act(nb)

    beta = jax.nn.sigmoid(ba)
    a = ba + dtb_ref[...]
    softplus = jnp.maximum(a, 0.0) + jnp.log(1.0 + jnp.exp(-jnp.abs(a)))
    lane = lax.broadcasted_iota(jnp.int32, ba.shape, 1)
    live = (lane >= heads) & (lane < 2 * heads)
    g = jnp.where(live, -jnp.exp(alog_ref[...]) * softplus, 0.0)
    g_hi, g_mid, g_lo = _split3(g)
    packed = (g_hi.astype(F32) + pltpu.roll(g_mid.astype(F32), heads, 1)
              + pltpu.roll(g_lo.astype(F32), 2 * heads, 1)).astype(BF16)
    r = jnp.dot(tri_scr[...], packed, preferred_element_type=F32)
    gc = r + pltpu.roll(r, LANES - heads, 1) + pltpu.roll(r, LANES - 2 * heads, 1)
    bgc_ref[0] = jnp.where(lane < heads, beta, gc)

    @pl.when(i == pl.num_programs(1) - 1)
    def _():
        hist_out_ref[0] = ext_scr[0:HIST_ROWS, :]


def _gdn_in(x, mod, g, w_in, a_log, dt_bias, conv_w, hist, v_width):
    B, L, D = x.shape
    H = a_log.shape[0]
    conv_ch = conv_w.shape[-1]
    tm = _row_tile(L)
    o1, o2 = conv_ch, conv_ch + v_width
    wba = jnp.zeros((D, LANES), F32).at[:, :2 * H].set(w_in[:, o2:o2 + 2 * H])
    alog = jnp.zeros((1, LANES), F32).at[0, H:2 * H].set(a_log)
    dtb = jnp.zeros((1, LANES), F32).at[0, H:2 * H].set(dt_bias)
    hist8 = jnp.zeros((B, HIST_ROWS, conv_ch), F32).at[:, HIST_ROWS - (GDN_CONV - 1):, :].set(hist)
    cb = (H // 2) * GDN_DK
    nblk = conv_ch // cb
    zb = v_width // nblk
    pieces = []
    for nb in range(nblk):
        pieces += [w_in[:, nb * cb:(nb + 1) * cb], w_in[:, o1 + nb * zb:o1 + (nb + 1) * zb]]
        if nb == 0:
            pieces.append(wba)
    w_cat = jnp.concatenate(pieces, axis=1)
    const = lambda b, i: (0, 0)

    def resident(shape):
        return pl.BlockSpec(shape, const, pipeline_mode=pl.Buffered(1))

    return pl.pallas_call(
        functools.partial(_gdn_in_kernel, heads=H, qk_heads=H // 2, chunk=_gdn_chunk(L)),
        grid=(B, L // tm),
        in_specs=[
            pl.BlockSpec((1, tm, D), lambda b, i: (b, i, 0)),
            pl.BlockSpec((1, 3, D), lambda b, i: (b, 0, 0)),
            resident((1, D)),
            resident((D, conv_ch + v_width + LANES)),
            resident((1, LANES)),
            resident((1, LANES)),
            resident((GDN_CONV, conv_ch)),
            pl.BlockSpec((1, HIST_ROWS, conv_ch), lambda b, i: (b, 0, 0)),
        ],
        out_specs=[
            pl.BlockSpec((1, tm, conv_ch), lambda b, i: (b, i, 0)),
            pl.BlockSpec((1, tm, v_width), lambda b, i: (b, i, 0)),
            pl.BlockSpec((1, tm, LANES), lambda b, i: (b, i, 0)),
            pl.BlockSpec((1, HIST_ROWS, conv_ch), lambda b, i: (b, 0, 0)),
        ],
        out_shape=[
            jax.ShapeDtypeStruct((B, L, conv_ch), F32),
            jax.ShapeDtypeStruct((B, L, v_width), F32),
            jax.ShapeDtypeStruct((B, L, LANES), F32),
            jax.ShapeDtypeStruct((B, HIST_ROWS, conv_ch), F32),
        ],
        scratch_shapes=[pltpu.VMEM((HIST_ROWS + tm, conv_ch), F32), pltpu.VMEM((tm, tm), BF16)],
        compiler_params=_params("parallel", "arbitrary"),
        name="gdn_in",
    )(x, mod, g.reshape(1, D), w_cat.astype(BF16), alog, dtb, conv_w, hist8)


def _gdn_core_kernel(act_ref, bgc_ref, s0_ref, o_ref, sout_ref, s_scr, *, T, qk_heads, v_heads):
    c = pl.program_id(1)
    nrows = act_ref.shape[0]
    qk_width = qk_heads * GDN_DK
    rep = v_heads // qk_heads

    @pl.when(c == 0)
    def _():
        s_scr[...] = s0_ref[...]

    row = lax.broadcasted_iota(jnp.int32, (T, T), 0)
    col = lax.broadcasted_iota(jnp.int32, (T, T), 1)
    causal = row >= col
    strict = row > col
    eye = (lax.broadcasted_iota(jnp.int32, (LANES, LANES), 0)
           == lax.broadcasted_iota(jnp.int32, (LANES, LANES), 1)).astype(BF16)
    bgs, gcts = [], []
    for r in range(nrows):
        bg = bgc_ref[r]
        gc_hi, gc_mid, gc_lo = _split3(bg)
        bgs.append(bg)
        gcts.append(lax.dot_general(eye, gc_hi, NT_DIMS, preferred_element_type=F32)
                    + lax.dot_general(eye, gc_mid, NT_DIMS, preferred_element_type=F32)
                    + lax.dot_general(eye, gc_lo, NT_DIMS, preferred_element_type=F32))
    sh = int(math.log2(INV_BLOCK))
    diag_blk = (row >> sh) == (col >> sh)
    merges = []
    while (1 << sh) < T:
        merges.append(((row >> (sh + 1)) == (col >> (sh + 1))) & ((row >> sh) > (col >> sh)))
        sh += 1

    qkp = [(r, hq) for r in range(nrows) for hq in range(qk_heads)]
    units = [(r, h) for r in range(nrows) for h in range(v_heads)]
    n = range(len(units))
    qk_of = [r * qk_heads + h // rep for r, h in units]
    qs = [act_ref[r, :, hq * GDN_DK:(hq + 1) * GDN_DK] for r, hq in qkp]
    ks = [act_ref[r, :, qk_width + hq * GDN_DK:qk_width + (hq + 1) * GDN_DK] for r, hq in qkp]
    kq = [lax.dot_general(jnp.concatenate([ks[i], qs[i]], axis=0).astype(BF16),
                          ks[i].astype(BF16), NT_DIMS, preferred_element_type=F32)
          for i in range(len(qkp))]
    kks = [kq[i][:T] for i in qk_of]
    qks = [kq[i][T:] for i in qk_of]
    qu = [qs[i] for i in qk_of]
    ku = [ks[i] for i in qk_of]
    beta = [bgs[r][:, h:h + 1] for r, h in units]
    gcol = [bgs[r][:, v_heads + h:v_heads + h + 1] for r, h in units]
    grow = [gcts[r][v_heads + h:v_heads + h + 1, :] for r, h in units]
    decay = [jnp.where(causal, jnp.exp(jnp.where(causal, gcol[u] - grow[u], 0.0)), 0.0) for u in n]
    egc = [jnp.exp(gcol[u]) for u in n]
    m = [jnp.where(strict, (beta[u] * kks[u]) * decay[u], 0.0) for u in n]
    p = [jnp.where(diag_blk, -m[u], 0.0) for u in n]
    e = list(p)
    for _ in range(int(math.log2(INV_BLOCK)) - 1):
        p = [_dot(p[u], p[u]) for u in n]
        e = [e[u] + p[u] + _dot(e[u], p[u]) for u in n]
    for blk in merges:
        m21 = [jnp.where(blk, m[u], 0.0) for u in n]
        y = [m21[u] + _dot(m21[u], e[u]) for u in n]
        e = [e[u] - (y[u] + _dot(e[u], y[u])) for u in n]
    rhs = [jnp.concatenate(
        [act_ref[r, :, 2 * qk_width + h * GDN_DV:2 * qk_width + (h + 1) * GDN_DV] * beta[u],
         ku[u] * (beta[u] * egc[u])], axis=1) for u, (r, h) in enumerate(units)]
    sol = [rhs[u] + _dot(e[u], rhs[u]) for u in n]
    s = [s_scr[r, h] for r, h in units]
    ws = [_dot(jnp.concatenate([sol[u][:, GDN_DV:], qu[u] * egc[u]], axis=0), s[u]) for u in n]
    v_new = [sol[u][:, :GDN_DV] - ws[u][:T] for u in n]
    for u, (r, h) in enumerate(units):
        o_ref[r, :, h * GDN_DV:(h + 1) * GDN_DV] = ws[u][T:] + _dot(qks[u] * decay[u], v_new[u])
    g_last = [grow[u][:, T - 1:T] for u in n]
    for u, (r, h) in enumerate(units):
        kd = ku[u] * jnp.exp(g_last[u] - gcol[u])
        s_scr[r, h] = s[u] * jnp.exp(g_last[u]) + lax.dot_general(
            kd.astype(BF16), v_new[u].astype(BF16), (((0,), (0,)), ((), ())),
            preferred_element_type=F32)

    @pl.when(c == pl.num_programs(1) - 1)
    def _():
        sout_ref[...] = s_scr[...]


def _gdn_core(act, bgc, s0):
    B, L, C = act.shape
    H = s0.shape[1]
    T = _gdn_chunk(L)
    nr = GDN_ROWS_PER_STEP
    return pl.pallas_call(
        functools.partial(_gdn_core_kernel, T=T, qk_heads=H // 2, v_heads=H),
        grid=(B // nr, L // T),
        in_specs=[
            pl.BlockSpec((nr, T, C), lambda b, c: (b, c, 0)),
            pl.BlockSpec((nr, T, LANES), lambda b, c: (b, c, 0)),
            pl.BlockSpec((nr, H, GDN_DK, GDN_DV), lambda b, c: (b, 0, 0, 0)),
        ],
        out_specs=[
            pl.BlockSpec((nr, T, H * GDN_DV), lambda b, c: (b, c, 0)),
            pl.BlockSpec((nr, H, GDN_DK, GDN_DV), lambda b, c: (b, 0, 0, 0)),
        ],
        out_shape=[
            jax.ShapeDtypeStruct((B, L, H * GDN_DV), F32),
            jax.ShapeDtypeStruct((B, H, GDN_DK, GDN_DV), F32),
        ],
        scratch_shapes=[pltpu.VMEM((nr, H, GDN_DK, GDN_DV), F32)],
        compiler_params=_params("parallel", "arbitrary"),
        name="gdn_core",
    )(act, bgc, s0)


def _gdn_out_kernel(o_ref, z_ref, x_ref, mod_ref, ng_ref, wo_ref, fg_ref, y_ref, *, heads):
    o = o_ref[0]
    parts = []
    for h in range(heads):
        oh = o[:, h * GDN_DV:(h + 1) * GDN_DV]
        parts.append(oh * lax.rsqrt(jnp.mean(oh * oh, axis=-1, keepdims=True) + EPS))
    on = jnp.concatenate(parts, axis=1) * ng_ref[...]
    out = _dot(on * jax.nn.silu(z_ref[0]), wo_ref[...])
    x2 = x_ref[0] + mod_ref[0][2:3] * out
    y_ref[0] = x2 * lax.rsqrt(jnp.mean(x2 * x2, axis=-1, keepdims=True) + EPS) * fg_ref[...]


def _gdn_out(o, z, x, mod, norm_g, w_out, final_g):
    B, L, D = x.shape
    V = o.shape[-1]
    H = V // GDN_DV
    tm = _row_tile(L)
    const = lambda b, i: (0, 0)
    return pl.pallas_call(
        functools.partial(_gdn_out_kernel, heads=H),
        grid=(B, L // tm),
        in_specs=[
            pl.BlockSpec((1, tm, V), lambda b, i: (b, i, 0)),
            pl.BlockSpec((1, tm, V), lambda b, i: (b, i, 0)),
            pl.BlockSpec((1, tm, D), lambda b, i: (b, i, 0)),
            pl.BlockSpec((1, 3, D), lambda b, i: (b, 0, 0)),
            pl.BlockSpec((1, V), const),
            pl.BlockSpec((V, D), const),
            pl.BlockSpec((1, D), const),
        ],
        out_specs=pl.BlockSpec((1, tm, D), lambda b, i: (b, i, 0)),
        out_shape=jax.ShapeDtypeStruct((B, L, D), F32),
        compiler_params=_params("parallel", "parallel"),
        name="gdn_out",
    )(o, z, x, mod, jnp.tile(norm_g, H).reshape(1, V), w_out.astype(BF16), final_g.reshape(1, D))


def _trunk(x, mod, s5_re0, s5_im0, gdn_s0, gdn_conv0, w):
    B, L, D = x.shape
    G, P = w["s5_lambda_re"].shape[1:]
    n = G * P

    ar, ai, bbrT, bbiT = _s5_disc(w["s5_log_step"][0], w["s5_lambda_re"][0], w["s5_lambda_im"][0],
                                  w["s5_b_re"][0], w["s5_b_im"][0])
    wb, wc = _s5_block_weights(bbrT, bbiT, w["s5_c_re"][0], w["s5_c_im"][0])
    x1, hr, hi = _s5_layer(x, mod[0].transpose(1, 0, 2), w["norm_g"][0], w["s5_w_in"][0], ar, ai,
                           wb, wc, s5_re0[0].reshape(B, n), s5_im0[0].reshape(B, n),
                           w["s5_d"][0], w["s5_w_glu"][0], w["s5_b_glu"][0], w["s5_w_out"][0])

    v_width = w["gdn_w_out"].shape[1]
    act, z2, bgc, hist8 = _gdn_in(x1, mod[1], w["norm_g"][1], w["gdn_w_in"][0], w["gdn_a_log"][0],
                                  w["gdn_dt_bias"][0], w["gdn_conv_w"][0], gdn_conv0[0], v_width)
    o, s_new = _gdn_core(act, bgc, gdn_s0[0])
    y = _gdn_out(o, z2, x1, mod[1], w["gdn_norm_g"][0], w["gdn_w_out"][0], w["final_g"])
    new_hist = hist8[:, HIST_ROWS - (GDN_CONV - 1):, :]
    return (y, hr.reshape(1, B, G, P), hi.reshape(1, B, G, P), s_new[None], new_hist[None])


def kernel(x_prompt, x_sample, c_prompt, c_sample, state_s5_re, state_s5_im, state_gdn, state_gdn_conv, norm_g, w_ada, b_ada, s5_w_in, s5_log_step, s5_lambda_re, s5_lambda_im, s5_b_re, s5_b_im, s5_c_re, s5_c_im, s5_d, s5_w_glu, s5_b_glu, s5_w_out, gdn_w_in, gdn_conv_w, gdn_a_log, gdn_dt_bias, gdn_norm_g, gdn_w_out, final_g):
    w = dict(norm_g=norm_g, s5_w_in=s5_w_in, s5_log_step=s5_log_step, s5_lambda_re=s5_lambda_re,
             s5_lambda_im=s5_lambda_im, s5_b_re=s5_b_re, s5_b_im=s5_b_im, s5_c_re=s5_c_re,
             s5_c_im=s5_c_im, s5_d=s5_d, s5_w_glu=s5_w_glu, s5_b_glu=s5_b_glu, s5_w_out=s5_w_out,
             gdn_w_in=gdn_w_in, gdn_conv_w=gdn_conv_w, gdn_a_log=gdn_a_log,
             gdn_dt_bias=gdn_dt_bias, gdn_norm_g=gdn_norm_g, gdn_w_out=gdn_w_out, final_g=final_g)
    bp, _, d = x_prompt.shape
    bs = x_sample.shape[0]
    depth = w_ada.shape[0]
    mod = _ada_mod(jnp.concatenate([c_prompt, c_sample], axis=0), w_ada, b_ada)
    mod = mod.reshape(depth, bp + bs, 3, d)

    z_s5 = jnp.zeros((state_s5_re.shape[0], bp) + state_s5_re.shape[2:], F32)
    z_gdn = jnp.zeros((state_gdn.shape[0], bp) + state_gdn.shape[2:], F32)
    z_conv = jnp.zeros((state_gdn_conv.shape[0], bp) + state_gdn_conv.shape[2:], F32)
    yp, s5r_p, s5i_p, gdn_p, conv_p = _trunk(x_prompt, mod[:, :bp], z_s5, z_s5, z_gdn, z_conv, w)
    ys, s5r_s, s5i_s, gdn_s, conv_s = _trunk(x_sample, mod[:, bp:], state_s5_re, state_s5_im,
                                             state_gdn, state_gdn_conv, w)
    return (yp, ys, s5r_p, s5i_p, gdn_p, conv_p, s5r_s, s5i_s, gdn_s, conv_s)
```

```python
import functools
import math

import jax
import jax.numpy as jnp
from jax import lax
from jax.experimental import pallas as pl
from jax.experimental.pallas import tpu as pltpu

F32 = jnp.float32
BF16 = jnp.bfloat16
EPS = 1e-6

S5_GROUP = 16
S5_STATE = 64
S5_BLOCK_GROUPS = 8
GDN_DK = 128
GDN_DV = 128
GDN_CONV = 4
GDN_CHUNK = 64
GDN_ROWS_PER_STEP = 2
INV_BLOCK = 16
HIST_ROWS = 8
LANES = 128
VMEM_LIMIT = 56 * 1024 * 1024
HI = lax.Precision.HIGHEST
NT_DIMS = (((1,), (1,)), ((), ()))


def _params(*sem):
    return pltpu.CompilerParams(dimension_semantics=sem, vmem_limit_bytes=VMEM_LIMIT)


def _dot(a, b):
    return jnp.dot(a.astype(BF16), b.astype(BF16), preferred_element_type=F32)


def _row_tile(L):
    return min(512, L)


def _ada_kernel(c_ref, w_ref, b_ref, o_ref):
    c = c_ref[...]
    o_ref[0] = _dot(jax.nn.silu(c), w_ref[0]) + b_ref[0]


def _ada_mod(c_all, w_ada, b_ada):
    depth, d, d3 = w_ada.shape
    r = c_all.shape[0]
    tn = 768
    return pl.pallas_call(
        _ada_kernel,
        grid=(depth, d3 // tn),
        in_specs=[
            pl.BlockSpec((r, d), lambda i, j: (0, 0)),
            pl.BlockSpec((1, d, tn), lambda i, j: (i, 0, j)),
            pl.BlockSpec((1, 1, tn), lambda i, j: (i, 0, j)),
        ],
        out_specs=pl.BlockSpec((1, r, tn), lambda i, j: (i, 0, j)),
        out_shape=jax.ShapeDtypeStruct((depth, r, d3), F32),
        compiler_params=_params("parallel", "parallel"),
        name="ada_mod",
    )(c_all, w_ada.astype(BF16), b_ada.reshape(depth, 1, d3))


def _norm_mod(x, g, m):
    h = x * lax.rsqrt(jnp.mean(x * x, axis=-1, keepdims=True) + EPS) * g
    return h * (1.0 + m[1:2]) + m[0:1]


def _s5_disc_kernel(ls_ref, lr_ref, li_ref, br_ref, bi_ref, ar_ref, ai_ref, bbr_ref, bbi_ref):
    step = jnp.exp(ls_ref[...])
    lr = lr_ref[...]
    li = li_ref[...]
    mag = jnp.exp(lr * step)
    ar = mag * jnp.cos(li * step)
    ai = mag * jnp.sin(li * step)
    den = lr * lr + li * li
    xr = ar - 1.0
    nr = (xr * lr + ai * li) / den
    ni = (ai * lr - xr * li) / den
    br = br_ref[...]
    bi = bi_ref[...]
    ar_ref[...] = ar
    ai_ref[...] = ai
    bbr_ref[...] = nr * br - ni * bi
    bbi_ref[...] = nr * bi + ni * br


def _s5_disc(log_step, lam_re, lam_im, b_re, b_im):
    G, P = lam_re.shape
    n = G * P
    c = b_re.shape[-1]
    ls = jnp.broadcast_to(log_step[:, None], (G, P)).reshape(1, n)
    brT = b_re.transpose(2, 0, 1).reshape(c, n)
    biT = b_im.transpose(2, 0, 1).reshape(c, n)
    vec = jax.ShapeDtypeStruct((1, n), F32)
    mat = jax.ShapeDtypeStruct((c, n), F32)
    return pl.pallas_call(
        _s5_disc_kernel,
        out_shape=[vec, vec, mat, mat],
        name="s5_disc",
    )(ls, lam_re.reshape(1, n), lam_im.reshape(1, n), brT, biT)


def _s5_block_weights(bbrT, bbiT, c_re, c_im):
    nb = S5_BLOCK_GROUPS
    G = c_re.shape[0]
    nblk = G // nb
    eye = jnp.eye(nb, dtype=F32)

    def bdiag_b(bbT):
        t = bbT.reshape(S5_GROUP, nblk, nb, S5_STATE)
        w = jnp.einsum("cjhp,gh->jgchp", t, eye)
        return w.reshape(nblk, nb * S5_GROUP, nb * S5_STATE)

    def bdiag_c(c):
        t = c.reshape(nblk, nb, S5_GROUP, S5_STATE)
        w = jnp.einsum("jgop,gh->jhpgo", t, eye)
        return w.reshape(nblk, nb * S5_STATE, nb * S5_GROUP)

    wb = jnp.concatenate([bdiag_b(bbrT), bdiag_b(bbiT)], axis=2).astype(BF16)
    wc = jnp.concatenate([bdiag_c(c_re), -bdiag_c(c_im)], axis=1).astype(BF16)
    return wb, wc


def _s5_layer_kernel(x_ref, mod_ref, g_ref, win_ref, ar_ref, ai_ref, wb_ref, wc_ref, hr0_ref,
                     hi0_ref, d_ref, wg_ref, bglu_ref, wo_ref, o_ref, hr_ref, hi_ref,
                     u_scr, z_scr, y_scr, bu_scr, xr_scr, xi_scr, *, steps, batch):
    i = pl.program_id(0)
    rows = steps * batch
    d_model = x_ref.shape[-1]
    e = u_scr.shape[-1]
    nblk = wb_ref.shape[0]
    half = wb_ref.shape[2] // 2
    ulanes = wb_ref.shape[1]

    @pl.when(i == 0)
    def _():
        xr_scr[...] = hr0_ref[...]
        xi_scr[...] = hi0_ref[...]

    x3 = jnp.swapaxes(x_ref[...], 0, 1)
    m = mod_ref[...]
    h3 = x3 * lax.rsqrt(jnp.mean(x3 * x3, axis=-1, keepdims=True) + EPS) * g_ref[...]
    h3 = h3 * (1.0 + m[1]) + m[0]
    p = _dot(h3.reshape(rows, d_model), win_ref[...])
    u_scr[...] = p[:, :e]
    z_scr[...] = p[:, e:]

    for j in range(nblk):
        sl = slice(j * half, (j + 1) * half)
        buf = bu_scr.at[j % 2]
        buf[...] = _dot(u_scr[:, j * ulanes:(j + 1) * ulanes], wb_ref[j])
        ar = jnp.broadcast_to(ar_ref[:, sl], (batch, half))
        ai = jnp.broadcast_to(ai_ref[:, sl], (batch, half))
        xr = xr_scr[:, sl]
        xi = xi_scr[:, sl]
        for t in range(steps):
            rows_t = slice(t * batch, (t + 1) * batch)
            xr, xi = (ar * xr - ai * xi + buf[rows_t, 0:half],
                      ar * xi + ai * xr + buf[rows_t, half:2 * half])
            buf[rows_t, 0:half] = xr
            buf[rows_t, half:2 * half] = xi
        xr_scr[:, sl] = xr
        xi_scr[:, sl] = xi
        y_scr[:, j * ulanes:(j + 1) * ulanes] = _dot(buf[...], wc_ref[j])

    y = jax.nn.gelu(y_scr[...] + d_ref[...] * u_scr[...])
    y = y * jax.nn.sigmoid(_dot(y, wg_ref[...]) + bglu_ref[...])
    y = y * jax.nn.silu(z_scr[...])
    out3 = _dot(y, wo_ref[...]).reshape(steps, batch, d_model)
    o_ref[...] = jnp.swapaxes(x3 + m[2] * out3, 0, 1)

    @pl.when(i == pl.num_programs(0) - 1)
    def _():
        hr_ref[...] = xr_scr[...]
        hi_ref[...] = xi_scr[...]


def _s5_layer(x, mod, g, w_in, ar, ai, wb, wc, hr0, hi0, d, w_glu, b_glu, w_out):
    B, L, D = x.shape
    E = w_in.shape[1] // 2
    n = ar.shape[1]
    steps = min(64, L)
    rows = steps * B
    const2 = lambda i: (0, 0)
    const3 = lambda i: (0, 0, 0)

    def resident(shape):
        return pl.BlockSpec(shape, const2 if len(shape) == 2 else const3,
                            pipeline_mode=pl.Buffered(1))

    return pl.pallas_call(
        functools.partial(_s5_layer_kernel, steps=steps, batch=B),
        grid=(L // steps,),
        in_specs=[
            pl.BlockSpec((B, steps, D), lambda i: (0, i, 0)),
            resident((3, B, D)),
            resident((1, D)),
            resident((D, 2 * E)),
            resident((1, n)),
            resident((1, n)),
            resident(wb.shape),
            resident(wc.shape),
            resident((B, n)),
            resident((B, n)),
            resident((1, E)),
            resident((E, E)),
            resident((1, E)),
            resident((E, D)),
        ],
        out_specs=[
            pl.BlockSpec((B, steps, D), lambda i: (0, i, 0)),
            pl.BlockSpec((B, n), const2),
            pl.BlockSpec((B, n), const2),
        ],
        out_shape=[
            jax.ShapeDtypeStruct((B, L, D), F32),
            jax.ShapeDtypeStruct((B, n), F32),
            jax.ShapeDtypeStruct((B, n), F32),
        ],
        scratch_shapes=[
            pltpu.VMEM((rows, E), F32),
            pltpu.VMEM((rows, E), F32),
            pltpu.VMEM((rows, E), F32),
            pltpu.VMEM((2, rows, wb.shape[2]), F32),
            pltpu.VMEM((B, n), F32),
            pltpu.VMEM((B, n), F32),
        ],
        compiler_params=_params("arbitrary"),
        name="s5_layer",
    )(x, mod, g.reshape(1, D), w_in.astype(BF16), ar, ai, wb, wc, hr0, hi0, d.reshape(1, E),
      w_glu.astype(BF16), b_glu.reshape(1, E), w_out.astype(BF16))


def _l2norm(x):
    return x * lax.rsqrt(jnp.sum(x * x, axis=-1, keepdims=True) + EPS)


def _gdn_chunk(L):
    return GDN_CHUNK if L % GDN_CHUNK == 0 else L


def _split3(x):
    hi = x.astype(BF16)
    r = x - hi.astype(F32)
    mid = r.astype(BF16)
    lo = (r - mid.astype(F32)).astype(BF16)
    return hi, mid, lo


def _gdn_in_kernel(x_ref, mod_ref, g_ref, w_ref, alog_ref, dtb_ref, cw_ref,
                   hist_ref, act_ref, z_ref, bgc_ref, hist_out_ref, ext_scr, tri_scr,
                   *, heads, qk_heads, chunk):
    i = pl.program_id(1)
    tm = x_ref.shape[1]
    qk_width = qk_heads * GDN_DK
    conv_ch = act_ref.shape[-1]
    cb = qk_width

    @pl.when(i == 0)
    def _():
        ext_scr[0:HIST_ROWS, :] = hist_ref[0]
        row = lax.broadcasted_iota(jnp.int32, (tm, tm), 0)
        col = lax.broadcasted_iota(jnp.int32, (tm, tm), 1)
        sh = int(math.log2(chunk))
        tri_scr[...] = ((row >= col) & ((row >> sh) == (col >> sh))).astype(BF16)

    h = _norm_mod(x_ref[0], g_ref[...], mod_ref[0]).astype(BF16)
    cw = cw_ref[...]
    assert GDN_CONV == 4 and HIST_ROWS >= GDN_CONV - 1

    nblk = conv_ch // cb
    zb = z_ref.shape[-1] // nblk
    wcols = cb + zb

    def project(nb):
        width = wcols + (LANES if nb == 0 else 0)
        start = nb * wcols + (LANES if nb > 0 else 0)
        r = jnp.dot(h, w_ref[:, start:start + width], preferred_element_type=F32)
        ext_scr[HIST_ROWS:HIST_ROWS + tm, nb * cb:(nb + 1) * cb] = r[:, :cb]
        z_ref[0, :, nb * zb:(nb + 1) * zb] = r[:, cb:wcols].astype(z_ref.dtype)
        return r[:, wcols:]

    def conv_act(nb):
        cols = slice(nb * cb, (nb + 1) * cb)
        e = ext_scr[:, cols]
        e1 = pltpu.roll(e, 1, 0)
        wj = cw[:, cols]
        near = e * wj[3:4] + e1 * wj[2:3]
        far = pltpu.roll(e * wj[1:2] + e1 * wj[0:1], 2, 0)
        conv = (near + far)[HIST_ROWS:]
        ext_scr[0:HIST_ROWS, cols] = ext_scr[tm:tm + HIST_ROWS, cols]
        act = jax.nn.silu(conv)
        for hh in range(cb // GDN_DK):
            a = act[:, hh * GDN_DK:(hh + 1) * GDN_DK]
            if nb == 0:
                a = _l2norm(a) * (GDN_DK ** -0.5)
            elif nb == 1:
                a = _l2norm(a)
            act_ref[0, :, nb * cb + hh * GDN_DK:nb * cb + (hh + 1) * GDN_DK] = a

    for nb in range(nblk):
        tail = project(nb)
        if nb == 0:
            ba = tail
        conv_act(nb)

    beta = jax.nn.sigmoid(ba)
    a = ba + dtb_ref[...]
    softplus = jnp.maximum(a, 0.0) + jnp.log(1.0 + jnp.exp(-jnp.abs(a)))
    lane = lax.broadcasted_iota(jnp.int32, ba.shape, 1)
    live = (lane >= heads) & (lane < 2 * heads)
    g = jnp.where(live, -jnp.exp(alog_ref[...]) * softplus, 0.0)
    g_hi, g_mid, g_lo = _split3(g)
    packed = (g_hi.astype(F32) + pltpu.roll(g_mid.astype(F32), heads, 1)
              + pltpu.roll(g_lo.astype(F32), 2 * heads, 1)).astype(BF16)
    r = jnp.dot(tri_scr[...], packed, preferred_element_type=F32)
    gc = r + pltpu.roll(r, LANES - heads, 1) + pltpu.roll(r, LANES - 2 * heads, 1)
    bgc_ref[0] = jnp.where(lane < heads, beta, gc)

    @pl.when(i == pl.num_programs(1) - 1)
    def _():
        hist_out_ref[0] = ext_scr[0:HIST_ROWS, :]


def _gdn_in(x, mod, g, w_in, a_log, dt_bias, conv_w, hist, v_width):
    B, L, D = x.shape
    H = a_log.shape[0]
    conv_ch = conv_w.shape[-1]
    tm = _row_tile(L)
    o1, o2 = conv_ch, conv_ch + v_width
    wba = jnp.zeros((D, LANES), F32).at[:, :2 * H].set(w_in[:, o2:o2 + 2 * H])
    alog = jnp.zeros((1, LANES), F32).at[0, H:2 * H].set(a_log)
    dtb = jnp.zeros((1, LANES), F32).at[0, H:2 * H].set(dt_bias)
    hist8 = jnp.zeros((B, HIST_ROWS, conv_ch), F32).at[:, HIST_ROWS - (GDN_CONV - 1):, :].set(hist)
    cb = (H // 2) * GDN_DK
    nblk = conv_ch // cb
    zb = v_width // nblk
    pieces = []
    for nb in range(nblk):
        pieces += [w_in[:, nb * cb:(nb + 1) * cb], w_in[:, o1 + nb * zb:o1 + (nb + 1) * zb]]
        if nb == 0:
            pieces.append(wba)
    w_cat = jnp.concatenate(pieces, axis=1)
    const = lambda b, i: (0, 0)

    def resident(shape):
        return pl.BlockSpec(shape, const, pipeline_mode=pl.Buffered(1))

    return pl.pallas_call(
        functools.partial(_gdn_in_kernel, heads=H, qk_heads=H // 2, chunk=_gdn_chunk(L)),
        grid=(B, L // tm),
        in_specs=[
            pl.BlockSpec((1, tm, D), lambda b, i: (b, i, 0)),
            pl.BlockSpec((1, 3, D), lambda b, i: (b, 0, 0)),
            resident((1, D)),
            resident((D, conv_ch + v_width + LANES)),
            resident((1, LANES)),
            resident((1, LANES)),
            resident((GDN_CONV, conv_ch)),
            pl.BlockSpec((1, HIST_ROWS, conv_ch), lambda b, i: (b, 0, 0)),
        ],
        out_specs=[
            pl.BlockSpec((1, tm, conv_ch), lambda b, i: (b, i, 0)),
            pl.BlockSpec((1, tm, v_width), lambda b, i: (b, i, 0)),
            pl.BlockSpec((1, tm, LANES), lambda b, i: (b, i, 0)),
            pl.BlockSpec((1, HIST_ROWS, conv_ch), lambda b, i: (b, 0, 0)),
        ],
        out_shape=[
            jax.ShapeDtypeStruct((B, L, conv_ch), F32),
            jax.ShapeDtypeStruct((B, L, v_width), BF16),
            jax.ShapeDtypeStruct((B, L, LANES), F32),
            jax.ShapeDtypeStruct((B, HIST_ROWS, conv_ch), F32),
        ],
        scratch_shapes=[pltpu.VMEM((HIST_ROWS + tm, conv_ch), F32), pltpu.VMEM((tm, tm), BF16)],
        compiler_params=_params("parallel", "arbitrary"),
        name="gdn_in",
    )(x, mod, g.reshape(1, D), w_cat.astype(BF16), alog, dtb, conv_w, hist8)


def _gdn_core_kernel(act_ref, bgc_ref, s0_ref, o_ref, sout_ref, s_scr, *, T, qk_heads, v_heads):
    c = pl.program_id(1)
    nrows = act_ref.shape[0]
    qk_width = qk_heads * GDN_DK
    rep = v_heads // qk_heads

    @pl.when(c == 0)
    def _():
        s_scr[...] = s0_ref[...]

    row = lax.broadcasted_iota(jnp.int32, (T, T), 0)
    col = lax.broadcasted_iota(jnp.int32, (T, T), 1)
    causal = row >= col
    strict = row > col
    eye = (lax.broadcasted_iota(jnp.int32, (LANES, LANES), 0)
           == lax.broadcasted_iota(jnp.int32, (LANES, LANES), 1)).astype(BF16)
    bgs, gcts = [], []
    for r in range(nrows):
        bg = bgc_ref[r]
        gc_hi, gc_mid, gc_lo = _split3(bg)
        bgs.append(bg)
        gcts.append(lax.dot_general(eye, gc_hi, NT_DIMS, preferred_element_type=F32)
                    + lax.dot_general(eye, gc_mid, NT_DIMS, preferred_element_type=F32)
                    + lax.dot_general(eye, gc_lo, NT_DIMS, preferred_element_type=F32))
    sh = int(math.log2(INV_BLOCK))
    diag_blk = (row >> sh) == (col >> sh)
    merges = []
    while (1 << sh) < T:
        merges.append(((row >> (sh + 1)) == (col >> (sh + 1))) & ((row >> sh) > (col >> sh)))
        sh += 1

    qkp = [(r, hq) for r in range(nrows) for hq in range(qk_heads)]
    units = [(r, h) for r in range(nrows) for h in range(v_heads)]
    n = range(len(units))
    qk_of = [r * qk_heads + h // rep for r, h in units]
    qs = [act_ref[r, :, hq * GDN_DK:(hq + 1) * GDN_DK] for r, hq in qkp]
    ks = [act_ref[r, :, qk_width + hq * GDN_DK:qk_width + (hq + 1) * GDN_DK] for r, hq in qkp]
    kq = [lax.dot_general(jnp.concatenate([ks[i], qs[i]], axis=0).astype(BF16),
                          ks[i].astype(BF16), NT_DIMS, preferred_element_type=F32)
          for i in range(len(qkp))]
    kks = [kq[i][:T] for i in qk_of]
    qks = [kq[i][T:] for i in qk_of]
    qu = [qs[i] for i in qk_of]
    ku = [ks[i] for i in qk_of]
    beta = [bgs[r][:, h:h + 1] for r, h in units]
    gcol = [bgs[r][:, v_heads + h:v_heads + h + 1] for r, h in units]
    grow = [gcts[r][v_heads + h:v_heads + h + 1, :] for r, h in units]
    decay = [jnp.where(causal, jnp.exp(jnp.where(causal, gcol[u] - grow[u], 0.0)), 0.0) for u in n]
    egc = [jnp.exp(gcol[u]) for u in n]
    m = [jnp.where(strict, (beta[u] * kks[u]) * decay[u], 0.0) for u in n]
    p = [jnp.where(diag_blk, -m[u], 0.0) for u in n]
    e = list(p)
    for _ in range(int(math.log2(INV_BLOCK)) - 1):
        p = [_dot(p[u], p[u]) for u in n]
        e = [e[u] + p[u] + _dot(e[u], p[u]) for u in n]
    for blk in merges:
        m21 = [jnp.where(blk, m[u], 0.0) for u in n]
        y = [m21[u] + _dot(m21[u], e[u]) for u in n]
        e = [e[u] - (y[u] + _dot(e[u], y[u])) for u in n]
    rhs = [jnp.concatenate(
        [act_ref[r, :, 2 * qk_width + h * GDN_DV:2 * qk_width + (h + 1) * GDN_DV] * beta[u],
         ku[u] * (beta[u] * egc[u])], axis=1) for u, (r, h) in enumerate(units)]
    sol = [rhs[u] + _dot(e[u], rhs[u]) for u in n]
    s = [s_scr[r, h] for r, h in units]
    ws = [_dot(jnp.concatenate([sol[u][:, GDN_DV:], qu[u] * egc[u]], axis=0), s[u]) for u in n]
    v_new = [sol[u][:, :GDN_DV] - ws[u][:T] for u in n]
    for u, (r, h) in enumerate(units):
        o_ref[r, :, h * GDN_DV:(h + 1) * GDN_DV] = (
            ws[u][T:] + _dot(qks[u] * decay[u], v_new[u])).astype(o_ref.dtype)
    g_last = [grow[u][:, T - 1:T] for u in n]
    for u, (r, h) in enumerate(units):
        kd = ku[u] * jnp.exp(g_last[u] - gcol[u])
        s_scr[r, h] = s[u] * jnp.exp(g_last[u]) + lax.dot_general(
            kd.astype(BF16), v_new[u].astype(BF16), (((0,), (0,)), ((), ())),
            preferred_element_type=F32)

    @pl.when(c == pl.num_programs(1) - 1)
    def _():
        sout_ref[...] = s_scr[...]


def _gdn_core(act, bgc, s0):
    B, L, C = act.shape
    H = s0.shape[1]
    T = _gdn_chunk(L)
    nr = GDN_ROWS_PER_STEP
    return pl.pallas_call(
        functools.partial(_gdn_core_kernel, T=T, qk_heads=H // 2, v_heads=H),
        grid=(B // nr, L // T),
        in_specs=[
            pl.BlockSpec((nr, T, C), lambda b, c: (b, c, 0)),
            pl.BlockSpec((nr, T, LANES), lambda b, c: (b, c, 0)),
            pl.BlockSpec((nr, H, GDN_DK, GDN_DV), lambda b, c: (b, 0, 0, 0)),
        ],
        out_specs=[
            pl.BlockSpec((nr, T, H * GDN_DV), lambda b, c: (b, c, 0)),
            pl.BlockSpec((nr, H, GDN_DK, GDN_DV), lambda b, c: (b, 0, 0, 0)),
        ],
        out_shape=[
            jax.ShapeDtypeStruct((B, L, H * GDN_DV), BF16),
            jax.ShapeDtypeStruct((B, H, GDN_DK, GDN_DV), F32),
        ],
        scratch_shapes=[pltpu.VMEM((nr, H, GDN_DK, GDN_DV), F32)],
        compiler_params=_params("parallel", "arbitrary"),
        name="gdn_core",
    )(act, bgc, s0)


def _gdn_out_kernel(o_ref, z_ref, x_ref, mod_ref, ng_ref, wo_ref, fg_ref, y_ref, *, heads):
    o = o_ref[0].astype(F32)
    parts = []
    for h in range(heads):
        oh = o[:, h * GDN_DV:(h + 1) * GDN_DV]
        parts.append(oh * lax.rsqrt(jnp.mean(oh * oh, axis=-1, keepdims=True) + EPS))
    on = jnp.concatenate(parts, axis=1) * ng_ref[...]
    out = _dot(on * jax.nn.silu(z_ref[0].astype(F32)), wo_ref[...])
    x2 = x_ref[0] + mod_ref[0][2:3] * out
    y_ref[0] = x2 * lax.rsqrt(jnp.mean(x2 * x2, axis=-1, keepdims=True) + EPS) * fg_ref[...]


def _gdn_out(o, z, x, mod, norm_g, w_out, final_g):
    B, L, D = x.shape
    V = o.shape[-1]
    H = V // GDN_DV
    tm = _row_tile(L)
    const = lambda b, i: (0, 0)
    return pl.pallas_call(
        functools.partial(_gdn_out_kernel, heads=H),
        grid=(B, L // tm),
        in_specs=[
            pl.BlockSpec((1, tm, V), lambda b, i: (b, i, 0)),
            pl.BlockSpec((1, tm, V), lambda b, i: (b, i, 0)),
            pl.BlockSpec((1, tm, D), lambda b, i: (b, i, 0)),
            pl.BlockSpec((1, 3, D), lambda b, i: (b, 0, 0)),
            pl.BlockSpec((1, V), const),
            pl.BlockSpec((V, D), const),
            pl.BlockSpec((1, D), const),
        ],
        out_specs=pl.BlockSpec((1, tm, D), lambda b, i: (b, i, 0)),
        out_shape=jax.ShapeDtypeStruct((B, L, D), F32),
        compiler_params=_params("parallel", "parallel"),
        name="gdn_out",
    )(o, z, x, mod, jnp.tile(norm_g, H).reshape(1, V), w_out.astype(BF16), final_g.reshape(1, D))


def _trunk(x, mod, s5_re0, s5_im0, gdn_s0, gdn_conv0, w):
    B, L, D = x.shape
    G, P = w["s5_lambda_re"].shape[1:]
    n = G * P

    ar, ai, bbrT, bbiT = _s5_disc(w["s5_log_step"][0], w["s5_lambda_re"][0], w["s5_lambda_im"][0],
                                  w["s5_b_re"][0], w["s5_b_im"][0])
    wb, wc = _s5_block_weights(bbrT, bbiT, w["s5_c_re"][0], w["s5_c_im"][0])
    x1, hr, hi = _s5_layer(x, mod[0].transpose(1, 0, 2), w["norm_g"][0], w["s5_w_in"][0], ar, ai,
                           wb, wc, s5_re0[0].reshape(B, n), s5_im0[0].reshape(B, n),
                           w["s5_d"][0], w["s5_w_glu"][0], w["s5_b_glu"][0], w["s5_w_out"][0])

    v_width = w["gdn_w_out"].shape[1]
    act, z2, bgc, hist8 = _gdn_in(x1, mod[1], w["norm_g"][1], w["gdn_w_in"][0], w["gdn_a_log"][0],
                                  w["gdn_dt_bias"][0], w["gdn_conv_w"][0], gdn_conv0[0], v_width)
    o, s_new = _gdn_core(act, bgc, gdn_s0[0])
    y = _gdn_out(o, z2, x1, mod[1], w["gdn_norm_g"][0], w["gdn_w_out"][0], w["final_g"])
    new_hist = hist8[:, HIST_ROWS - (GDN_CONV - 1):, :]
    return (y, hr.reshape(1, B, G, P), hi.reshape(1, B, G, P), s_new[None], new_hist[None])


def kernel(x_prompt, x_sample, c_prompt, c_sample, state_s5_re, state_s5_im, state_gdn, state_gdn_conv, norm_g, w_ada, b_ada, s5_w_in, s5_log_step, s5_lambda_re, s5_lambda_im, s5_b_re, s5_b_im, s5_c_re, s5_c_im, s5_d, s5_w_glu, s5_b_glu, s5_w_out, gdn_w_in, gdn_conv_w, gdn_a_log, gdn_dt_bias, gdn_norm_g, gdn_w_out, final_g):
    w = dict(norm_g=norm_g, s5_w_in=s5_w_in, s5_log_step=s5_log_step, s5_lambda_re=s5_lambda_re,
             s5_lambda_im=s5_lambda_im, s5_b_re=s5_b_re, s5_b_im=s5_b_im, s5_c_re=s5_c_re,
             s5_c_im=s5_c_im, s5_d=s5_d, s5_w_glu=s5_w_glu, s5_b_glu=s5_b_glu, s5_w_out=s5_w_out,
             gdn_w_in=gdn_w_in, gdn_conv_w=gdn_conv_w, gdn_a_log=gdn_a_log,
             gdn_dt_bias=gdn_dt_bias, gdn_norm_g=gdn_norm_g, gdn_w_out=gdn_w_out, final_g=final_g)
    bp, _, d = x_prompt.shape
    bs = x_sample.shape[0]
    depth = w_ada.shape[0]
    mod = _ada_mod(jnp.concatenate([c_prompt, c_sample], axis=0), w_ada, b_ada)
    mod = mod.reshape(depth, bp + bs, 3, d)

    z_s5 = jnp.zeros((state_s5_re.shape[0], bp) + state_s5_re.shape[2:], F32)
    z_gdn = jnp.zeros((state_gdn.shape[0], bp) + state_gdn.shape[2:], F32)
    z_conv = jnp.zeros((state_gdn_conv.shape[0], bp) + state_gdn_conv.shape[2:], F32)
    yp, s5r_p, s5i_p, gdn_p, conv_p = _trunk(x_prompt, mod[:, :bp], z_s5, z_s5, z_gdn, z_conv, w)
    ys, s5r_s, s5i_s, gdn_s, conv_s = _trunk(x_sample, mod[:, bp:], state_s5_re, state_s5_im,
                                             state_gdn, state_gdn_conv, w)
    return (yp, ys, s5r_p, s5i_p, gdn_p, conv_p, s5r_s, s5i_s, gdn_s, conv_s)
```

```python
import functools
import math

import jax
import jax.numpy as jnp
from jax import lax
from jax.experimental import pallas as pl
from jax.experimental.pallas import tpu as pltpu

F32 = jnp.float32
BF16 = jnp.bfloat16
EPS = 1e-6

S5_GROUP = 16
S5_STATE = 64
S5_BLOCK_GROUPS = 8
GDN_DK = 128
GDN_DV = 128
GDN_CONV = 4
GDN_CHUNK = 64
GDN_ROWS_PER_STEP = 2
INV_BLOCK = 16
HIST_ROWS = 8
LANES = 128
VMEM_LIMIT = 56 * 1024 * 1024
HI = lax.Precision.HIGHEST
NT_DIMS = (((1,), (1,)), ((), ()))


def _params(*sem):
    return pltpu.CompilerParams(dimension_semantics=sem, vmem_limit_bytes=VMEM_LIMIT)


def _dot(a, b):
    return jnp.dot(a.astype(BF16), b.astype(BF16), preferred_element_type=F32)


def _row_tile(L):
    return min(512, L)


def _ada_kernel(c_ref, w_ref, b_ref, o_ref):
    c = c_ref[...]
    o_ref[0] = _dot(jax.nn.silu(c), w_ref[0]) + b_ref[0]


def _ada_mod(c_all, w_ada, b_ada):
    depth, d, d3 = w_ada.shape
    r = c_all.shape[0]
    tn = 768
    return pl.pallas_call(
        _ada_kernel,
        grid=(depth, d3 // tn),
        in_specs=[
            pl.BlockSpec((r, d), lambda i, j: (0, 0)),
            pl.BlockSpec((1, d, tn), lambda i, j: (i, 0, j)),
            pl.BlockSpec((1, 1, tn), lambda i, j: (i, 0, j)),
        ],
        out_specs=pl.BlockSpec((1, r, tn), lambda i, j: (i, 0, j)),
        out_shape=jax.ShapeDtypeStruct((depth, r, d3), F32),
        compiler_params=_params("parallel", "parallel"),
        name="ada_mod",
    )(c_all, w_ada.astype(BF16), b_ada.reshape(depth, 1, d3))


def _norm_mod(x, g, m):
    h = x * lax.rsqrt(jnp.mean(x * x, axis=-1, keepdims=True) + EPS) * g
    return h * (1.0 + m[1:2]) + m[0:1]


def _s5_disc_kernel(ls_ref, lr_ref, li_ref, br_ref, bi_ref, cr_ref, ci_ref, seg_ref,
                    a2r_ref, a2i_ref, bbr_ref, bbi_ref, abr_ref, abi_ref, car_ref, cai_ref,
                    ca2r_ref, ca2i_ref, k0_ref, k1_ref):
    step = jnp.exp(ls_ref[...])
    lr = lr_ref[...]
    li = li_ref[...]
    mag = jnp.exp(lr * step)
    ar = mag * jnp.cos(li * step)
    ai = mag * jnp.sin(li * step)
    den = lr * lr + li * li
    xr = ar - 1.0
    nr = (xr * lr + ai * li) / den
    ni = (ai * lr - xr * li) / den
    bbr = nr * br_ref[...] - ni * bi_ref[...]
    bbi = nr * bi_ref[...] + ni * br_ref[...]
    abr = ar * bbr - ai * bbi
    abi = ar * bbi + ai * bbr
    a2r = ar * ar - ai * ai
    a2i = 2.0 * (ar * ai)
    cr = cr_ref[...]
    ci = ci_ref[...]
    a2r_ref[...] = a2r
    a2i_ref[...] = a2i
    bbr_ref[...] = bbr
    bbi_ref[...] = bbi
    abr_ref[...] = abr
    abi_ref[...] = abi
    car_ref[...] = cr * ar - ci * ai
    cai_ref[...] = cr * ai + ci * ar
    ca2r_ref[...] = cr * a2r - ci * a2i
    ca2i_ref[...] = cr * a2i + ci * a2r
    seg = seg_ref[...]
    for c in range(br_ref.shape[0]):
        k0_ref[c] = jnp.dot(cr * bbr[c:c + 1] - ci * bbi[c:c + 1], seg, precision=HI,
                            preferred_element_type=F32)
        k1_ref[c] = jnp.dot(cr * abr[c:c + 1] - ci * abi[c:c + 1], seg, precision=HI,
                            preferred_element_type=F32)


def _s5_disc(log_step, lam_re, lam_im, b_re, b_im, c_re, c_im):
    G, P = lam_re.shape
    n = G * P
    c = b_re.shape[-1]
    ls = jnp.broadcast_to(log_step[:, None], (G, P)).reshape(1, n)
    brT = b_re.transpose(2, 0, 1).reshape(c, n)
    biT = b_im.transpose(2, 0, 1).reshape(c, n)
    crT = c_re.transpose(1, 0, 2).reshape(c, n)
    ciT = c_im.transpose(1, 0, 2).reshape(c, n)
    seg = jnp.repeat(jnp.eye(G, dtype=F32), P, axis=0)
    vec = jax.ShapeDtypeStruct((1, n), F32)
    mat = jax.ShapeDtypeStruct((c, n), F32)
    kmat = jax.ShapeDtypeStruct((c, c, G), F32)
    return pl.pallas_call(
        _s5_disc_kernel,
        out_shape=[vec, vec] + [mat] * 8 + [kmat, kmat],
        name="s5_disc",
    )(ls, lam_re.reshape(1, n), lam_im.reshape(1, n), brT, biT, crT, ciT, seg)


def _s5_block_weights(bbrT, bbiT, abrT, abiT, carT, caiT, ca2rT, ca2iT, k0, k1):
    nb = S5_BLOCK_GROUPS
    G = k0.shape[-1]
    nblk = G // nb
    eye = jnp.eye(nb, dtype=F32)

    def bdiag_b(bbT):
        t = bbT.reshape(S5_GROUP, nblk, nb, S5_STATE)
        w = jnp.einsum("cjhp,gh->jgchp", t, eye)
        return w.reshape(nblk, nb * S5_GROUP, nb * S5_STATE)

    def bdiag_c(cT):
        t = cT.reshape(S5_GROUP, nblk, nb, S5_STATE)
        w = jnp.einsum("ojgp,gh->jhpgo", t, eye)
        return w.reshape(nblk, nb * S5_STATE, nb * S5_GROUP)

    def bdiag_k(k):
        t = k.reshape(S5_GROUP, S5_GROUP, nblk, nb)
        w = jnp.einsum("iojg,gh->jgiho", t, eye)
        return w.reshape(nblk, nb * S5_GROUP, nb * S5_GROUP)

    wb = jnp.concatenate([
        jnp.concatenate([bdiag_b(abrT), bdiag_b(abiT)], axis=2),
        jnp.concatenate([bdiag_b(bbrT), bdiag_b(bbiT)], axis=2)], axis=1).astype(BF16)
    wc = jnp.concatenate([
        jnp.concatenate([bdiag_c(carT), bdiag_c(ca2rT)], axis=2),
        jnp.concatenate([-bdiag_c(caiT), -bdiag_c(ca2iT)], axis=2)], axis=1).astype(BF16)
    d0, d1 = bdiag_k(k0), bdiag_k(k1)
    wd = jnp.concatenate([
        jnp.concatenate([d0, d1], axis=2),
        jnp.concatenate([jnp.zeros_like(d0), d0], axis=2)], axis=1).astype(BF16)
    return wb, wc, wd


def _s5_layer_kernel(x_ref, mod_ref, g_ref, win_ref, ar_ref, ai_ref, wb_ref, wc_ref, wd_ref,
                     hr0_ref, hi0_ref, d_ref, wg_ref, bglu_ref, wo_ref, o_ref, hr_ref, hi_ref,
                     u_scr, z_scr, y_scr, bu_scr, xr_scr, xi_scr, *, steps, batch):
    i = pl.program_id(0)
    rows = steps * batch
    pairs = steps // 2
    prow = pairs * batch
    d_model = x_ref.shape[-1]
    e = u_scr.shape[-1]
    nblk = wb_ref.shape[0]
    half = wb_ref.shape[2] // 2
    ulanes = wb_ref.shape[1] // 2

    @pl.when(i == 0)
    def _():
        xr_scr[...] = hr0_ref[...]
        xi_scr[...] = hi0_ref[...]

    x3 = jnp.swapaxes(x_ref[...], 0, 1)
    m = mod_ref[...]
    h3 = x3 * lax.rsqrt(jnp.mean(x3 * x3, axis=-1, keepdims=True) + EPS) * g_ref[...]
    h3 = h3 * (1.0 + m[1]) + m[0]
    p = _dot(h3.reshape(rows, d_model), win_ref[...])
    u_scr[...] = p[:, :e].reshape(pairs, 2, batch, e)
    z_scr[...] = p[:, e:]

    for j in range(nblk):
        sl = slice(j * half, (j + 1) * half)
        ch = slice(j * ulanes, (j + 1) * ulanes)
        buf = bu_scr.at[j % 2]
        lhs = jnp.concatenate([u_scr[:, 0, :, ch].reshape(prow, ulanes),
                               u_scr[:, 1, :, ch].reshape(prow, ulanes)], axis=1).astype(BF16)
        buf[batch:batch + prow, :] = jnp.dot(lhs, wb_ref[j], preferred_element_type=F32)
        ar = jnp.broadcast_to(ar_ref[:, sl], (batch, half))
        ai = jnp.broadcast_to(ai_ref[:, sl], (batch, half))
        xr = xr_scr[:, sl]
        xi = xi_scr[:, sl]
        buf[0:batch, 0:half] = xr
        buf[0:batch, half:2 * half] = xi
        for t in range(1, pairs + 1):
            rows_t = slice(t * batch, (t + 1) * batch)
            xr, xi = (ar * xr - ai * xi + buf[rows_t, 0:half],
                      ar * xi + ai * xr + buf[rows_t, half:2 * half])
            buf[rows_t, 0:half] = xr
            buf[rows_t, half:2 * half] = xi
        xr_scr[:, sl] = xr
        xi_scr[:, sl] = xi
        ypair = _dot(buf[0:prow, :], wc_ref[j]) + jnp.dot(lhs, wd_ref[j],
                                                          preferred_element_type=F32)
        y_scr[:, 0, :, ch] = ypair[:, :ulanes].reshape(pairs, batch, ulanes)
        y_scr[:, 1, :, ch] = ypair[:, ulanes:].reshape(pairs, batch, ulanes)

    y = jax.nn.gelu(y_scr[...].reshape(rows, e) + d_ref[...] * u_scr[...].reshape(rows, e))
    y = y * jax.nn.sigmoid(_dot(y, wg_ref[...]) + bglu_ref[...])
    y = y * jax.nn.silu(z_scr[...])
    out3 = _dot(y, wo_ref[...]).reshape(steps, batch, d_model)
    o_ref[...] = jnp.swapaxes(x3 + m[2] * out3, 0, 1)

    @pl.when(i == pl.num_programs(0) - 1)
    def _():
        hr_ref[...] = xr_scr[...]
        hi_ref[...] = xi_scr[...]


def _s5_layer(x, mod, g, w_in, ar, ai, wb, wc, wd, hr0, hi0, d, w_glu, b_glu, w_out):
    B, L, D = x.shape
    E = w_in.shape[1] // 2
    n = ar.shape[1]
    steps = min(64, L)
    assert steps % 2 == 0 and L % steps == 0
    rows = steps * B
    pairs = steps // 2
    const2 = lambda i: (0, 0)
    const3 = lambda i: (0, 0, 0)

    def resident(shape):
        return pl.BlockSpec(shape, const2 if len(shape) == 2 else const3,
                            pipeline_mode=pl.Buffered(1))

    return pl.pallas_call(
        functools.partial(_s5_layer_kernel, steps=steps, batch=B),
        grid=(L // steps,),
        in_specs=[
            pl.BlockSpec((B, steps, D), lambda i: (0, i, 0)),
            resident((3, B, D)),
            resident((1, D)),
            resident((D, 2 * E)),
            resident((1, n)),
            resident((1, n)),
            resident(wb.shape),
            resident(wc.shape),
            resident(wd.shape),
            resident((B, n)),
            resident((B, n)),
            resident((1, E)),
            resident((E, E)),
            resident((1, E)),
            resident((E, D)),
        ],
        out_specs=[
            pl.BlockSpec((B, steps, D), lambda i: (0, i, 0)),
            pl.BlockSpec((B, n), const2),
            pl.BlockSpec((B, n), const2),
        ],
        out_shape=[
            jax.ShapeDtypeStruct((B, L, D), F32),
            jax.ShapeDtypeStruct((B, n), F32),
            jax.ShapeDtypeStruct((B, n), F32),
        ],
        scratch_shapes=[
            pltpu.VMEM((pairs, 2, B, E), F32),
            pltpu.VMEM((rows, E), F32),
            pltpu.VMEM((pairs, 2, B, E), F32),
            pltpu.VMEM((2, (pairs + 1) * B, wb.shape[2]), F32),
            pltpu.VMEM((B, n), F32),
            pltpu.VMEM((B, n), F32),
        ],
        compiler_params=_params("arbitrary"),
        name="s5_layer",
    )(x, mod, g.reshape(1, D), w_in.astype(BF16), ar, ai, wb, wc, wd, hr0, hi0, d.reshape(1, E),
      w_glu.astype(BF16), b_glu.reshape(1, E), w_out.astype(BF16))


def _l2norm(x):
    return x * lax.rsqrt(jnp.sum(x * x, axis=-1, keepdims=True) + EPS)


def _gdn_chunk(L):
    return GDN_CHUNK if L % GDN_CHUNK == 0 else L


def _split3(x):
    hi = x.astype(BF16)
    r = x - hi.astype(F32)
    mid = r.astype(BF16)
    lo = (r - mid.astype(F32)).astype(BF16)
    return hi, mid, lo


def _gdn_in_kernel(x_ref, mod_ref, g_ref, w_ref, alog_ref, dtb_ref, cw_ref,
                   hist_ref, act_ref, z_ref, bgc_ref, hist_out_ref, ext_scr, tri_scr,
                   *, heads, qk_heads, chunk):
    i = pl.program_id(1)
    tm = x_ref.shape[1]
    qk_width = qk_heads * GDN_DK
    conv_ch = act_ref.shape[-1]
    cb = qk_width

    @pl.when(i == 0)
    def _():
        ext_scr[0:HIST_ROWS, :] = hist_ref[0]
        row = lax.broadcasted_iota(jnp.int32, (tm, tm), 0)
        col = lax.broadcasted_iota(jnp.int32, (tm, tm), 1)
        sh = int(math.log2(chunk))
        tri_scr[...] = ((row >= col) & ((row >> sh) == (col >> sh))).astype(BF16)

    h = _norm_mod(x_ref[0], g_ref[...], mod_ref[0]).astype(BF16)
    cw = cw_ref[...]
    assert GDN_CONV == 4 and HIST_ROWS >= GDN_CONV - 1

    nblk = conv_ch // cb
    zb = z_ref.shape[-1] // nblk
    wcols = cb + zb

    def project(nb):
        width = wcols + (LANES if nb == 0 else 0)
        start = nb * wcols + (LANES if nb > 0 else 0)
        r = jnp.dot(h, w_ref[:, start:start + width], preferred_element_type=F32)
        ext_scr[HIST_ROWS:HIST_ROWS + tm, nb * cb:(nb + 1) * cb] = r[:, :cb]
        z_ref[0, :, nb * zb:(nb + 1) * zb] = r[:, cb:wcols].astype(z_ref.dtype)
        return r[:, wcols:]

    def conv_act(nb):
        cols = slice(nb * cb, (nb + 1) * cb)
        e = ext_scr[:, cols]
        e1 = pltpu.roll(e, 1, 0)
        wj = cw[:, cols]
        near = e * wj[3:4] + e1 * wj[2:3]
        far = pltpu.roll(e * wj[1:2] + e1 * wj[0:1], 2, 0)
        conv = (near + far)[HIST_ROWS:]
        ext_scr[0:HIST_ROWS, cols] = ext_scr[tm:tm + HIST_ROWS, cols]
        act = jax.nn.silu(conv)
        for hh in range(cb // GDN_DK):
            a = act[:, hh * GDN_DK:(hh + 1) * GDN_DK]
            if nb == 0:
                a = _l2norm(a) * (GDN_DK ** -0.5)
            elif nb == 1:
                a = _l2norm(a)
            act_ref[0, :, nb * cb + hh * GDN_DK:nb * cb + (hh + 1) * GDN_DK] = a

    for nb in range(nblk):
        tail = project(nb)
        if nb == 0:
            ba = tail
        conv_act(nb)

    beta = jax.nn.sigmoid(ba)
    a = ba + dtb_ref[...]
    softplus = jnp.maximum(a, 0.0) + jnp.log(1.0 + jnp.exp(-jnp.abs(a)))
    lane = lax.broadcasted_iota(jnp.int32, ba.shape, 1)
    live = (lane >= heads) & (lane < 2 * heads)
    g = jnp.where(live, -jnp.exp(alog_ref[...]) * softplus, 0.0)
    g_hi, g_mid, g_lo = _split3(g)
    packed = (g_hi.astype(F32) + pltpu.roll(g_mid.astype(F32), heads, 1)
              + pltpu.roll(g_lo.astype(F32), 2 * heads, 1)).astype(BF16)
    r = jnp.dot(tri_scr[...], packed, preferred_element_type=F32)
    gc = r + pltpu.roll(r, LANES - heads, 1) + pltpu.roll(r, LANES - 2 * heads, 1)
    bgc_ref[0] = jnp.where(lane < heads, beta, gc)

    @pl.when(i == pl.num_programs(1) - 1)
    def _():
        hist_out_ref[0] = ext_scr[0:HIST_ROWS, :]


def _gdn_in(x, mod, g, w_in, a_log, dt_bias, conv_w, hist, v_width):
    B, L, D = x.shape
    H = a_log.shape[0]
    conv_ch = conv_w.shape[-1]
    tm = _row_tile(L)
    o1, o2 = conv_ch, conv_ch + v_width
    wba = jnp.zeros((D, LANES), F32).at[:, :2 * H].set(w_in[:, o2:o2 + 2 * H])
    alog = jnp.zeros((1, LANES), F32).at[0, H:2 * H].set(a_log)
    dtb = jnp.zeros((1, LANES), F32).at[0, H:2 * H].set(dt_bias)
    hist8 = jnp.zeros((B, HIST_ROWS, conv_ch), F32).at[:, HIST_ROWS - (GDN_CONV - 1):, :].set(hist)
    cb = (H // 2) * GDN_DK
    nblk = conv_ch // cb
    zb = v_width // nblk
    pieces = []
    for nb in range(nblk):
        pieces += [w_in[:, nb * cb:(nb + 1) * cb], w_in[:, o1 + nb * zb:o1 + (nb + 1) * zb]]
        if nb == 0:
            pieces.append(wba)
    w_cat = jnp.concatenate(pieces, axis=1)
    const = lambda b, i: (0, 0)

    def resident(shape):
        return pl.BlockSpec(shape, const, pipeline_mode=pl.Buffered(1))

    return pl.pallas_call(
        functools.partial(_gdn_in_kernel, heads=H, qk_heads=H // 2, chunk=_gdn_chunk(L)),
        grid=(B, L // tm),
        in_specs=[
            pl.BlockSpec((1, tm, D), lambda b, i: (b, i, 0)),
            pl.BlockSpec((1, 3, D), lambda b, i: (b, 0, 0)),
            resident((1, D)),
            resident((D, conv_ch + v_width + LANES)),
            resident((1, LANES)),
            resident((1, LANES)),
            resident((GDN_CONV, conv_ch)),
            pl.BlockSpec((1, HIST_ROWS, conv_ch), lambda b, i: (b, 0, 0)),
        ],
        out_specs=[
            pl.BlockSpec((1, tm, conv_ch), lambda b, i: (b, i, 0)),
            pl.BlockSpec((1, tm, v_width), lambda b, i: (b, i, 0)),
            pl.BlockSpec((1, tm, LANES), lambda b, i: (b, i, 0)),
            pl.BlockSpec((1, HIST_ROWS, conv_ch), lambda b, i: (b, 0, 0)),
        ],
        out_shape=[
            jax.ShapeDtypeStruct((B, L, conv_ch), F32),
            jax.ShapeDtypeStruct((B, L, v_width), BF16),
            jax.ShapeDtypeStruct((B, L, LANES), F32),
            jax.ShapeDtypeStruct((B, HIST_ROWS, conv_ch), F32),
        ],
        scratch_shapes=[pltpu.VMEM((HIST_ROWS + tm, conv_ch), F32), pltpu.VMEM((tm, tm), BF16)],
        compiler_params=_params("parallel", "arbitrary"),
        name="gdn_in",
    )(x, mod, g.reshape(1, D), w_cat.astype(BF16), alog, dtb, conv_w, hist8)


def _gdn_core_kernel(act_ref, bgc_ref, s0_ref, o_ref, sout_ref, s_scr, *, T, qk_heads, v_heads):
    c = pl.program_id(1)
    nrows = act_ref.shape[0]
    qk_width = qk_heads * GDN_DK
    rep = v_heads // qk_heads

    @pl.when(c == 0)
    def _():
        s_scr[...] = s0_ref[...]

    row = lax.broadcasted_iota(jnp.int32, (T, T), 0)
    col = lax.broadcasted_iota(jnp.int32, (T, T), 1)
    causal = row >= col
    strict = row > col
    eye = (lax.broadcasted_iota(jnp.int32, (LANES, LANES), 0)
           == lax.broadcasted_iota(jnp.int32, (LANES, LANES), 1)).astype(BF16)
    bgs, gcts = [], []
    for r in range(nrows):
        bg = bgc_ref[r]
        gc_hi, gc_mid, gc_lo = _split3(bg)
        bgs.append(bg)
        gcts.append(lax.dot_general(eye, gc_hi, NT_DIMS, preferred_element_type=F32)
                    + lax.dot_general(eye, gc_mid, NT_DIMS, preferred_element_type=F32)
                    + lax.dot_general(eye, gc_lo, NT_DIMS, preferred_element_type=F32))
    sh = int(math.log2(INV_BLOCK))
    diag_blk = (row >> sh) == (col >> sh)
    merges = []
    while (1 << sh) < T:
        merges.append(((row >> (sh + 1)) == (col >> (sh + 1))) & ((row >> sh) > (col >> sh)))
        sh += 1

    qkp = [(r, hq) for r in range(nrows) for hq in range(qk_heads)]
    units = [(r, h) for r in range(nrows) for h in range(v_heads)]
    n = range(len(units))
    qk_of = [r * qk_heads + h // rep for r, h in units]
    qs = [act_ref[r, :, hq * GDN_DK:(hq + 1) * GDN_DK] for r, hq in qkp]
    ks = [act_ref[r, :, qk_width + hq * GDN_DK:qk_width + (hq + 1) * GDN_DK] for r, hq in qkp]
    kq = [lax.dot_general(jnp.concatenate([ks[i], qs[i]], axis=0).astype(BF16),
                          ks[i].astype(BF16), NT_DIMS, preferred_element_type=F32)
          for i in range(len(qkp))]
    kks = [kq[i][:T] for i in qk_of]
    qks = [kq[i][T:] for i in qk_of]
    qu = [qs[i] for i in qk_of]
    ku = [ks[i] for i in qk_of]
    beta = [bgs[r][:, h:h + 1] for r, h in units]
    gcol = [bgs[r][:, v_heads + h:v_heads + h + 1] for r, h in units]
    grow = [gcts[r][v_heads + h:v_heads + h + 1, :] for r, h in units]
    decay = [jnp.where(causal, jnp.exp(jnp.where(causal, gcol[u] - grow[u], 0.0)), 0.0) for u in n]
    egc = [jnp.exp(gcol[u]) for u in n]
    m = [jnp.where(strict, (beta[u] * kks[u]) * decay[u], 0.0) for u in n]
    p = [jnp.where(diag_blk, -m[u], 0.0) for u in n]
    e = list(p)
    for _ in range(int(math.log2(INV_BLOCK)) - 1):
        p = [_dot(p[u], p[u]) for u in n]
        e = [e[u] + p[u] + _dot(e[u], p[u]) for u in n]
    for blk in merges:
        m21 = [jnp.where(blk, m[u], 0.0) for u in n]
        y = [m21[u] + _dot(m21[u], e[u]) for u in n]
        e = [e[u] - (y[u] + _dot(e[u], y[u])) for u in n]
    rhs = [jnp.concatenate(
        [act_ref[r, :, 2 * qk_width + h * GDN_DV:2 * qk_width + (h + 1) * GDN_DV] * beta[u],
         ku[u] * (beta[u] * egc[u])], axis=1) for u, (r, h) in enumerate(units)]
    sol = [rhs[u] + _dot(e[u], rhs[u]) for u in n]
    s = [s_scr[r, h] for r, h in units]
    ws = [_dot(jnp.concatenate([sol[u][:, GDN_DV:], qu[u] * egc[u]], axis=0), s[u]) for u in n]
    v_new = [sol[u][:, :GDN_DV] - ws[u][:T] for u in n]
    for u, (r, h) in enumerate(units):
        o_ref[r, :, h * GDN_DV:(h + 1) * GDN_DV] = (
            ws[u][T:] + _dot(qks[u] * decay[u], v_new[u])).astype(o_ref.dtype)
    g_last = [grow[u][:, T - 1:T] for u in n]
    for u, (r, h) in enumerate(units):
        kd = ku[u] * jnp.exp(g_last[u] - gcol[u])
        s_scr[r, h] = s[u] * jnp.exp(g_last[u]) + lax.dot_general(
            kd.astype(BF16), v_new[u].astype(BF16), (((0,), (0,)), ((), ())),
            preferred_element_type=F32)

    @pl.when(c == pl.num_programs(1) - 1)
    def _():
        sout_ref[...] = s_scr[...]


def _gdn_core(act, bgc, s0):
    B, L, C = act.shape
    H = s0.shape[1]
    T = _gdn_chunk(L)
    nr = GDN_ROWS_PER_STEP
    return pl.pallas_call(
        functools.partial(_gdn_core_kernel, T=T, qk_heads=H // 2, v_heads=H),
        grid=(B // nr, L // T),
        in_specs=[
            pl.BlockSpec((nr, T, C), lambda b, c: (b, c, 0)),
            pl.BlockSpec((nr, T, LANES), lambda b, c: (b, c, 0)),
            pl.BlockSpec((nr, H, GDN_DK, GDN_DV), lambda b, c: (b, 0, 0, 0)),
        ],
        out_specs=[
            pl.BlockSpec((nr, T, H * GDN_DV), lambda b, c: (b, c, 0)),
            pl.BlockSpec((nr, H, GDN_DK, GDN_DV), lambda b, c: (b, 0, 0, 0)),
        ],
        out_shape=[
            jax.ShapeDtypeStruct((B, L, H * GDN_DV), BF16),
            jax.ShapeDtypeStruct((B, H, GDN_DK, GDN_DV), F32),
        ],
        scratch_shapes=[pltpu.VMEM((nr, H, GDN_DK, GDN_DV), F32)],
        compiler_params=_params("parallel", "arbitrary"),
        name="gdn_core",
    )(act, bgc, s0)


def _gdn_out_kernel(o_ref, z_ref, x_ref, mod_ref, ng_ref, wo_ref, fg_ref, y_ref, *, heads):
    o = o_ref[0].astype(F32)
    parts = []
    for h in range(heads):
        oh = o[:, h * GDN_DV:(h + 1) * GDN_DV]
        parts.append(oh * lax.rsqrt(jnp.mean(oh * oh, axis=-1, keepdims=True) + EPS))
    on = jnp.concatenate(parts, axis=1) * ng_ref[...]
    out = _dot(on * jax.nn.silu(z_ref[0].astype(F32)), wo_ref[...])
    x2 = x_ref[0] + mod_ref[0][2:3] * out
    y_ref[0] = x2 * lax.rsqrt(jnp.mean(x2 * x2, axis=-1, keepdims=True) + EPS) * fg_ref[...]


def _gdn_out(o, z, x, mod, norm_g, w_out, final_g):
    B, L, D = x.shape
    V = o.shape[-1]
    H = V // GDN_DV
    tm = _row_tile(L)
    const = lambda b, i: (0, 0)
    return pl.pallas_call(
        functools.partial(_gdn_out_kernel, heads=H),
        grid=(B, L // tm),
        in_specs=[
            pl.BlockSpec((1, tm, V), lambda b, i: (b, i, 0)),
            pl.BlockSpec((1, tm, V), lambda b, i: (b, i, 0)),
            pl.BlockSpec((1, tm, D), lambda b, i: (b, i, 0)),
            pl.BlockSpec((1, 3, D), lambda b, i: (b, 0, 0)),
            pl.BlockSpec((1, V), const),
            pl.BlockSpec((V, D), const),
            pl.BlockSpec((1, D), const),
        ],
        out_specs=pl.BlockSpec((1, tm, D), lambda b, i: (b, i, 0)),
        out_shape=jax.ShapeDtypeStruct((B, L, D), F32),
        compiler_params=_params("parallel", "parallel"),
        name="gdn_out",
    )(o, z, x, mod, jnp.tile(norm_g, H).reshape(1, V), w_out.astype(BF16), final_g.reshape(1, D))


def _trunk(x, mod, s5_re0, s5_im0, gdn_s0, gdn_conv0, w):
    B, L, D = x.shape
    G, P = w["s5_lambda_re"].shape[1:]
    n = G * P

    a2r, a2i, *ops = _s5_disc(w["s5_log_step"][0], w["s5_lambda_re"][0], w["s5_lambda_im"][0],
                              w["s5_b_re"][0], w["s5_b_im"][0], w["s5_c_re"][0], w["s5_c_im"][0])
    wb, wc, wd = _s5_block_weights(*ops)
    x1, hr, hi = _s5_layer(x, mod[0].transpose(1, 0, 2), w["norm_g"][0], w["s5_w_in"][0], a2r, a2i,
                           wb, wc, wd, s5_re0[0].reshape(B, n), s5_im0[0].reshape(B, n),
                           w["s5_d"][0], w["s5_w_glu"][0], w["s5_b_glu"][0], w["s5_w_out"][0])

    v_width = w["gdn_w_out"].shape[1]
    act, z2, bgc, hist8 = _gdn_in(x1, mod[1], w["norm_g"][1], w["gdn_w_in"][0], w["gdn_a_log"][0],
                                  w["gdn_dt_bias"][0], w["gdn_conv_w"][0], gdn_conv0[0], v_width)
    o, s_new = _gdn_core(act, bgc, gdn_s0[0])
    y = _gdn_out(o, z2, x1, mod[1], w["gdn_norm_g"][0], w["gdn_w_out"][0], w["final_g"])
    new_hist = hist8[:, HIST_ROWS - (GDN_CONV - 1):, :]
    return (y, hr.reshape(1, B, G, P), hi.reshape(1, B, G, P), s_new[None], new_hist[None])


def kernel(x_prompt, x_sample, c_prompt, c_sample, state_s5_re, state_s5_im, state_gdn, state_gdn_conv, norm_g, w_ada, b_ada, s5_w_in, s5_log_step, s5_lambda_re, s5_lambda_im, s5_b_re, s5_b_im, s5_c_re, s5_c_im, s5_d, s5_w_glu, s5_b_glu, s5_w_out, gdn_w_in, gdn_conv_w, gdn_a_log, gdn_dt_bias, gdn_norm_g, gdn_w_out, final_g):
    w = dict(norm_g=norm_g, s5_w_in=s5_w_in, s5_log_step=s5_log_step, s5_lambda_re=s5_lambda_re,
             s5_lambda_im=s5_lambda_im, s5_b_re=s5_b_re, s5_b_im=s5_b_im, s5_c_re=s5_c_re,
             s5_c_im=s5_c_im, s5_d=s5_d, s5_w_glu=s5_w_glu, s5_b_glu=s5_b_glu, s5_w_out=s5_w_out,
             gdn_w_in=gdn_w_in, gdn_conv_w=gdn_conv_w, gdn_a_log=gdn_a_log,
             gdn_dt_bias=gdn_dt_bias, gdn_norm_g=gdn_norm_g, gdn_w_out=gdn_w_out, final_g=final_g)
    bp, _, d = x_prompt.shape
    bs = x_sample.shape[0]
    depth = w_ada.shape[0]
    mod = _ada_mod(jnp.concatenate([c_prompt, c_sample], axis=0), w_ada, b_ada)
    mod = mod.reshape(depth, bp + bs, 3, d)

    z_s5 = jnp.zeros((state_s5_re.shape[0], bp) + state_s5_re.shape[2:], F32)
    z_gdn = jnp.zeros((state_gdn.shape[0], bp) + state_gdn.shape[2:], F32)
    z_conv = jnp.zeros((state_gdn_conv.shape[0], bp) + state_gdn_conv.shape[2:], F32)
    yp, s5r_p, s5i_p, gdn_p, conv_p = _trunk(x_prompt, mod[:, :bp], z_s5, z_s5, z_gdn, z_conv, w)
    ys, s5r_s, s5i_s, gdn_s, conv_s = _trunk(x_sample, mod[:, bp:], state_s5_re, state_s5_im,
                                             state_gdn, state_gdn_conv, w)
    return (yp, ys, s5r_p, s5i_p, gdn_p, conv_p, s5r_s, s5i_s, gdn_s, conv_s)
```

```python
import functools
import math

import jax
import jax.numpy as jnp
from jax import lax
from jax.experimental import pallas as pl
from jax.experimental.pallas import tpu as pltpu

F32 = jnp.float32
BF16 = jnp.bfloat16
EPS = 1e-6

S5_GROUP = 16
S5_STATE = 64
S5_BLOCK_GROUPS = 8
GDN_DK = 128
GDN_DV = 128
GDN_CONV = 4
GDN_CHUNK = 64
GDN_ROWS_PER_STEP = 2
INV_BLOCK = 16
HIST_ROWS = 8
LANES = 128
VMEM_LIMIT = 56 * 1024 * 1024
HI = lax.Precision.HIGHEST
NT_DIMS = (((1,), (1,)), ((), ()))


def _params(*sem):
    return pltpu.CompilerParams(dimension_semantics=sem, vmem_limit_bytes=VMEM_LIMIT)


def _dot(a, b):
    return jnp.dot(a.astype(BF16), b.astype(BF16), preferred_element_type=F32)


def _row_tile(L, rows=512):
    return min(rows, L)


def _ada_kernel(c_ref, w_ref, b_ref, o_ref):
    c = c_ref[...]
    o_ref[0] = _dot(jax.nn.silu(c), w_ref[0]) + b_ref[0]


def _ada_mod(c_all, w_ada, b_ada):
    depth, d, d3 = w_ada.shape
    r = c_all.shape[0]
    tn = 768
    return pl.pallas_call(
        _ada_kernel,
        grid=(depth, d3 // tn),
        in_specs=[
            pl.BlockSpec((r, d), lambda i, j: (0, 0)),
            pl.BlockSpec((1, d, tn), lambda i, j: (i, 0, j)),
            pl.BlockSpec((1, 1, tn), lambda i, j: (i, 0, j)),
        ],
        out_specs=pl.BlockSpec((1, r, tn), lambda i, j: (i, 0, j)),
        out_shape=jax.ShapeDtypeStruct((depth, r, d3), F32),
        compiler_params=_params("parallel", "parallel"),
        name="ada_mod",
    )(c_all, w_ada.astype(BF16), b_ada.reshape(depth, 1, d3))


def _norm_mod(x, g, m):
    h = x * lax.rsqrt(jnp.mean(x * x, axis=-1, keepdims=True) + EPS) * g
    return h * (1.0 + m[1:2]) + m[0:1]


def _s5_disc_kernel(ls_ref, lr_ref, li_ref, br_ref, bi_ref, cr_ref, ci_ref, seg_ref,
                    a2r_ref, a2i_ref, bbr_ref, bbi_ref, abr_ref, abi_ref, car_ref, cai_ref,
                    ca2r_ref, ca2i_ref, k0_ref, k1_ref):
    step = jnp.exp(ls_ref[...])
    lr = lr_ref[...]
    li = li_ref[...]
    mag = jnp.exp(lr * step)
    ar = mag * jnp.cos(li * step)
    ai = mag * jnp.sin(li * step)
    den = lr * lr + li * li
    xr = ar - 1.0
    nr = (xr * lr + ai * li) / den
    ni = (ai * lr - xr * li) / den
    bbr = nr * br_ref[...] - ni * bi_ref[...]
    bbi = nr * bi_ref[...] + ni * br_ref[...]
    abr = ar * bbr - ai * bbi
    abi = ar * bbi + ai * bbr
    a2r = ar * ar - ai * ai
    a2i = 2.0 * (ar * ai)
    cr = cr_ref[...]
    ci = ci_ref[...]
    a2r_ref[...] = a2r
    a2i_ref[...] = a2i
    bbr_ref[...] = bbr
    bbi_ref[...] = bbi
    abr_ref[...] = abr
    abi_ref[...] = abi
    car_ref[...] = cr * ar - ci * ai
    cai_ref[...] = cr * ai + ci * ar
    ca2r_ref[...] = cr * a2r - ci * a2i
    ca2i_ref[...] = cr * a2i + ci * a2r
    nc = br_ref.shape[0]
    prod0 = jnp.concatenate([cr * bbr[c:c + 1] - ci * bbi[c:c + 1] for c in range(nc)], axis=0)
    prod1 = jnp.concatenate([cr * abr[c:c + 1] - ci * abi[c:c + 1] for c in range(nc)], axis=0)
    seg = seg_ref[...]
    k0_ref[...] = jnp.dot(prod0, seg, precision=HI, preferred_element_type=F32)
    k1_ref[...] = jnp.dot(prod1, seg, precision=HI, preferred_element_type=F32)


def _s5_disc(log_step, lam_re, lam_im, b_re, b_im, c_re, c_im):
    G, P = lam_re.shape
    n = G * P
    c = b_re.shape[-1]
    ls = jnp.broadcast_to(log_step[:, None], (G, P)).reshape(1, n)
    brT = b_re.transpose(2, 0, 1).reshape(c, n)
    biT = b_im.transpose(2, 0, 1).reshape(c, n)
    crT = c_re.transpose(1, 0, 2).reshape(c, n)
    ciT = c_im.transpose(1, 0, 2).reshape(c, n)
    seg = jnp.repeat(jnp.eye(G, dtype=F32), P, axis=0)
    vec = jax.ShapeDtypeStruct((1, n), F32)
    mat = jax.ShapeDtypeStruct((c, n), F32)
    kmat = jax.ShapeDtypeStruct((c * c, G), F32)
    return pl.pallas_call(
        _s5_disc_kernel,
        out_shape=[vec, vec] + [mat] * 8 + [kmat, kmat],
        name="s5_disc",
    )(ls, lam_re.reshape(1, n), lam_im.reshape(1, n), brT, biT, crT, ciT, seg)


def _s5_block_weights(bbrT, bbiT, abrT, abiT, carT, caiT, ca2rT, ca2iT, k0, k1):
    nb = S5_BLOCK_GROUPS
    G = k0.shape[-1]
    nblk = G // nb
    eye = jnp.eye(nb, dtype=F32)

    def bdiag_b(bbT):
        t = bbT.reshape(S5_GROUP, nblk, nb, S5_STATE)
        w = jnp.einsum("cjhp,gh->jgchp", t, eye)
        return w.reshape(nblk, nb * S5_GROUP, nb * S5_STATE)

    def bdiag_c(cT):
        t = cT.reshape(S5_GROUP, nblk, nb, S5_STATE)
        w = jnp.einsum("ojgp,gh->jhpgo", t, eye)
        return w.reshape(nblk, nb * S5_STATE, nb * S5_GROUP)

    def bdiag_k(k):
        t = k.reshape(S5_GROUP, S5_GROUP, nblk, nb)
        w = jnp.einsum("iojg,gh->jgiho", t, eye)
        return w.reshape(nblk, nb * S5_GROUP, nb * S5_GROUP)

    wb = jnp.concatenate([
        jnp.concatenate([bdiag_b(abrT), bdiag_b(abiT)], axis=2),
        jnp.concatenate([bdiag_b(bbrT), bdiag_b(bbiT)], axis=2)], axis=1).astype(BF16)
    wc = jnp.concatenate([
        jnp.concatenate([bdiag_c(carT), bdiag_c(ca2rT)], axis=2),
        jnp.concatenate([-bdiag_c(caiT), -bdiag_c(ca2iT)], axis=2)], axis=1).astype(BF16)
    d0, d1 = bdiag_k(k0), bdiag_k(k1)
    wd = jnp.concatenate([
        jnp.concatenate([d0, d1], axis=2),
        jnp.concatenate([jnp.zeros_like(d0), d0], axis=2)], axis=1).astype(BF16)
    return wb, wc, wd


def _s5_layer_kernel(x_ref, mod_ref, g_ref, win_ref, ar_ref, ai_ref, wb_ref, wc_ref, wd_ref,
                     hr0_ref, hi0_ref, d_ref, wg_ref, bglu_ref, wo_ref, o_ref, hr_ref, hi_ref,
                     u_scr, z_scr, y_scr, bu_scr, xr_scr, xi_scr, *, steps, batch):
    i = pl.program_id(0)
    rows = steps * batch
    pairs = steps // 2
    prow = pairs * batch
    d_model = x_ref.shape[-1]
    e = u_scr.shape[-1]
    nblk = wb_ref.shape[0]
    half = wb_ref.shape[2] // 2
    ulanes = wb_ref.shape[1] // 2

    @pl.when(i == 0)
    def _():
        xr_scr[...] = hr0_ref[...]
        xi_scr[...] = hi0_ref[...]

    x3 = jnp.swapaxes(x_ref[...], 0, 1)
    m = mod_ref[...]
    h3 = x3 * lax.rsqrt(jnp.mean(x3 * x3, axis=-1, keepdims=True) + EPS) * g_ref[...]
    h3 = h3 * (1.0 + m[1]) + m[0]
    p = _dot(h3.reshape(rows, d_model), win_ref[...])
    u_scr[...] = p[:, :e].reshape(pairs, 2, batch, e)
    z_scr[...] = p[:, e:]

    for j in range(nblk):
        sl = slice(j * half, (j + 1) * half)
        ch = slice(j * ulanes, (j + 1) * ulanes)
        buf = bu_scr.at[j % 2]
        lhs = jnp.concatenate([u_scr[:, 0, :, ch].reshape(prow, ulanes),
                               u_scr[:, 1, :, ch].reshape(prow, ulanes)], axis=1).astype(BF16)
        buf[batch:batch + prow, :] = jnp.dot(lhs, wb_ref[j], preferred_element_type=F32)
        ar = jnp.broadcast_to(ar_ref[:, sl], (batch, half))
        ai = jnp.broadcast_to(ai_ref[:, sl], (batch, half))
        xr = xr_scr[:, sl]
        xi = xi_scr[:, sl]
        buf[0:batch, 0:half] = xr
        buf[0:batch, half:2 * half] = xi
        for t in range(1, pairs + 1):
            rows_t = slice(t * batch, (t + 1) * batch)
            xr, xi = (ar * xr - ai * xi + buf[rows_t, 0:half],
                      ar * xi + ai * xr + buf[rows_t, half:2 * half])
            buf[rows_t, 0:half] = xr
            buf[rows_t, half:2 * half] = xi
        xr_scr[:, sl] = xr
        xi_scr[:, sl] = xi
        ypair = _dot(buf[0:prow, :], wc_ref[j]) + jnp.dot(lhs, wd_ref[j],
                                                          preferred_element_type=F32)
        y_scr[:, 0, :, ch] = ypair[:, :ulanes].reshape(pairs, batch, ulanes)
        y_scr[:, 1, :, ch] = ypair[:, ulanes:].reshape(pairs, batch, ulanes)

    y = jax.nn.gelu(y_scr[...].reshape(rows, e) + d_ref[...] * u_scr[...].reshape(rows, e))
    y = y * jax.nn.sigmoid(_dot(y, wg_ref[...]) + bglu_ref[...])
    y = y * jax.nn.silu(z_scr[...])
    out3 = _dot(y, wo_ref[...]).reshape(steps, batch, d_model)
    o_ref[...] = jnp.swapaxes(x3 + m[2] * out3, 0, 1)

    @pl.when(i == pl.num_programs(0) - 1)
    def _():
        hr_ref[...] = xr_scr[...]
        hi_ref[...] = xi_scr[...]


def _s5_layer(x, mod, g, w_in, ar, ai, wb, wc, wd, hr0, hi0, d, w_glu, b_glu, w_out):
    B, L, D = x.shape
    E = w_in.shape[1] // 2
    n = ar.shape[1]
    steps = min(64, L)
    assert steps % 2 == 0 and L % steps == 0
    rows = steps * B
    pairs = steps // 2
    const2 = lambda i: (0, 0)
    const3 = lambda i: (0, 0, 0)

    def resident(shape):
        return pl.BlockSpec(shape, const2 if len(shape) == 2 else const3,
                            pipeline_mode=pl.Buffered(1))

    return pl.pallas_call(
        functools.partial(_s5_layer_kernel, steps=steps, batch=B),
        grid=(L // steps,),
        in_specs=[
            pl.BlockSpec((B, steps, D), lambda i: (0, i, 0)),
            resident((3, B, D)),
            resident((1, D)),
            resident((D, 2 * E)),
            resident((1, n)),
            resident((1, n)),
            resident(wb.shape),
            resident(wc.shape),
            resident(wd.shape),
            resident((B, n)),
            resident((B, n)),
            resident((1, E)),
            resident((E, E)),
            resident((1, E)),
            resident((E, D)),
        ],
        out_specs=[
            pl.BlockSpec((B, steps, D), lambda i: (0, i, 0)),
            pl.BlockSpec((B, n), const2),
            pl.BlockSpec((B, n), const2),
        ],
        out_shape=[
            jax.ShapeDtypeStruct((B, L, D), F32),
            jax.ShapeDtypeStruct((B, n), F32),
            jax.ShapeDtypeStruct((B, n), F32),
        ],
        scratch_shapes=[
            pltpu.VMEM((pairs, 2, B, E), F32),
            pltpu.VMEM((rows, E), F32),
            pltpu.VMEM((pairs, 2, B, E), F32),
            pltpu.VMEM((2, (pairs + 1) * B, wb.shape[2]), F32),
            pltpu.VMEM((B, n), F32),
            pltpu.VMEM((B, n), F32),
        ],
        compiler_params=_params("arbitrary"),
        name="s5_layer",
    )(x, mod, g.reshape(1, D), w_in.astype(BF16), ar, ai, wb, wc, wd, hr0, hi0, d.reshape(1, E),
      w_glu.astype(BF16), b_glu.reshape(1, E), w_out.astype(BF16))


def _l2norm(x):
    return x * lax.rsqrt(jnp.sum(x * x, axis=-1, keepdims=True) + EPS)


def _gdn_chunk(L):
    return GDN_CHUNK if L % GDN_CHUNK == 0 else L


def _split3(x):
    hi = x.astype(BF16)
    r = x - hi.astype(F32)
    mid = r.astype(BF16)
    lo = (r - mid.astype(F32)).astype(BF16)
    return hi, mid, lo


def _gdn_in_kernel(x_ref, mod_ref, g_ref, w_ref, alog_ref, dtb_ref, cw_ref,
                   hist_ref, act_ref, z_ref, bgc_ref, hist_out_ref, ext_scr, tri_scr,
                   *, heads, qk_heads, chunk):
    i = pl.program_id(1)
    tm = x_ref.shape[1]
    qk_width = qk_heads * GDN_DK
    conv_ch = act_ref.shape[-1]
    cb = qk_width

    @pl.when(i == 0)
    def _():
        ext_scr[0:HIST_ROWS, :] = hist_ref[0]
        row = lax.broadcasted_iota(jnp.int32, (tm, tm), 0)
        col = lax.broadcasted_iota(jnp.int32, (tm, tm), 1)
        sh = int(math.log2(chunk))
        tri_scr[...] = ((row >= col) & ((row >> sh) == (col >> sh))).astype(BF16)

    h = _norm_mod(x_ref[0], g_ref[...], mod_ref[0]).astype(BF16)
    cw = cw_ref[...]
    assert GDN_CONV == 4 and HIST_ROWS >= GDN_CONV - 1

    nblk = conv_ch // cb
    zb = z_ref.shape[-1] // nblk
    wcols = cb + zb

    def project(nb):
        width = wcols + (LANES if nb == 0 else 0)
        start = nb * wcols + (LANES if nb > 0 else 0)
        r = jnp.dot(h, w_ref[:, start:start + width], preferred_element_type=F32)
        ext_scr[HIST_ROWS:HIST_ROWS + tm, nb * cb:(nb + 1) * cb] = r[:, :cb]
        z_ref[0, :, nb * zb:(nb + 1) * zb] = r[:, cb:wcols].astype(z_ref.dtype)
        return r[:, wcols:]

    def conv_act(nb):
        cols = slice(nb * cb, (nb + 1) * cb)
        e = ext_scr[:, cols]
        e1 = pltpu.roll(e, 1, 0)
        wj = cw[:, cols]
        near = e * wj[3:4] + e1 * wj[2:3]
        far = pltpu.roll(e * wj[1:2] + e1 * wj[0:1], 2, 0)
        conv = (near + far)[HIST_ROWS:]
        ext_scr[0:HIST_ROWS, cols] = ext_scr[tm:tm + HIST_ROWS, cols]
        act = jax.nn.silu(conv)
        for hh in range(cb // GDN_DK):
            a = act[:, hh * GDN_DK:(hh + 1) * GDN_DK]
            if nb == 0:
                a = _l2norm(a) * (GDN_DK ** -0.5)
            elif nb == 1:
                a = _l2norm(a)
            act_ref[0, :, nb * cb + hh * GDN_DK:nb * cb + (hh + 1) * GDN_DK] = a

    for nb in range(nblk):
        tail = project(nb)
        if nb == 0:
            ba = tail
        conv_act(nb)

    beta = jax.nn.sigmoid(ba)
    a = ba + dtb_ref[...]
    softplus = jnp.maximum(a, 0.0) + jnp.log(1.0 + jnp.exp(-jnp.abs(a)))
    lane = lax.broadcasted_iota(jnp.int32, ba.shape, 1)
    live = (lane >= heads) & (lane < 2 * heads)
    g = jnp.where(live, -jnp.exp(alog_ref[...]) * softplus, 0.0)
    g_hi, g_mid, g_lo = _split3(g)
    packed = (g_hi.astype(F32) + pltpu.roll(g_mid.astype(F32), heads, 1)
              + pltpu.roll(g_lo.astype(F32), 2 * heads, 1)).astype(BF16)
    r = jnp.dot(tri_scr[...], packed, preferred_element_type=F32)
    gc = r + pltpu.roll(r, LANES - heads, 1) + pltpu.roll(r, LANES - 2 * heads, 1)
    bgc_ref[0] = jnp.where(lane < heads, beta, gc)

    @pl.when(i == pl.num_programs(1) - 1)
    def _():
        hist_out_ref[0] = ext_scr[0:HIST_ROWS, :]


def _gdn_in(x, mod, g, w_in, a_log, dt_bias, conv_w, hist, v_width):
    B, L, D = x.shape
    H = a_log.shape[0]
    conv_ch = conv_w.shape[-1]
    tm = _row_tile(L)
    o1, o2 = conv_ch, conv_ch + v_width
    wba = jnp.zeros((D, LANES), F32).at[:, :2 * H].set(w_in[:, o2:o2 + 2 * H])
    alog = jnp.zeros((1, LANES), F32).at[0, H:2 * H].set(a_log)
    dtb = jnp.zeros((1, LANES), F32).at[0, H:2 * H].set(dt_bias)
    hist8 = jnp.zeros((B, HIST_ROWS, conv_ch), F32).at[:, HIST_ROWS - (GDN_CONV - 1):, :].set(hist)
    cb = (H // 2) * GDN_DK
    nblk = conv_ch // cb
    zb = v_width // nblk
    pieces = []
    for nb in range(nblk):
        pieces += [w_in[:, nb * cb:(nb + 1) * cb], w_in[:, o1 + nb * zb:o1 + (nb + 1) * zb]]
        if nb == 0:
            pieces.append(wba)
    w_cat = jnp.concatenate(pieces, axis=1)
    const = lambda b, i: (0, 0)

    def resident(shape):
        return pl.BlockSpec(shape, const, pipeline_mode=pl.Buffered(1))

    return pl.pallas_call(
        functools.partial(_gdn_in_kernel, heads=H, qk_heads=H // 2, chunk=_gdn_chunk(L)),
        grid=(B, L // tm),
        in_specs=[
            pl.BlockSpec((1, tm, D), lambda b, i: (b, i, 0)),
            pl.BlockSpec((1, 3, D), lambda b, i: (b, 0, 0)),
            resident((1, D)),
            resident((D, conv_ch + v_width + LANES)),
            resident((1, LANES)),
            resident((1, LANES)),
            resident((GDN_CONV, conv_ch)),
            pl.BlockSpec((1, HIST_ROWS, conv_ch), lambda b, i: (b, 0, 0)),
        ],
        out_specs=[
            pl.BlockSpec((1, tm, conv_ch), lambda b, i: (b, i, 0)),
            pl.BlockSpec((1, tm, v_width), lambda b, i: (b, i, 0)),
            pl.BlockSpec((1, tm, LANES), lambda b, i: (b, i, 0)),
            pl.BlockSpec((1, HIST_ROWS, conv_ch), lambda b, i: (b, 0, 0)),
        ],
        out_shape=[
            jax.ShapeDtypeStruct((B, L, conv_ch), F32),
            jax.ShapeDtypeStruct((B, L, v_width), BF16),
            jax.ShapeDtypeStruct((B, L, LANES), F32),
            jax.ShapeDtypeStruct((B, HIST_ROWS, conv_ch), F32),
        ],
        scratch_shapes=[pltpu.VMEM((HIST_ROWS + tm, conv_ch), F32), pltpu.VMEM((tm, tm), BF16)],
        compiler_params=_params("parallel", "arbitrary"),
        name="gdn_in",
    )(x, mod, g.reshape(1, D), w_cat.astype(BF16), alog, dtb, conv_w, hist8)


def _gdn_core_kernel(act_ref, bgc_ref, s0_ref, o_ref, sout_ref, s_scr, *, T, qk_heads, v_heads):
    c = pl.program_id(1)
    nrows = act_ref.shape[0]
    qk_width = qk_heads * GDN_DK
    rep = v_heads // qk_heads

    @pl.when(c == 0)
    def _():
        s_scr[...] = s0_ref[...]

    row = lax.broadcasted_iota(jnp.int32, (T, T), 0)
    col = lax.broadcasted_iota(jnp.int32, (T, T), 1)
    causal = row >= col
    strict = row > col
    eye = (lax.broadcasted_iota(jnp.int32, (LANES, LANES), 0)
           == lax.broadcasted_iota(jnp.int32, (LANES, LANES), 1)).astype(BF16)
    bgs, gcts = [], []
    for r in range(nrows):
        bg = bgc_ref[r]
        gc_hi, gc_mid, gc_lo = _split3(bg)
        bgs.append(bg)
        gcts.append(lax.dot_general(eye, gc_hi, NT_DIMS, preferred_element_type=F32)
                    + lax.dot_general(eye, gc_mid, NT_DIMS, preferred_element_type=F32)
                    + lax.dot_general(eye, gc_lo, NT_DIMS, preferred_element_type=F32))
    sh = int(math.log2(INV_BLOCK))
    diag_blk = (row >> sh) == (col >> sh)
    merges = []
    while (1 << sh) < T:
        merges.append(((row >> (sh + 1)) == (col >> (sh + 1))) & ((row >> sh) > (col >> sh)))
        sh += 1

    qkp = [(r, hq) for r in range(nrows) for hq in range(qk_heads)]
    units = [(r, h) for r in range(nrows) for h in range(v_heads)]
    n = range(len(units))
    qk_of = [r * qk_heads + h // rep for r, h in units]
    qs = [act_ref[r, :, hq * GDN_DK:(hq + 1) * GDN_DK] for r, hq in qkp]
    ks = [act_ref[r, :, qk_width + hq * GDN_DK:qk_width + (hq + 1) * GDN_DK] for r, hq in qkp]
    kq = [lax.dot_general(jnp.concatenate([ks[i], qs[i]], axis=0).astype(BF16),
                          ks[i].astype(BF16), NT_DIMS, preferred_element_type=F32)
          for i in range(len(qkp))]
    kks = [kq[i][:T] for i in qk_of]
    qks = [kq[i][T:] for i in qk_of]
    qu = [qs[i] for i in qk_of]
    ku = [ks[i] for i in qk_of]
    beta = [bgs[r][:, h:h + 1] for r, h in units]
    gcol = [bgs[r][:, v_heads + h:v_heads + h + 1] for r, h in units]
    grow = [gcts[r][v_heads + h:v_heads + h + 1, :] for r, h in units]
    decay = [jnp.where(causal, jnp.exp(jnp.where(causal, gcol[u] - grow[u], 0.0)), 0.0) for u in n]
    egc = [jnp.exp(gcol[u]) for u in n]
    m = [jnp.where(strict, (beta[u] * kks[u]) * decay[u], 0.0) for u in n]
    p = [jnp.where(diag_blk, -m[u], 0.0) for u in n]
    e = list(p)
    for _ in range(int(math.log2(INV_BLOCK)) - 1):
        p = [_dot(p[u], p[u]) for u in n]
        e = [e[u] + p[u] + _dot(e[u], p[u]) for u in n]
    for blk in merges:
        m21 = [jnp.where(blk, m[u], 0.0) for u in n]
        y = [m21[u] + _dot(m21[u], e[u]) for u in n]
        e = [e[u] - (y[u] + _dot(e[u], y[u])) for u in n]
    rhs = [jnp.concatenate(
        [act_ref[r, :, 2 * qk_width + h * GDN_DV:2 * qk_width + (h + 1) * GDN_DV] * beta[u],
         ku[u] * (beta[u] * egc[u])], axis=1) for u, (r, h) in enumerate(units)]
    sol = [rhs[u] + _dot(e[u], rhs[u]) for u in n]
    s = [s_scr[r, h] for r, h in units]
    ws = [_dot(jnp.concatenate([sol[u][:, GDN_DV:], qu[u] * egc[u]], axis=0), s[u]) for u in n]
    v_new = [sol[u][:, :GDN_DV] - ws[u][:T] for u in n]
    for u, (r, h) in enumerate(units):
        o_ref[r, :, h * GDN_DV:(h + 1) * GDN_DV] = (
            ws[u][T:] + _dot(qks[u] * decay[u], v_new[u])).astype(o_ref.dtype)
    g_last = [grow[u][:, T - 1:T] for u in n]
    for u, (r, h) in enumerate(units):
        kd = ku[u] * jnp.exp(g_last[u] - gcol[u])
        s_scr[r, h] = s[u] * jnp.exp(g_last[u]) + lax.dot_general(
            kd.astype(BF16), v_new[u].astype(BF16), (((0,), (0,)), ((), ())),
            preferred_element_type=F32)

    @pl.when(c == pl.num_programs(1) - 1)
    def _():
        sout_ref[...] = s_scr[...]


def _gdn_core(act, bgc, s0):
    B, L, C = act.shape
    H = s0.shape[1]
    T = _gdn_chunk(L)
    nr = GDN_ROWS_PER_STEP
    return pl.pallas_call(
        functools.partial(_gdn_core_kernel, T=T, qk_heads=H // 2, v_heads=H),
        grid=(B // nr, L // T),
        in_specs=[
            pl.BlockSpec((nr, T, C), lambda b, c: (b, c, 0)),
            pl.BlockSpec((nr, T, LANES), lambda b, c: (b, c, 0)),
            pl.BlockSpec((nr, H, GDN_DK, GDN_DV), lambda b, c: (b, 0, 0, 0)),
        ],
        out_specs=[
            pl.BlockSpec((nr, T, H * GDN_DV), lambda b, c: (b, c, 0)),
            pl.BlockSpec((nr, H, GDN_DK, GDN_DV), lambda b, c: (b, 0, 0, 0)),
        ],
        out_shape=[
            jax.ShapeDtypeStruct((B, L, H * GDN_DV), BF16),
            jax.ShapeDtypeStruct((B, H, GDN_DK, GDN_DV), F32),
        ],
        scratch_shapes=[pltpu.VMEM((nr, H, GDN_DK, GDN_DV), F32)],
        compiler_params=_params("parallel", "arbitrary"),
        name="gdn_core",
    )(act, bgc, s0)


def _gdn_out_kernel(o_ref, z_ref, x_ref, mod_ref, ng_ref, wo_ref, fg_ref, y_ref, *, heads):
    o = o_ref[0].astype(F32)
    parts = []
    for h in range(heads):
        oh = o[:, h * GDN_DV:(h + 1) * GDN_DV]
        parts.append(oh * lax.rsqrt(jnp.mean(oh * oh, axis=-1, keepdims=True) + EPS))
    on = jnp.concatenate(parts, axis=1) * ng_ref[...]
    out = _dot(on * jax.nn.silu(z_ref[0].astype(F32)), wo_ref[...])
    x2 = x_ref[0] + mod_ref[0][2:3] * out
    y_ref[0] = x2 * lax.rsqrt(jnp.mean(x2 * x2, axis=-1, keepdims=True) + EPS) * fg_ref[...]


def _gdn_out(o, z, x, mod, norm_g, w_out, final_g):
    B, L, D = x.shape
    V = o.shape[-1]
    H = V // GDN_DV
    tm = _row_tile(L, 1024)
    const = lambda b, i: (0, 0)
    return pl.pallas_call(
        functools.partial(_gdn_out_kernel, heads=H),
        grid=(B, L // tm),
        in_specs=[
            pl.BlockSpec((1, tm, V), lambda b, i: (b, i, 0)),
            pl.BlockSpec((1, tm, V), lambda b, i: (b, i, 0)),
            pl.BlockSpec((1, tm, D), lambda b, i: (b, i, 0)),
            pl.BlockSpec((1, 3, D), lambda b, i: (b, 0, 0)),
            pl.BlockSpec((1, V), const),
            pl.BlockSpec((V, D), const),
            pl.BlockSpec((1, D), const),
        ],
        out_specs=pl.BlockSpec((1, tm, D), lambda b, i: (b, i, 0)),
        out_shape=jax.ShapeDtypeStruct((B, L, D), F32),
        compiler_params=_params("parallel", "parallel"),
        name="gdn_out",
    )(o, z, x, mod, jnp.tile(norm_g, H).reshape(1, V), w_out.astype(BF16), final_g.reshape(1, D))


def _trunk(x, mod, s5_re0, s5_im0, gdn_s0, gdn_conv0, w):
    B, L, D = x.shape
    G, P = w["s5_lambda_re"].shape[1:]
    n = G * P

    a2r, a2i, *ops = _s5_disc(w["s5_log_step"][0], w["s5_lambda_re"][0], w["s5_lambda_im"][0],
                              w["s5_b_re"][0], w["s5_b_im"][0], w["s5_c_re"][0], w["s5_c_im"][0])
    wb, wc, wd = _s5_block_weights(*ops)
    x1, hr, hi = _s5_layer(x, mod[0].transpose(1, 0, 2), w["norm_g"][0], w["s5_w_in"][0], a2r, a2i,
                           wb, wc, wd, s5_re0[0].reshape(B, n), s5_im0[0].reshape(B, n),
                           w["s5_d"][0], w["s5_w_glu"][0], w["s5_b_glu"][0], w["s5_w_out"][0])

    v_width = w["gdn_w_out"].shape[1]
    act, z2, bgc, hist8 = _gdn_in(x1, mod[1], w["norm_g"][1], w["gdn_w_in"][0], w["gdn_a_log"][0],
                                  w["gdn_dt_bias"][0], w["gdn_conv_w"][0], gdn_conv0[0], v_width)
    o, s_new = _gdn_core(act, bgc, gdn_s0[0])
    y = _gdn_out(o, z2, x1, mod[1], w["gdn_norm_g"][0], w["gdn_w_out"][0], w["final_g"])
    new_hist = hist8[:, HIST_ROWS - (GDN_CONV - 1):, :]
    return (y, hr.reshape(1, B, G, P), hi.reshape(1, B, G, P), s_new[None], new_hist[None])


def kernel(x_prompt, x_sample, c_prompt, c_sample, state_s5_re, state_s5_im, state_gdn, state_gdn_conv, norm_g, w_ada, b_ada, s5_w_in, s5_log_step, s5_lambda_re, s5_lambda_im, s5_b_re, s5_b_im, s5_c_re, s5_c_im, s5_d, s5_w_glu, s5_b_glu, s5_w_out, gdn_w_in, gdn_conv_w, gdn_a_log, gdn_dt_bias, gdn_norm_g, gdn_w_out, final_g):
    w = dict(norm_g=norm_g, s5_w_in=s5_w_in, s5_log_step=s5_log_step, s5_lambda_re=s5_lambda_re,
             s5_lambda_im=s5_lambda_im, s5_b_re=s5_b_re, s5_b_im=s5_b_im, s5_c_re=s5_c_re,
             s5_c_im=s5_c_im, s5_d=s5_d, s5_w_glu=s5_w_glu, s5_b_glu=s5_b_glu, s5_w_out=s5_w_out,
             gdn_w_in=gdn_w_in, gdn_conv_w=gdn_conv_w, gdn_a_log=gdn_a_log,
             gdn_dt_bias=gdn_dt_bias, gdn_norm_g=gdn_norm_g, gdn_w_out=gdn_w_out, final_g=final_g)
    bp, _, d = x_prompt.shape
    bs = x_sample.shape[0]
    depth = w_ada.shape[0]
    mod = _ada_mod(jnp.concatenate([c_prompt, c_sample], axis=0), w_ada, b_ada)
    mod = mod.reshape(depth, bp + bs, 3, d)

    z_s5 = jnp.zeros((state_s5_re.shape[0], bp) + state_s5_re.shape[2:], F32)
    z_gdn = jnp.zeros((state_gdn.shape[0], bp) + state_gdn.shape[2:], F32)
    z_conv = jnp.zeros((state_gdn_conv.shape[0], bp) + state_gdn_conv.shape[2:], F32)
    yp, s5r_p, s5i_p, gdn_p, conv_p = _trunk(x_prompt, mod[:, :bp], z_s5, z_s5, z_gdn, z_conv, w)
    ys, s5r_s, s5i_s, gdn_s, conv_s = _trunk(x_sample, mod[:, bp:], state_s5_re, state_s5_im,
                                             state_gdn, state_gdn_conv, w)
    return (yp, ys, s5r_p, s5i_p, gdn_p, conv_p, s5r_s, s5i_s, gdn_s, conv_s)
```

```python
import functools
import math

import jax
import jax.numpy as jnp
from jax import lax
from jax.experimental import pallas as pl
from jax.experimental.pallas import tpu as pltpu

F32 = jnp.float32
BF16 = jnp.bfloat16
EPS = 1e-6

S5_GROUP = 16
S5_STATE = 64
S5_BLOCK_GROUPS = 8
GDN_DK = 128
GDN_DV = 128
GDN_CONV = 4
GDN_CHUNK = 64
GDN_ROWS_PER_STEP = 2
INV_BLOCK = 16
HIST_ROWS = 8
LANES = 128
VMEM_LIMIT = 56 * 1024 * 1024
HI = lax.Precision.HIGHEST
NT_DIMS = (((1,), (1,)), ((), ()))


def _params(*sem):
    return pltpu.CompilerParams(dimension_semantics=sem, vmem_limit_bytes=VMEM_LIMIT)


def _dot(a, b):
    return jnp.dot(a.astype(BF16), b.astype(BF16), preferred_element_type=F32)


def _row_tile(L, rows=512):
    return min(rows, L)


def _ada_kernel(c_ref, w_ref, b_ref, o_ref):
    c = c_ref[...]
    o_ref[0] = _dot(jax.nn.silu(c), w_ref[0]) + b_ref[0]


def _ada_mod(c_all, w_ada, b_ada):
    depth, d, d3 = w_ada.shape
    r = c_all.shape[0]
    tn = 768
    return pl.pallas_call(
        _ada_kernel,
        grid=(depth, d3 // tn),
        in_specs=[
            pl.BlockSpec((r, d), lambda i, j: (0, 0)),
            pl.BlockSpec((1, d, tn), lambda i, j: (i, 0, j)),
            pl.BlockSpec((1, 1, tn), lambda i, j: (i, 0, j)),
        ],
        out_specs=pl.BlockSpec((1, r, tn), lambda i, j: (i, 0, j)),
        out_shape=jax.ShapeDtypeStruct((depth, r, d3), F32),
        compiler_params=_params("parallel", "parallel"),
        name="ada_mod",
    )(c_all, w_ada.astype(BF16), b_ada.reshape(depth, 1, d3))


def _norm_mod(x, g, m):
    h = x * lax.rsqrt(jnp.mean(x * x, axis=-1, keepdims=True) + EPS) * g
    return h * (1.0 + m[1:2]) + m[0:1]


def _s5_disc_kernel(ls_ref, lr_ref, li_ref, br_ref, bi_ref, cr_ref, ci_ref, seg_ref,
                    a2r_ref, a2i_ref, bbr_ref, bbi_ref, abr_ref, abi_ref, car_ref, cai_ref,
                    ca2r_ref, ca2i_ref, k0_ref, k1_ref):
    step = jnp.exp(ls_ref[...])
    lr = lr_ref[...]
    li = li_ref[...]
    mag = jnp.exp(lr * step)
    ar = mag * jnp.cos(li * step)
    ai = mag * jnp.sin(li * step)
    den = lr * lr + li * li
    xr = ar - 1.0
    nr = (xr * lr + ai * li) / den
    ni = (ai * lr - xr * li) / den
    bbr = nr * br_ref[...] - ni * bi_ref[...]
    bbi = nr * bi_ref[...] + ni * br_ref[...]
    abr = ar * bbr - ai * bbi
    abi = ar * bbi + ai * bbr
    a2r = ar * ar - ai * ai
    a2i = 2.0 * (ar * ai)
    cr = cr_ref[...]
    ci = ci_ref[...]
    a2r_ref[...] = a2r
    a2i_ref[...] = a2i
    bbr_ref[...] = bbr
    bbi_ref[...] = bbi
    abr_ref[...] = abr
    abi_ref[...] = abi
    car_ref[...] = cr * ar - ci * ai
    cai_ref[...] = cr * ai + ci * ar
    ca2r_ref[...] = cr * a2r - ci * a2i
    ca2i_ref[...] = cr * a2i + ci * a2r
    nc = br_ref.shape[0]
    prod0 = jnp.concatenate([cr * bbr[c:c + 1] - ci * bbi[c:c + 1] for c in range(nc)], axis=0)
    prod1 = jnp.concatenate([cr * abr[c:c + 1] - ci * abi[c:c + 1] for c in range(nc)], axis=0)
    seg = seg_ref[...]
    k0_ref[...] = jnp.dot(prod0, seg, precision=HI, preferred_element_type=F32)
    k1_ref[...] = jnp.dot(prod1, seg, precision=HI, preferred_element_type=F32)


def _s5_disc(log_step, lam_re, lam_im, b_re, b_im, c_re, c_im):
    G, P = lam_re.shape
    n = G * P
    c = b_re.shape[-1]
    ls = jnp.broadcast_to(log_step[:, None], (G, P)).reshape(1, n)
    brT = b_re.transpose(2, 0, 1).reshape(c, n)
    biT = b_im.transpose(2, 0, 1).reshape(c, n)
    crT = c_re.transpose(1, 0, 2).reshape(c, n)
    ciT = c_im.transpose(1, 0, 2).reshape(c, n)
    seg = jnp.repeat(jnp.eye(G, dtype=F32), P, axis=0)
    vec = jax.ShapeDtypeStruct((1, n), F32)
    mat = jax.ShapeDtypeStruct((c, n), F32)
    kmat = jax.ShapeDtypeStruct((c * c, G), F32)
    return pl.pallas_call(
        _s5_disc_kernel,
        out_shape=[vec, vec] + [mat] * 8 + [kmat, kmat],
        name="s5_disc",
    )(ls, lam_re.reshape(1, n), lam_im.reshape(1, n), brT, biT, crT, ciT, seg)


def _s5_block_weights(bbrT, bbiT, abrT, abiT, carT, caiT, ca2rT, ca2iT, k0, k1):
    nb = S5_BLOCK_GROUPS
    G = k0.shape[-1]
    nblk = G // nb
    eye = jnp.eye(nb, dtype=F32)

    def bdiag_b(bbT):
        t = bbT.reshape(S5_GROUP, nblk, nb, S5_STATE)
        w = jnp.einsum("cjhp,gh->jgchp", t, eye)
        return w.reshape(nblk, nb * S5_GROUP, nb * S5_STATE)

    def bdiag_c(cT):
        t = cT.reshape(S5_GROUP, nblk, nb, S5_STATE)
        w = jnp.einsum("ojgp,gh->jhpgo", t, eye)
        return w.reshape(nblk, nb * S5_STATE, nb * S5_GROUP)

    def bdiag_k(k):
        t = k.reshape(S5_GROUP, S5_GROUP, nblk, nb)
        w = jnp.einsum("iojg,gh->jgiho", t, eye)
        return w.reshape(nblk, nb * S5_GROUP, nb * S5_GROUP)

    wb = jnp.concatenate([
        jnp.concatenate([bdiag_b(abrT), bdiag_b(abiT)], axis=2),
        jnp.concatenate([bdiag_b(bbrT), bdiag_b(bbiT)], axis=2)], axis=1).astype(BF16)
    wc = jnp.concatenate([
        jnp.concatenate([bdiag_c(carT), bdiag_c(ca2rT)], axis=2),
        jnp.concatenate([-bdiag_c(caiT), -bdiag_c(ca2iT)], axis=2)], axis=1).astype(BF16)
    d0, d1 = bdiag_k(k0), bdiag_k(k1)
    wd = jnp.concatenate([
        jnp.concatenate([d0, d1], axis=2),
        jnp.concatenate([jnp.zeros_like(d0), d0], axis=2)], axis=1).astype(BF16)
    return wb, wc, wd


def _s5_layer_kernel(x_ref, mod_ref, g_ref, win_ref, ar_ref, ai_ref, wb_ref, wc_ref, wd_ref,
                     hr0_ref, hi0_ref, d_ref, wg_ref, bglu_ref, wo_ref, o_ref, hr_ref, hi_ref,
                     u_scr, z_scr, y_scr, bu_scr, xr_scr, xi_scr, *, steps, batch):
    i = pl.program_id(0)
    rows = steps * batch
    pairs = steps // 2
    prow = pairs * batch
    d_model = x_ref.shape[-1]
    e = u_scr.shape[-1]
    nblk = wb_ref.shape[0]
    half = wb_ref.shape[2] // 2
    ulanes = wb_ref.shape[1] // 2

    @pl.when(i == 0)
    def _():
        xr_scr[...] = hr0_ref[...]
        xi_scr[...] = hi0_ref[...]

    x3 = jnp.swapaxes(x_ref[...], 0, 1)
    m = mod_ref[...]
    h3 = x3 * lax.rsqrt(jnp.mean(x3 * x3, axis=-1, keepdims=True) + EPS) * g_ref[...]
    h3 = h3 * (1.0 + m[1]) + m[0]
    p = _dot(h3.reshape(rows, d_model), win_ref[...])
    u_scr[...] = p[:, :e].reshape(pairs, 2, batch, e)
    z_scr[...] = p[:, e:]

    for j in range(nblk):
        sl = slice(j * half, (j + 1) * half)
        ch = slice(j * ulanes, (j + 1) * ulanes)
        buf = bu_scr.at[j % 2]
        lhs = jnp.concatenate([u_scr[:, 0, :, ch].reshape(prow, ulanes),
                               u_scr[:, 1, :, ch].reshape(prow, ulanes)], axis=1).astype(BF16)
        buf[batch:batch + prow, :] = jnp.dot(lhs, wb_ref[j], preferred_element_type=F32)
        ar = jnp.broadcast_to(ar_ref[:, sl], (batch, half))
        ai = jnp.broadcast_to(ai_ref[:, sl], (batch, half))
        xr = xr_scr[:, sl]
        xi = xi_scr[:, sl]
        buf[0:batch, 0:half] = xr
        buf[0:batch, half:2 * half] = xi
        for t in range(1, pairs + 1):
            rows_t = slice(t * batch, (t + 1) * batch)
            xr, xi = (ar * xr - ai * xi + buf[rows_t, 0:half],
                      ar * xi + ai * xr + buf[rows_t, half:2 * half])
            buf[rows_t, 0:half] = xr
            buf[rows_t, half:2 * half] = xi
        xr_scr[:, sl] = xr
        xi_scr[:, sl] = xi
        ypair = _dot(buf[0:prow, :], wc_ref[j]) + jnp.dot(lhs, wd_ref[j],
                                                          preferred_element_type=F32)
        y_scr[:, 0, :, ch] = ypair[:, :ulanes].reshape(pairs, batch, ulanes)
        y_scr[:, 1, :, ch] = ypair[:, ulanes:].reshape(pairs, batch, ulanes)

    y = jax.nn.gelu(y_scr[...].reshape(rows, e) + d_ref[...] * u_scr[...].reshape(rows, e))
    y = y * jax.nn.sigmoid(_dot(y, wg_ref[...]) + bglu_ref[...])
    y = y * jax.nn.silu(z_scr[...])
    out3 = _dot(y, wo_ref[...]).reshape(steps, batch, d_model)
    o_ref[...] = jnp.swapaxes(x3 + m[2] * out3, 0, 1)

    @pl.when(i == pl.num_programs(0) - 1)
    def _():
        hr_ref[...] = xr_scr[...]
        hi_ref[...] = xi_scr[...]


def _s5_layer(x, mod, g, w_in, ar, ai, wb, wc, wd, hr0, hi0, d, w_glu, b_glu, w_out):
    B, L, D = x.shape
    E = w_in.shape[1] // 2
    n = ar.shape[1]
    steps = min(64, L)
    assert steps % 2 == 0 and L % steps == 0
    rows = steps * B
    pairs = steps // 2
    const2 = lambda i: (0, 0)
    const3 = lambda i: (0, 0, 0)

    def resident(shape):
        return pl.BlockSpec(shape, const2 if len(shape) == 2 else const3,
                            pipeline_mode=pl.Buffered(1))

    return pl.pallas_call(
        functools.partial(_s5_layer_kernel, steps=steps, batch=B),
        grid=(L // steps,),
        in_specs=[
            pl.BlockSpec((B, steps, D), lambda i: (0, i, 0)),
            resident((3, B, D)),
            resident((1, D)),
            resident((D, 2 * E)),
            resident((1, n)),
            resident((1, n)),
            resident(wb.shape),
            resident(wc.shape),
            resident(wd.shape),
            resident((B, n)),
            resident((B, n)),
            resident((1, E)),
            resident((E, E)),
            resident((1, E)),
            resident((E, D)),
        ],
        out_specs=[
            pl.BlockSpec((B, steps, D), lambda i: (0, i, 0)),
            pl.BlockSpec((B, n), const2),
            pl.BlockSpec((B, n), const2),
        ],
        out_shape=[
            jax.ShapeDtypeStruct((B, L, D), F32),
            jax.ShapeDtypeStruct((B, n), F32),
            jax.ShapeDtypeStruct((B, n), F32),
        ],
        scratch_shapes=[
            pltpu.VMEM((pairs, 2, B, E), F32),
            pltpu.VMEM((rows, E), F32),
            pltpu.VMEM((pairs, 2, B, E), F32),
            pltpu.VMEM((2, (pairs + 1) * B, wb.shape[2]), F32),
            pltpu.VMEM((B, n), F32),
            pltpu.VMEM((B, n), F32),
        ],
        compiler_params=_params("arbitrary"),
        name="s5_layer",
    )(x, mod, g.reshape(1, D), w_in.astype(BF16), ar, ai, wb, wc, wd, hr0, hi0, d.reshape(1, E),
      w_glu.astype(BF16), b_glu.reshape(1, E), w_out.astype(BF16))


def _l2norm(x):
    return x * lax.rsqrt(jnp.sum(x * x, axis=-1, keepdims=True) + EPS)


def _gdn_chunk(L):
    return GDN_CHUNK if L % GDN_CHUNK == 0 else L


def _split3(x):
    hi = x.astype(BF16)
    r = x - hi.astype(F32)
    mid = r.astype(BF16)
    lo = (r - mid.astype(F32)).astype(BF16)
    return hi, mid, lo


def _gdn_in_kernel(x_ref, mod_ref, g_ref, w_ref, alog_ref, dtb_ref, cw_ref,
                   hist_ref, act_ref, z_ref, bgc_ref, hist_out_ref, ext_scr, tri_scr,
                   *, heads, qk_heads, chunk):
    i = pl.program_id(1)
    tm = x_ref.shape[1]
    qk_width = qk_heads * GDN_DK
    conv_ch = act_ref.shape[-1]
    cb = qk_width

    @pl.when(i == 0)
    def _():
        ext_scr[0:HIST_ROWS, :] = hist_ref[0]
        row = lax.broadcasted_iota(jnp.int32, (tm, tm), 0)
        col = lax.broadcasted_iota(jnp.int32, (tm, tm), 1)
        sh = int(math.log2(chunk))
        tri_scr[...] = ((row >= col) & ((row >> sh) == (col >> sh))).astype(BF16)

    h = _norm_mod(x_ref[0], g_ref[...], mod_ref[0]).astype(BF16)
    cw = cw_ref[...]
    assert GDN_CONV == 4 and HIST_ROWS >= GDN_CONV - 1

    nblk = conv_ch // cb
    zb = z_ref.shape[-1] // nblk
    wcols = cb + zb

    def project(nb):
        width = wcols + (LANES if nb == 0 else 0)
        start = nb * wcols + (LANES if nb > 0 else 0)
        r = jnp.dot(h, w_ref[:, start:start + width], preferred_element_type=F32)
        ext_scr[HIST_ROWS:HIST_ROWS + tm, nb * cb:(nb + 1) * cb] = r[:, :cb]
        z_ref[0, :, nb * zb:(nb + 1) * zb] = r[:, cb:wcols].astype(z_ref.dtype)
        return r[:, wcols:]

    def conv_act(nb):
        cols = slice(nb * cb, (nb + 1) * cb)
        e = ext_scr[:, cols]
        e1 = pltpu.roll(e, 1, 0)
        wj = cw[:, cols]
        near = e * wj[3:4] + e1 * wj[2:3]
        far = pltpu.roll(e * wj[1:2] + e1 * wj[0:1], 2, 0)
        conv = (near + far)[HIST_ROWS:]
        ext_scr[0:HIST_ROWS, cols] = ext_scr[tm:tm + HIST_ROWS, cols]
        act = jax.nn.silu(conv)
        for hh in range(cb // GDN_DK):
            a = act[:, hh * GDN_DK:(hh + 1) * GDN_DK]
            if nb == 0:
                a = _l2norm(a) * (GDN_DK ** -0.5)
            elif nb == 1:
                a = _l2norm(a)
            act_ref[0, :, nb * cb + hh * GDN_DK:nb * cb + (hh + 1) * GDN_DK] = a

    for nb in range(nblk):
        tail = project(nb)
        if nb == 0:
            ba = tail
        conv_act(nb)

    beta = jax.nn.sigmoid(ba)
    a = ba + dtb_ref[...]
    softplus = jnp.maximum(a, 0.0) + jnp.log(1.0 + jnp.exp(-jnp.abs(a)))
    lane = lax.broadcasted_iota(jnp.int32, ba.shape, 1)
    live = (lane >= heads) & (lane < 2 * heads)
    g = jnp.where(live, -jnp.exp(alog_ref[...]) * softplus, 0.0)
    g_hi, g_mid, g_lo = _split3(g)
    packed = (g_hi.astype(F32) + pltpu.roll(g_mid.astype(F32), heads, 1)
              + pltpu.roll(g_lo.astype(F32), 2 * heads, 1)).astype(BF16)
    r = jnp.dot(tri_scr[...], packed, preferred_element_type=F32)
    gc = r + pltpu.roll(r, LANES - heads, 1) + pltpu.roll(r, LANES - 2 * heads, 1)
    bgc_ref[0] = jnp.where(lane < heads, beta, gc)

    @pl.when(i == pl.num_programs(1) - 1)
    def _():
        hist_out_ref[0] = ext_scr[0:HIST_ROWS, :]


def _gdn_in(x, mod, g, w_in, a_log, dt_bias, conv_w, hist, v_width):
    B, L, D = x.shape
    H = a_log.shape[0]
    conv_ch = conv_w.shape[-1]
    tm = _row_tile(L)
    o1, o2 = conv_ch, conv_ch + v_width
    wba = jnp.zeros((D, LANES), F32).at[:, :2 * H].set(w_in[:, o2:o2 + 2 * H])
    alog = jnp.zeros((1, LANES), F32).at[0, H:2 * H].set(a_log)
    dtb = jnp.zeros((1, LANES), F32).at[0, H:2 * H].set(dt_bias)
    hist8 = jnp.zeros((B, HIST_ROWS, conv_ch), F32).at[:, HIST_ROWS - (GDN_CONV - 1):, :].set(hist)
    cb = (H // 2) * GDN_DK
    nblk = conv_ch // cb
    zb = v_width // nblk
    pieces = []
    for nb in range(nblk):
        pieces += [w_in[:, nb * cb:(nb + 1) * cb], w_in[:, o1 + nb * zb:o1 + (nb + 1) * zb]]
        if nb == 0:
            pieces.append(wba)
    w_cat = jnp.concatenate(pieces, axis=1)
    const = lambda b, i: (0, 0)

    def resident(shape):
        return pl.BlockSpec(shape, const, pipeline_mode=pl.Buffered(1))

    return pl.pallas_call(
        functools.partial(_gdn_in_kernel, heads=H, qk_heads=H // 2, chunk=_gdn_chunk(L)),
        grid=(B, L // tm),
        in_specs=[
            pl.BlockSpec((1, tm, D), lambda b, i: (b, i, 0)),
            pl.BlockSpec((1, 3, D), lambda b, i: (b, 0, 0)),
            resident((1, D)),
            resident((D, conv_ch + v_width + LANES)),
            resident((1, LANES)),
            resident((1, LANES)),
            resident((GDN_CONV, conv_ch)),
            pl.BlockSpec((1, HIST_ROWS, conv_ch), lambda b, i: (b, 0, 0)),
        ],
        out_specs=[
            pl.BlockSpec((1, tm, conv_ch), lambda b, i: (b, i, 0)),
            pl.BlockSpec((1, tm, v_width), lambda b, i: (b, i, 0)),
            pl.BlockSpec((1, tm, LANES), lambda b, i: (b, i, 0)),
            pl.BlockSpec((1, HIST_ROWS, conv_ch), lambda b, i: (b, 0, 0)),
        ],
        out_shape=[
            jax.ShapeDtypeStruct((B, L, conv_ch), F32),
            jax.ShapeDtypeStruct((B, L, v_width), BF16),
            jax.ShapeDtypeStruct((B, L, LANES), F32),
            jax.ShapeDtypeStruct((B, HIST_ROWS, conv_ch), F32),
        ],
        scratch_shapes=[pltpu.VMEM((HIST_ROWS + tm, conv_ch), F32), pltpu.VMEM((tm, tm), BF16)],
        compiler_params=_params("parallel", "arbitrary"),
        name="gdn_in",
    )(x, mod, g.reshape(1, D), w_cat.astype(BF16), alog, dtb, conv_w, hist8)


def _gdn_core_kernel(act_ref, bgc_ref, s0_ref, o_ref, sout_ref, s_scr, *, T, qk_heads, v_heads):
    c = pl.program_id(1)
    nrows = act_ref.shape[0]
    qk_width = qk_heads * GDN_DK
    rep = v_heads // qk_heads

    @pl.when(c == 0)
    def _():
        s_scr[...] = s0_ref[...]

    row = lax.broadcasted_iota(jnp.int32, (T, T), 0)
    col = lax.broadcasted_iota(jnp.int32, (T, T), 1)
    causal = row >= col
    strict = row > col
    eye = (lax.broadcasted_iota(jnp.int32, (LANES, LANES), 0)
           == lax.broadcasted_iota(jnp.int32, (LANES, LANES), 1)).astype(BF16)
    bgs, gcts = [], []
    for r in range(nrows):
        bg = bgc_ref[r]
        gc_hi, gc_mid, gc_lo = _split3(bg)
        bgs.append(bg)
        gcts.append(lax.dot_general(eye, gc_hi, NT_DIMS, preferred_element_type=F32)
                    + lax.dot_general(eye, gc_mid, NT_DIMS, preferred_element_type=F32)
                    + lax.dot_general(eye, gc_lo, NT_DIMS, preferred_element_type=F32))
    sh = int(math.log2(INV_BLOCK))
    diag_blk = (row >> sh) == (col >> sh)
    merges = []
    while (1 << sh) < T:
        merges.append(((row >> (sh + 1)) == (col >> (sh + 1))) & ((row >> sh) > (col >> sh)))
        sh += 1

    qkp = [(r, hq) for r in range(nrows) for hq in range(qk_heads)]
    units = [(r, h) for r in range(nrows) for h in range(v_heads)]
    n = range(len(units))
    qk_of = [r * qk_heads + h // rep for r, h in units]
    qs = [act_ref[r, :, hq * GDN_DK:(hq + 1) * GDN_DK] for r, hq in qkp]
    ks = [act_ref[r, :, qk_width + hq * GDN_DK:qk_width + (hq + 1) * GDN_DK] for r, hq in qkp]
    kq = [lax.dot_general(jnp.concatenate([ks[i], qs[i]], axis=0).astype(BF16),
                          ks[i].astype(BF16), NT_DIMS, preferred_element_type=F32)
          for i in range(len(qkp))]
    kks = [kq[i][:T] for i in qk_of]
    qks = [kq[i][T:] for i in qk_of]
    qu = [qs[i] for i in qk_of]
    ku = [ks[i] for i in qk_of]
    beta = [bgs[r][:, h:h + 1] for r, h in units]
    gcol = [bgs[r][:, v_heads + h:v_heads + h + 1] for r, h in units]
    grow = [gcts[r][v_heads + h:v_heads + h + 1, :] for r, h in units]
    decay = [jnp.where(causal, jnp.exp(jnp.where(causal, gcol[u] - grow[u], 0.0)), 0.0) for u in n]
    egc = [jnp.exp(gcol[u]) for u in n]
    m = [jnp.where(strict, (beta[u] * kks[u]) * decay[u], 0.0) for u in n]
    p = [jnp.where(diag_blk, -m[u], 0.0) for u in n]
    e = list(p)
    levels = int(math.log2(INV_BLOCK))
    p = [_dot(p[u], p[u]) for u in n]
    for k in range(1, levels):
        if k + 1 < levels:
            pe = [_dot(jnp.concatenate([p[u], e[u]], axis=0), p[u]) for u in n]
            e = [e[u] + p[u] + pe[u][T:] for u in n]
            p = [pe[u][:T] for u in n]
        else:
            e = [e[u] + p[u] + _dot(e[u], p[u]) for u in n]
    for blk in merges:
        m21 = [jnp.where(blk, m[u], 0.0) for u in n]
        y = [m21[u] + _dot(m21[u], e[u]) for u in n]
        e = [e[u] - (y[u] + _dot(e[u], y[u])) for u in n]
    rhs = [jnp.concatenate(
        [act_ref[r, :, 2 * qk_width + h * GDN_DV:2 * qk_width + (h + 1) * GDN_DV] * beta[u],
         ku[u] * (beta[u] * egc[u])], axis=1) for u, (r, h) in enumerate(units)]
    sol = [rhs[u] + _dot(e[u], rhs[u]) for u in n]
    s = [s_scr[r, h] for r, h in units]
    ws = [_dot(jnp.concatenate([sol[u][:, GDN_DV:], qu[u] * egc[u]], axis=0), s[u]) for u in n]
    v_new = [sol[u][:, :GDN_DV] - ws[u][:T] for u in n]
    for u, (r, h) in enumerate(units):
        o_ref[r, :, h * GDN_DV:(h + 1) * GDN_DV] = (
            ws[u][T:] + _dot(qks[u] * decay[u], v_new[u])).astype(o_ref.dtype)
    g_last = [grow[u][:, T - 1:T] for u in n]
    for u, (r, h) in enumerate(units):
        kd = ku[u] * jnp.exp(g_last[u] - gcol[u])
        s_scr[r, h] = s[u] * jnp.exp(g_last[u]) + lax.dot_general(
            kd.astype(BF16), v_new[u].astype(BF16), (((0,), (0,)), ((), ())),
            preferred_element_type=F32)

    @pl.when(c == pl.num_programs(1) - 1)
    def _():
        sout_ref[...] = s_scr[...]


def _gdn_core(act, bgc, s0):
    B, L, C = act.shape
    H = s0.shape[1]
    T = _gdn_chunk(L)
    nr = GDN_ROWS_PER_STEP
    return pl.pallas_call(
        functools.partial(_gdn_core_kernel, T=T, qk_heads=H // 2, v_heads=H),
        grid=(B // nr, L // T),
        in_specs=[
            pl.BlockSpec((nr, T, C), lambda b, c: (b, c, 0)),
            pl.BlockSpec((nr, T, LANES), lambda b, c: (b, c, 0)),
            pl.BlockSpec((nr, H, GDN_DK, GDN_DV), lambda b, c: (b, 0, 0, 0)),
        ],
        out_specs=[
            pl.BlockSpec((nr, T, H * GDN_DV), lambda b, c: (b, c, 0)),
            pl.BlockSpec((nr, H, GDN_DK, GDN_DV), lambda b, c: (b, 0, 0, 0)),
        ],
        out_shape=[
            jax.ShapeDtypeStruct((B, L, H * GDN_DV), BF16),
            jax.ShapeDtypeStruct((B, H, GDN_DK, GDN_DV), F32),
        ],
        scratch_shapes=[pltpu.VMEM((nr, H, GDN_DK, GDN_DV), F32)],
        compiler_params=_params("parallel", "arbitrary"),
        name="gdn_core",
    )(act, bgc, s0)


def _gdn_out_kernel(o_ref, z_ref, x_ref, mod_ref, ng_ref, wo_ref, fg_ref, y_ref, *, heads):
    o = o_ref[0].astype(F32)
    parts = []
    for h in range(heads):
        oh = o[:, h * GDN_DV:(h + 1) * GDN_DV]
        parts.append(oh * lax.rsqrt(jnp.mean(oh * oh, axis=-1, keepdims=True) + EPS))
    on = jnp.concatenate(parts, axis=1) * ng_ref[...]
    out = _dot(on * jax.nn.silu(z_ref[0].astype(F32)), wo_ref[...])
    x2 = x_ref[0] + mod_ref[0][2:3] * out
    y_ref[0] = x2 * lax.rsqrt(jnp.mean(x2 * x2, axis=-1, keepdims=True) + EPS) * fg_ref[...]


def _gdn_out(o, z, x, mod, norm_g, w_out, final_g):
    B, L, D = x.shape
    V = o.shape[-1]
    H = V // GDN_DV
    tm = _row_tile(L, 1024)
    const = lambda b, i: (0, 0)
    return pl.pallas_call(
        functools.partial(_gdn_out_kernel, heads=H),
        grid=(B, L // tm),
        in_specs=[
            pl.BlockSpec((1, tm, V), lambda b, i: (b, i, 0)),
            pl.BlockSpec((1, tm, V), lambda b, i: (b, i, 0)),
            pl.BlockSpec((1, tm, D), lambda b, i: (b, i, 0)),
            pl.BlockSpec((1, 3, D), lambda b, i: (b, 0, 0)),
            pl.BlockSpec((1, V), const),
            pl.BlockSpec((V, D), const),
            pl.BlockSpec((1, D), const),
        ],
        out_specs=pl.BlockSpec((1, tm, D), lambda b, i: (b, i, 0)),
        out_shape=jax.ShapeDtypeStruct((B, L, D), F32),
        compiler_params=_params("parallel", "parallel"),
        name="gdn_out",
    )(o, z, x, mod, jnp.tile(norm_g, H).reshape(1, V), w_out.astype(BF16), final_g.reshape(1, D))


def _trunk(x, mod, s5_re0, s5_im0, gdn_s0, gdn_conv0, w):
    B, L, D = x.shape
    G, P = w["s5_lambda_re"].shape[1:]
    n = G * P

    a2r, a2i, *ops = _s5_disc(w["s5_log_step"][0], w["s5_lambda_re"][0], w["s5_lambda_im"][0],
                              w["s5_b_re"][0], w["s5_b_im"][0], w["s5_c_re"][0], w["s5_c_im"][0])
    wb, wc, wd = _s5_block_weights(*ops)
    x1, hr, hi = _s5_layer(x, mod[0].transpose(1, 0, 2), w["norm_g"][0], w["s5_w_in"][0], a2r, a2i,
                           wb, wc, wd, s5_re0[0].reshape(B, n), s5_im0[0].reshape(B, n),
                           w["s5_d"][0], w["s5_w_glu"][0], w["s5_b_glu"][0], w["s5_w_out"][0])

    v_width = w["gdn_w_out"].shape[1]
    act, z2, bgc, hist8 = _gdn_in(x1, mod[1], w["norm_g"][1], w["gdn_w_in"][0], w["gdn_a_log"][0],
                                  w["gdn_dt_bias"][0], w["gdn_conv_w"][0], gdn_conv0[0], v_width)
    o, s_new = _gdn_core(act, bgc, gdn_s0[0])
    y = _gdn_out(o, z2, x1, mod[1], w["gdn_norm_g"][0], w["gdn_w_out"][0], w["final_g"])
    new_hist = hist8[:, HIST_ROWS - (GDN_CONV - 1):, :]
    return (y, hr.reshape(1, B, G, P), hi.reshape(1, B, G, P), s_new[None], new_hist[None])


def kernel(x_prompt, x_sample, c_prompt, c_sample, state_s5_re, state_s5_im, state_gdn, state_gdn_conv, norm_g, w_ada, b_ada, s5_w_in, s5_log_step, s5_lambda_re, s5_lambda_im, s5_b_re, s5_b_im, s5_c_re, s5_c_im, s5_d, s5_w_glu, s5_b_glu, s5_w_out, gdn_w_in, gdn_conv_w, gdn_a_log, gdn_dt_bias, gdn_norm_g, gdn_w_out, final_g):
    w = dict(norm_g=norm_g, s5_w_in=s5_w_in, s5_log_step=s5_log_step, s5_lambda_re=s5_lambda_re,
             s5_lambda_im=s5_lambda_im, s5_b_re=s5_b_re, s5_b_im=s5_b_im, s5_c_re=s5_c_re,
             s5_c_im=s5_c_im, s5_d=s5_d, s5_w_glu=s5_w_glu, s5_b_glu=s5_b_glu, s5_w_out=s5_w_out,
             gdn_w_in=gdn_w_in, gdn_conv_w=gdn_conv_w, gdn_a_log=gdn_a_log,
             gdn_dt_bias=gdn_dt_bias, gdn_norm_g=gdn_norm_g, gdn_w_out=gdn_w_out, final_g=final_g)
    bp, _, d = x_prompt.shape
    bs = x_sample.shape[0]
    depth = w_ada.shape[0]
    mod = _ada_mod(jnp.concatenate([c_prompt, c_sample], axis=0), w_ada, b_ada)
    mod = mod.reshape(depth, bp + bs, 3, d)

    z_s5 = jnp.zeros((state_s5_re.shape[0], bp) + state_s5_re.shape[2:], F32)
    z_gdn = jnp.zeros((state_gdn.shape[0], bp) + state_gdn.shape[2:], F32)
    z_conv = jnp.zeros((state_gdn_conv.shape[0], bp) + state_gdn_conv.shape[2:], F32)
    yp, s5r_p, s5i_p, gdn_p, conv_p = _trunk(x_prompt, mod[:, :bp], z_s5, z_s5, z_gdn, z_conv, w)
    ys, s5r_s, s5i_s, gdn_s, conv_s = _trunk(x_sample, mod[:, bp:], state_s5_re, state_s5_im,
                                             state_gdn, state_gdn_conv, w)
    return (yp, ys, s5r_p, s5i_p, gdn_p, conv_p, s5r_s, s5i_s, gdn_s, conv_s)
```

```python
import functools
import math

import jax
import jax.numpy as jnp
from jax import lax
from jax.experimental import pallas as pl
from jax.experimental.pallas import tpu as pltpu

F32 = jnp.float32
BF16 = jnp.bfloat16
EPS = 1e-6

S5_GROUP = 16
S5_STATE = 64
S5_BLOCK_GROUPS = 8
GDN_DK = 128
GDN_DV = 128
GDN_CONV = 4
GDN_CHUNK = 64
GDN_ROWS_PER_STEP = 2
INV_BLOCK = 16
HIST_ROWS = 8
LANES = 128
VMEM_LIMIT = 56 * 1024 * 1024
HI = lax.Precision.HIGHEST
NT_DIMS = (((1,), (1,)), ((), ()))


def _params(*sem):
    return pltpu.CompilerParams(dimension_semantics=sem, vmem_limit_bytes=VMEM_LIMIT)


def _dot(a, b):
    return jnp.dot(a.astype(BF16), b.astype(BF16), preferred_element_type=F32)


def _row_tile(L, rows=512):
    return min(rows, L)


def _ada_kernel(c_ref, w_ref, b_ref, o_ref):
    c = c_ref[...]
    o_ref[0] = _dot(jax.nn.silu(c), w_ref[0]) + b_ref[0]


def _ada_mod(c_all, w_ada, b_ada):
    depth, d, d3 = w_ada.shape
    r = c_all.shape[0]
    tn = 768
    return pl.pallas_call(
        _ada_kernel,
        grid=(depth, d3 // tn),
        in_specs=[
            pl.BlockSpec((r, d), lambda i, j: (0, 0)),
            pl.BlockSpec((1, d, tn), lambda i, j: (i, 0, j)),
            pl.BlockSpec((1, 1, tn), lambda i, j: (i, 0, j)),
        ],
        out_specs=pl.BlockSpec((1, r, tn), lambda i, j: (i, 0, j)),
        out_shape=jax.ShapeDtypeStruct((depth, r, d3), F32),
        compiler_params=_params("parallel", "parallel"),
        name="ada_mod",
    )(c_all, w_ada.astype(BF16), b_ada.reshape(depth, 1, d3))


def _norm_mod(x, g, m):
    h = x * lax.rsqrt(jnp.mean(x * x, axis=-1, keepdims=True) + EPS) * g
    return h * (1.0 + m[1:2]) + m[0:1]


def _s5_disc_kernel(ls_ref, lr_ref, li_ref, br_ref, bi_ref, cr_ref, ci_ref, seg_ref,
                    a2r_ref, a2i_ref, bbr_ref, bbi_ref, abr_ref, abi_ref, car_ref, cai_ref,
                    ca2r_ref, ca2i_ref, k0_ref, k1_ref):
    step = jnp.exp(ls_ref[...])
    lr = lr_ref[...]
    li = li_ref[...]
    mag = jnp.exp(lr * step)
    ar = mag * jnp.cos(li * step)
    ai = mag * jnp.sin(li * step)
    den = lr * lr + li * li
    xr = ar - 1.0
    nr = (xr * lr + ai * li) / den
    ni = (ai * lr - xr * li) / den
    bbr = nr * br_ref[...] - ni * bi_ref[...]
    bbi = nr * bi_ref[...] + ni * br_ref[...]
    abr = ar * bbr - ai * bbi
    abi = ar * bbi + ai * bbr
    a2r = ar * ar - ai * ai
    a2i = 2.0 * (ar * ai)
    cr = cr_ref[...]
    ci = ci_ref[...]
    a2r_ref[...] = a2r
    a2i_ref[...] = a2i
    bbr_ref[...] = bbr
    bbi_ref[...] = bbi
    abr_ref[...] = abr
    abi_ref[...] = abi
    car_ref[...] = cr * ar - ci * ai
    cai_ref[...] = cr * ai + ci * ar
    ca2r_ref[...] = cr * a2r - ci * a2i
    ca2i_ref[...] = cr * a2i + ci * a2r
    nc = br_ref.shape[0]
    prod0 = jnp.concatenate([cr * bbr[c:c + 1] - ci * bbi[c:c + 1] for c in range(nc)], axis=0)
    prod1 = jnp.concatenate([cr * abr[c:c + 1] - ci * abi[c:c + 1] for c in range(nc)], axis=0)
    seg = seg_ref[...]
    k0_ref[...] = jnp.dot(prod0, seg, precision=HI, preferred_element_type=F32)
    k1_ref[...] = jnp.dot(prod1, seg, precision=HI, preferred_element_type=F32)


def _s5_disc(log_step, lam_re, lam_im, b_re, b_im, c_re, c_im):
    G, P = lam_re.shape
    n = G * P
    c = b_re.shape[-1]
    ls = jnp.broadcast_to(log_step[:, None], (G, P)).reshape(1, n)
    brT = b_re.transpose(2, 0, 1).reshape(c, n)
    biT = b_im.transpose(2, 0, 1).reshape(c, n)
    crT = c_re.transpose(1, 0, 2).reshape(c, n)
    ciT = c_im.transpose(1, 0, 2).reshape(c, n)
    seg = jnp.repeat(jnp.eye(G, dtype=F32), P, axis=0)
    vec = jax.ShapeDtypeStruct((1, n), F32)
    mat = jax.ShapeDtypeStruct((c, n), F32)
    kmat = jax.ShapeDtypeStruct((c * c, G), F32)
    return pl.pallas_call(
        _s5_disc_kernel,
        out_shape=[vec, vec] + [mat] * 8 + [kmat, kmat],
        name="s5_disc",
    )(ls, lam_re.reshape(1, n), lam_im.reshape(1, n), brT, biT, crT, ciT, seg)


def _s5_block_weights(bbrT, bbiT, abrT, abiT, carT, caiT, ca2rT, ca2iT, k0, k1):
    nb = S5_BLOCK_GROUPS
    G = k0.shape[-1]
    nblk = G // nb
    eye = jnp.eye(nb, dtype=F32)

    def bdiag_b(bbT):
        t = bbT.reshape(S5_GROUP, nblk, nb, S5_STATE)
        w = jnp.einsum("cjhp,gh->jgchp", t, eye)
        return w.reshape(nblk, nb * S5_GROUP, nb * S5_STATE)

    def bdiag_c(cT):
        t = cT.reshape(S5_GROUP, nblk, nb, S5_STATE)
        w = jnp.einsum("ojgp,gh->jhpgo", t, eye)
        return w.reshape(nblk, nb * S5_STATE, nb * S5_GROUP)

    def bdiag_k(k):
        t = k.reshape(S5_GROUP, S5_GROUP, nblk, nb)
        w = jnp.einsum("iojg,gh->jgiho", t, eye)
        return w.reshape(nblk, nb * S5_GROUP, nb * S5_GROUP)

    wb = jnp.concatenate([
        jnp.concatenate([bdiag_b(abrT), bdiag_b(abiT)], axis=2),
        jnp.concatenate([bdiag_b(bbrT), bdiag_b(bbiT)], axis=2)], axis=1).astype(BF16)
    wc = jnp.concatenate([
        jnp.concatenate([bdiag_c(carT), bdiag_c(ca2rT)], axis=2),
        jnp.concatenate([-bdiag_c(caiT), -bdiag_c(ca2iT)], axis=2)], axis=1).astype(BF16)
    d0, d1 = bdiag_k(k0), bdiag_k(k1)
    wd = jnp.concatenate([
        jnp.concatenate([d0, d1], axis=2),
        jnp.concatenate([jnp.zeros_like(d0), d0], axis=2)], axis=1).astype(BF16)
    return wb, wc, wd


def _s5_layer_kernel(x_ref, mod_ref, g_ref, win_ref, ar_ref, ai_ref, wb_ref, wc_ref, wd_ref,
                     hr0_ref, hi0_ref, d_ref, wg_ref, bglu_ref, wo_ref, o_ref, hr_ref, hi_ref,
                     u_scr, z_scr, y_scr, bu_scr, xr_scr, xi_scr, *, steps, batch):
    i = pl.program_id(0)
    rows = steps * batch
    pairs = steps // 2
    prow = pairs * batch
    d_model = x_ref.shape[-1]
    e = u_scr.shape[-1]
    nblk = wb_ref.shape[0]
    half = wb_ref.shape[2] // 2
    ulanes = wb_ref.shape[1] // 2

    @pl.when(i == 0)
    def _():
        xr_scr[...] = hr0_ref[...]
        xi_scr[...] = hi0_ref[...]

    x3 = jnp.swapaxes(x_ref[...], 0, 1)
    m = mod_ref[...]
    h3 = x3 * lax.rsqrt(jnp.mean(x3 * x3, axis=-1, keepdims=True) + EPS) * g_ref[...]
    h3 = h3 * (1.0 + m[1]) + m[0]
    p = _dot(h3.reshape(rows, d_model), win_ref[...])
    u_scr[...] = p[:, :e].reshape(pairs, 2, batch, e)
    z_scr[...] = p[:, e:]

    for j in range(nblk):
        sl = slice(j * half, (j + 1) * half)
        ch = slice(j * ulanes, (j + 1) * ulanes)
        buf = bu_scr.at[j % 2]
        lhs = jnp.concatenate([u_scr[:, 0, :, ch].reshape(prow, ulanes),
                               u_scr[:, 1, :, ch].reshape(prow, ulanes)], axis=1).astype(BF16)
        buf[batch:batch + prow, :] = jnp.dot(lhs, wb_ref[j], preferred_element_type=F32)
        ar = jnp.broadcast_to(ar_ref[:, sl], (batch, half))
        ai = jnp.broadcast_to(ai_ref[:, sl], (batch, half))
        xr = xr_scr[:, sl]
        xi = xi_scr[:, sl]
        buf[0:batch, 0:half] = xr
        buf[0:batch, half:2 * half] = xi
        for t in range(1, pairs + 1):
            rows_t = slice(t * batch, (t + 1) * batch)
            xr, xi = (ar * xr - ai * xi + buf[rows_t, 0:half],
                      ar * xi + ai * xr + buf[rows_t, half:2 * half])
            buf[rows_t, 0:half] = xr
            buf[rows_t, half:2 * half] = xi
        xr_scr[:, sl] = xr
        xi_scr[:, sl] = xi
        ypair = _dot(buf[0:prow, :], wc_ref[j]) + jnp.dot(lhs, wd_ref[j],
                                                          preferred_element_type=F32)
        y_scr[:, 0, :, ch] = ypair[:, :ulanes].reshape(pairs, batch, ulanes)
        y_scr[:, 1, :, ch] = ypair[:, ulanes:].reshape(pairs, batch, ulanes)

    y = jax.nn.gelu(y_scr[...].reshape(rows, e) + d_ref[...] * u_scr[...].reshape(rows, e))
    y = y * jax.nn.sigmoid(_dot(y, wg_ref[...]) + bglu_ref[...])
    y = y * jax.nn.silu(z_scr[...])
    out3 = _dot(y, wo_ref[...]).reshape(steps, batch, d_model)
    o_ref[...] = jnp.swapaxes(x3 + m[2] * out3, 0, 1)

    @pl.when(i == pl.num_programs(0) - 1)
    def _():
        hr_ref[...] = xr_scr[...]
        hi_ref[...] = xi_scr[...]


def _s5_layer(x, mod, g, w_in, ar, ai, wb, wc, wd, hr0, hi0, d, w_glu, b_glu, w_out):
    B, L, D = x.shape
    E = w_in.shape[1] // 2
    n = ar.shape[1]
    steps = min(64, L)
    assert steps % 2 == 0 and L % steps == 0
    rows = steps * B
    pairs = steps // 2
    const2 = lambda i: (0, 0)
    const3 = lambda i: (0, 0, 0)

    def resident(shape):
        return pl.BlockSpec(shape, const2 if len(shape) == 2 else const3,
                            pipeline_mode=pl.Buffered(1))

    return pl.pallas_call(
        functools.partial(_s5_layer_kernel, steps=steps, batch=B),
        grid=(L // steps,),
        in_specs=[
            pl.BlockSpec((B, steps, D), lambda i: (0, i, 0)),
            resident((3, B, D)),
            resident((1, D)),
            resident((D, 2 * E)),
            resident((1, n)),
            resident((1, n)),
            resident(wb.shape),
            resident(wc.shape),
            resident(wd.shape),
            resident((B, n)),
            resident((B, n)),
            resident((1, E)),
            resident((E, E)),
            resident((1, E)),
            resident((E, D)),
        ],
        out_specs=[
            pl.BlockSpec((B, steps, D), lambda i: (0, i, 0)),
            pl.BlockSpec((B, n), const2),
            pl.BlockSpec((B, n), const2),
        ],
        out_shape=[
            jax.ShapeDtypeStruct((B, L, D), F32),
            jax.ShapeDtypeStruct((B, n), F32),
            jax.ShapeDtypeStruct((B, n), F32),
        ],
        scratch_shapes=[
            pltpu.VMEM((pairs, 2, B, E), F32),
            pltpu.VMEM((rows, E), F32),
            pltpu.VMEM((pairs, 2, B, E), F32),
            pltpu.VMEM((2, (pairs + 1) * B, wb.shape[2]), F32),
            pltpu.VMEM((B, n), F32),
            pltpu.VMEM((B, n), F32),
        ],
        compiler_params=_params("arbitrary"),
        name="s5_layer",
    )(x, mod, g.reshape(1, D), w_in.astype(BF16), ar, ai, wb, wc, wd, hr0, hi0, d.reshape(1, E),
      w_glu.astype(BF16), b_glu.reshape(1, E), w_out.astype(BF16))


def _l2norm(x):
    return x * lax.rsqrt(jnp.sum(x * x, axis=-1, keepdims=True) + EPS)


def _gdn_chunk(L):
    return GDN_CHUNK if L % GDN_CHUNK == 0 else L


def _split3(x):
    hi = x.astype(BF16)
    r = x - hi.astype(F32)
    mid = r.astype(BF16)
    lo = (r - mid.astype(F32)).astype(BF16)
    return hi, mid, lo


def _gdn_in_kernel(x_ref, mod_ref, g_ref, w_ref, alog_ref, dtb_ref, cw_ref,
                   hist_ref, act_ref, z_ref, bgc_ref, hist_out_ref, ext_scr, tri_scr,
                   *, heads, qk_heads, chunk):
    i = pl.program_id(1)
    tm = x_ref.shape[1]
    qk_width = qk_heads * GDN_DK
    conv_ch = act_ref.shape[-1]
    cb = qk_width

    @pl.when(i == 0)
    def _():
        ext_scr[0:HIST_ROWS, :] = hist_ref[0]
        row = lax.broadcasted_iota(jnp.int32, (tm, tm), 0)
        col = lax.broadcasted_iota(jnp.int32, (tm, tm), 1)
        sh = int(math.log2(chunk))
        tri_scr[...] = ((row >= col) & ((row >> sh) == (col >> sh))).astype(BF16)

    h = _norm_mod(x_ref[0], g_ref[...], mod_ref[0]).astype(BF16)
    cw = cw_ref[...]
    assert GDN_CONV == 4 and HIST_ROWS >= GDN_CONV - 1

    nblk = conv_ch // cb
    zb = z_ref.shape[-1] // nblk
    wcols = cb + zb

    def project(nb):
        width = wcols + (LANES if nb == 0 else 0)
        start = nb * wcols + (LANES if nb > 0 else 0)
        r = jnp.dot(h, w_ref[:, start:start + width], preferred_element_type=F32)
        ext_scr[HIST_ROWS:HIST_ROWS + tm, nb * cb:(nb + 1) * cb] = r[:, :cb]
        z_ref[0, :, nb * zb:(nb + 1) * zb] = r[:, cb:wcols].astype(z_ref.dtype)
        return r[:, wcols:]

    def conv_act(nb):
        cols = slice(nb * cb, (nb + 1) * cb)
        e = ext_scr[:, cols]
        e1 = pltpu.roll(e, 1, 0)
        wj = cw[:, cols]
        near = e * wj[3:4] + e1 * wj[2:3]
        far = pltpu.roll(e * wj[1:2] + e1 * wj[0:1], 2, 0)
        conv = (near + far)[HIST_ROWS:]
        ext_scr[0:HIST_ROWS, cols] = ext_scr[tm:tm + HIST_ROWS, cols]
        act = jax.nn.silu(conv)
        for hh in range(cb // GDN_DK):
            a = act[:, hh * GDN_DK:(hh + 1) * GDN_DK]
            if nb == 0:
                a = _l2norm(a) * (GDN_DK ** -0.5)
            elif nb == 1:
                a = _l2norm(a)
            act_ref[0, :, nb * cb + hh * GDN_DK:nb * cb + (hh + 1) * GDN_DK] = a

    for nb in range(nblk):
        tail = project(nb)
        if nb == 0:
            ba = tail
        conv_act(nb)

    beta = jax.nn.sigmoid(ba)
    a = ba + dtb_ref[...]
    softplus = jnp.maximum(a, 0.0) + jnp.log(1.0 + jnp.exp(-jnp.abs(a)))
    lane = lax.broadcasted_iota(jnp.int32, ba.shape, 1)
    live = (lane >= heads) & (lane < 2 * heads)
    g = jnp.where(live, -jnp.exp(alog_ref[...]) * softplus, 0.0)
    g_hi, g_mid, g_lo = _split3(g)
    packed = (g_hi.astype(F32) + pltpu.roll(g_mid.astype(F32), heads, 1)
              + pltpu.roll(g_lo.astype(F32), 2 * heads, 1)).astype(BF16)
    r = jnp.dot(tri_scr[...], packed, preferred_element_type=F32)
    gc = r + pltpu.roll(r, LANES - heads, 1) + pltpu.roll(r, LANES - 2 * heads, 1)
    bgc_ref[0] = jnp.where(lane < heads, beta, gc)

    @pl.when(i == pl.num_programs(1) - 1)
    def _():
        hist_out_ref[0] = ext_scr[0:HIST_ROWS, :]


def _gdn_in(x, mod, g, w_in, a_log, dt_bias, conv_w, hist, v_width):
    B, L, D = x.shape
    H = a_log.shape[0]
    conv_ch = conv_w.shape[-1]
    tm = _row_tile(L)
    o1, o2 = conv_ch, conv_ch + v_width
    wba = jnp.zeros((D, LANES), F32).at[:, :2 * H].set(w_in[:, o2:o2 + 2 * H])
    alog = jnp.zeros((1, LANES), F32).at[0, H:2 * H].set(a_log)
    dtb = jnp.zeros((1, LANES), F32).at[0, H:2 * H].set(dt_bias)
    hist8 = jnp.zeros((B, HIST_ROWS, conv_ch), F32).at[:, HIST_ROWS - (GDN_CONV - 1):, :].set(hist)
    cb = (H // 2) * GDN_DK
    nblk = conv_ch // cb
    zb = v_width // nblk
    pieces = []
    for nb in range(nblk):
        pieces += [w_in[:, nb * cb:(nb + 1) * cb], w_in[:, o1 + nb * zb:o1 + (nb + 1) * zb]]
        if nb == 0:
            pieces.append(wba)
    w_cat = jnp.concatenate(pieces, axis=1)
    const = lambda b, i: (0, 0)

    def resident(shape):
        return pl.BlockSpec(shape, const, pipeline_mode=pl.Buffered(1))

    return pl.pallas_call(
        functools.partial(_gdn_in_kernel, heads=H, qk_heads=H // 2, chunk=_gdn_chunk(L)),
        grid=(B, L // tm),
        in_specs=[
            pl.BlockSpec((1, tm, D), lambda b, i: (b, i, 0)),
            pl.BlockSpec((1, 3, D), lambda b, i: (b, 0, 0)),
            resident((1, D)),
            resident((D, conv_ch + v_width + LANES)),
            resident((1, LANES)),
            resident((1, LANES)),
            resident((GDN_CONV, conv_ch)),
            pl.BlockSpec((1, HIST_ROWS, conv_ch), lambda b, i: (b, 0, 0)),
        ],
        out_specs=[
            pl.BlockSpec((1, tm, conv_ch), lambda b, i: (b, i, 0)),
            pl.BlockSpec((1, tm, v_width), lambda b, i: (b, i, 0)),
            pl.BlockSpec((1, tm, LANES), lambda b, i: (b, i, 0)),
            pl.BlockSpec((1, HIST_ROWS, conv_ch), lambda b, i: (b, 0, 0)),
        ],
        out_shape=[
            jax.ShapeDtypeStruct((B, L, conv_ch), F32),
            jax.ShapeDtypeStruct((B, L, v_width), BF16),
            jax.ShapeDtypeStruct((B, L, LANES), F32),
            jax.ShapeDtypeStruct((B, HIST_ROWS, conv_ch), F32),
        ],
        scratch_shapes=[pltpu.VMEM((HIST_ROWS + tm, conv_ch), F32), pltpu.VMEM((tm, tm), BF16)],
        compiler_params=_params("parallel", "arbitrary"),
        name="gdn_in",
    )(x, mod, g.reshape(1, D), w_cat.astype(BF16), alog, dtb, conv_w, hist8)


def _gdn_core_kernel(act_ref, bgc_ref, s0_ref, o_ref, sout_ref, s_scr, *, T, qk_heads, v_heads):
    c = pl.program_id(1)
    nrows = act_ref.shape[0]
    qk_width = qk_heads * GDN_DK
    rep = v_heads // qk_heads

    @pl.when(c == 0)
    def _():
        s_scr[...] = s0_ref[...]

    row = lax.broadcasted_iota(jnp.int32, (T, T), 0)
    col = lax.broadcasted_iota(jnp.int32, (T, T), 1)
    causal = row >= col
    strict = row > col
    eye = (lax.broadcasted_iota(jnp.int32, (LANES, LANES), 0)
           == lax.broadcasted_iota(jnp.int32, (LANES, LANES), 1)).astype(BF16)
    bgs, gcts = [], []
    for r in range(nrows):
        bg = bgc_ref[r]
        gc_hi, gc_mid, gc_lo = _split3(bg)
        bgs.append(bg)
        gcts.append(lax.dot_general(eye, gc_hi, NT_DIMS, preferred_element_type=F32)
                    + lax.dot_general(eye, gc_mid, NT_DIMS, preferred_element_type=F32)
                    + lax.dot_general(eye, gc_lo, NT_DIMS, preferred_element_type=F32))
    sh = int(math.log2(INV_BLOCK))
    diag_blk = (row >> sh) == (col >> sh)
    merges = []
    while (1 << sh) < T:
        merges.append(((row >> (sh + 1)) == (col >> (sh + 1))) & ((row >> sh) > (col >> sh)))
        sh += 1

    qkp = [(r, hq) for r in range(nrows) for hq in range(qk_heads)]
    units = [(r, h) for r in range(nrows) for h in range(v_heads)]
    n = range(len(units))
    qk_of = [r * qk_heads + h // rep for r, h in units]
    qs = [act_ref[r, :, hq * GDN_DK:(hq + 1) * GDN_DK] for r, hq in qkp]
    ks = [act_ref[r, :, qk_width + hq * GDN_DK:qk_width + (hq + 1) * GDN_DK] for r, hq in qkp]
    kq = [lax.dot_general(jnp.concatenate([ks[i], qs[i]], axis=0).astype(BF16),
                          ks[i].astype(BF16), NT_DIMS, preferred_element_type=F32)
          for i in range(len(qkp))]
    kks = [kq[i][:T] for i in qk_of]
    qks = [kq[i][T:] for i in qk_of]
    qu = [qs[i] for i in qk_of]
    ku = [ks[i] for i in qk_of]
    beta = [bgs[r][:, h:h + 1] for r, h in units]
    gcol = [bgs[r][:, v_heads + h:v_heads + h + 1] for r, h in units]
    grow = [gcts[r][v_heads + h:v_heads + h + 1, :] for r, h in units]
    decay = [jnp.where(causal, jnp.exp(jnp.where(causal, gcol[u] - grow[u], 0.0)), 0.0) for u in n]
    egc = [jnp.exp(gcol[u]) for u in n]
    m = [jnp.where(strict, (beta[u] * kks[u]) * decay[u], 0.0) for u in n]
    p = [jnp.where(diag_blk, -m[u], 0.0) for u in n]
    e = list(p)
    levels = int(math.log2(INV_BLOCK))
    p = [_dot(p[u], p[u]) for u in n]
    for k in range(1, levels):
        if k + 1 < levels:
            pe = [_dot(jnp.concatenate([p[u], e[u]], axis=0), p[u]) for u in n]
            e = [e[u] + p[u] + pe[u][T:] for u in n]
            p = [pe[u][:T] for u in n]
        else:
            e = [e[u] + p[u] + _dot(e[u], p[u]) for u in n]
    for blk in merges:
        m21 = [jnp.where(blk, m[u], 0.0) for u in n]
        y = [m21[u] + _dot(m21[u], e[u]) for u in n]
        e = [e[u] - (y[u] + _dot(e[u], y[u])) for u in n]
    s = [s_scr[r, h] for r, h in units]
    ps = [_dot(jnp.concatenate([ku[u] * (beta[u] * egc[u]), qu[u] * egc[u]], axis=0), s[u])
          for u in n]
    d = [act_ref[r, :, 2 * qk_width + h * GDN_DV:2 * qk_width + (h + 1) * GDN_DV] * beta[u]
         - ps[u][:T] for u, (r, h) in enumerate(units)]
    v_new = [d[u] + _dot(e[u], d[u]) for u in n]
    for u, (r, h) in enumerate(units):
        o_ref[r, :, h * GDN_DV:(h + 1) * GDN_DV] = (
            ps[u][T:] + _dot(qks[u] * decay[u], v_new[u])).astype(o_ref.dtype)
    g_last = [grow[u][:, T - 1:T] for u in n]
    for u, (r, h) in enumerate(units):
        kd = ku[u] * jnp.exp(g_last[u] - gcol[u])
        s_scr[r, h] = s[u] * jnp.exp(g_last[u]) + lax.dot_general(
            kd.astype(BF16), v_new[u].astype(BF16), (((0,), (0,)), ((), ())),
            preferred_element_type=F32)

    @pl.when(c == pl.num_programs(1) - 1)
    def _():
        sout_ref[...] = s_scr[...]


def _gdn_core(act, bgc, s0):
    B, L, C = act.shape
    H = s0.shape[1]
    T = _gdn_chunk(L)
    nr = GDN_ROWS_PER_STEP
    return pl.pallas_call(
        functools.partial(_gdn_core_kernel, T=T, qk_heads=H // 2, v_heads=H),
        grid=(B // nr, L // T),
        in_specs=[
            pl.BlockSpec((nr, T, C), lambda b, c: (b, c, 0)),
            pl.BlockSpec((nr, T, LANES), lambda b, c: (b, c, 0)),
            pl.BlockSpec((nr, H, GDN_DK, GDN_DV), lambda b, c: (b, 0, 0, 0)),
        ],
        out_specs=[
            pl.BlockSpec((nr, T, H * GDN_DV), lambda b, c: (b, c, 0)),
            pl.BlockSpec((nr, H, GDN_DK, GDN_DV), lambda b, c: (b, 0, 0, 0)),
        ],
        out_shape=[
            jax.ShapeDtypeStruct((B, L, H * GDN_DV), BF16),
            jax.ShapeDtypeStruct((B, H, GDN_DK, GDN_DV), F32),
        ],
        scratch_shapes=[pltpu.VMEM((nr, H, GDN_DK, GDN_DV), F32)],
        compiler_params=_params("parallel", "arbitrary"),
        name="gdn_core",
    )(act, bgc, s0)


def _gdn_out_kernel(o_ref, z_ref, x_ref, mod_ref, ng_ref, wo_ref, fg_ref, y_ref, *, heads):
    o = o_ref[0].astype(F32)
    parts = []
    for h in range(heads):
        oh = o[:, h * GDN_DV:(h + 1) * GDN_DV]
        parts.append(oh * lax.rsqrt(jnp.mean(oh * oh, axis=-1, keepdims=True) + EPS))
    on = jnp.concatenate(parts, axis=1) * ng_ref[...]
    out = _dot(on * jax.nn.silu(z_ref[0].astype(F32)), wo_ref[...])
    x2 = x_ref[0] + mod_ref[0][2:3] * out
    y_ref[0] = x2 * lax.rsqrt(jnp.mean(x2 * x2, axis=-1, keepdims=True) + EPS) * fg_ref[...]


def _gdn_out(o, z, x, mod, norm_g, w_out, final_g):
    B, L, D = x.shape
    V = o.shape[-1]
    H = V // GDN_DV
    tm = _row_tile(L, 1024)
    const = lambda b, i: (0, 0)
    return pl.pallas_call(
        functools.partial(_gdn_out_kernel, heads=H),
        grid=(B, L // tm),
        in_specs=[
            pl.BlockSpec((1, tm, V), lambda b, i: (b, i, 0)),
            pl.BlockSpec((1, tm, V), lambda b, i: (b, i, 0)),
            pl.BlockSpec((1, tm, D), lambda b, i: (b, i, 0)),
            pl.BlockSpec((1, 3, D), lambda b, i: (b, 0, 0)),
            pl.BlockSpec((1, V), const),
            pl.BlockSpec((V, D), const),
            pl.BlockSpec((1, D), const),
        ],
        out_specs=pl.BlockSpec((1, tm, D), lambda b, i: (b, i, 0)),
        out_shape=jax.ShapeDtypeStruct((B, L, D), F32),
        compiler_params=_params("parallel", "parallel"),
        name="gdn_out",
    )(o, z, x, mod, jnp.tile(norm_g, H).reshape(1, V), w_out.astype(BF16), final_g.reshape(1, D))


def _trunk(x, mod, s5_re0, s5_im0, gdn_s0, gdn_conv0, w):
    B, L, D = x.shape
    G, P = w["s5_lambda_re"].shape[1:]
    n = G * P

    a2r, a2i, *ops = _s5_disc(w["s5_log_step"][0], w["s5_lambda_re"][0], w["s5_lambda_im"][0],
                              w["s5_b_re"][0], w["s5_b_im"][0], w["s5_c_re"][0], w["s5_c_im"][0])
    wb, wc, wd = _s5_block_weights(*ops)
    x1, hr, hi = _s5_layer(x, mod[0].transpose(1, 0, 2), w["norm_g"][0], w["s5_w_in"][0], a2r, a2i,
                           wb, wc, wd, s5_re0[0].reshape(B, n), s5_im0[0].reshape(B, n),
                           w["s5_d"][0], w["s5_w_glu"][0], w["s5_b_glu"][0], w["s5_w_out"][0])

    v_width = w["gdn_w_out"].shape[1]
    act, z2, bgc, hist8 = _gdn_in(x1, mod[1], w["norm_g"][1], w["gdn_w_in"][0], w["gdn_a_log"][0],
                                  w["gdn_dt_bias"][0], w["gdn_conv_w"][0], gdn_conv0[0], v_width)
    o, s_new = _gdn_core(act, bgc, gdn_s0[0])
    y = _gdn_out(o, z2, x1, mod[1], w["gdn_norm_g"][0], w["gdn_w_out"][0], w["final_g"])
    new_hist = hist8[:, HIST_ROWS - (GDN_CONV - 1):, :]
    return (y, hr.reshape(1, B, G, P), hi.reshape(1, B, G, P), s_new[None], new_hist[None])


def kernel(x_prompt, x_sample, c_prompt, c_sample, state_s5_re, state_s5_im, state_gdn, state_gdn_conv, norm_g, w_ada, b_ada, s5_w_in, s5_log_step, s5_lambda_re, s5_lambda_im, s5_b_re, s5_b_im, s5_c_re, s5_c_im, s5_d, s5_w_glu, s5_b_glu, s5_w_out, gdn_w_in, gdn_conv_w, gdn_a_log, gdn_dt_bias, gdn_norm_g, gdn_w_out, final_g):
    w = dict(norm_g=norm_g, s5_w_in=s5_w_in, s5_log_step=s5_log_step, s5_lambda_re=s5_lambda_re,
             s5_lambda_im=s5_lambda_im, s5_b_re=s5_b_re, s5_b_im=s5_b_im, s5_c_re=s5_c_re,
             s5_c_im=s5_c_im, s5_d=s5_d, s5_w_glu=s5_w_glu, s5_b_glu=s5_b_glu, s5_w_out=s5_w_out,
             gdn_w_in=gdn_w_in, gdn_conv_w=gdn_conv_w, gdn_a_log=gdn_a_log,
             gdn_dt_bias=gdn_dt_bias, gdn_norm_g=gdn_norm_g, gdn_w_out=gdn_w_out, final_g=final_g)
    bp, _, d = x_prompt.shape
    bs = x_sample.shape[0]
    depth = w_ada.shape[0]
    mod = _ada_mod(jnp.concatenate([c_prompt, c_sample], axis=0), w_ada, b_ada)
    mod = mod.reshape(depth, bp + bs, 3, d)

    z_s5 = jnp.zeros((state_s5_re.shape[0], bp) + state_s5_re.shape[2:], F32)
    z_gdn = jnp.zeros((state_gdn.shape[0], bp) + state_gdn.shape[2:], F32)
    z_conv = jnp.zeros((state_gdn_conv.shape[0], bp) + state_gdn_conv.shape[2:], F32)
    yp, s5r_p, s5i_p, gdn_p, conv_p = _trunk(x_prompt, mod[:, :bp], z_s5, z_s5, z_gdn, z_conv, w)
    ys, s5r_s, s5i_s, gdn_s, conv_s = _trunk(x_sample, mod[:, bp:], state_s5_re, state_s5_im,
                                             state_gdn, state_gdn_conv, w)
    return (yp, ys, s5r_p, s5i_p, gdn_p, conv_p, s5r_s, s5i_s, gdn_s, conv_s)
```

```python
import functools
import math

import jax
import jax.numpy as jnp
from jax import lax
from jax.experimental import pallas as pl
from jax.experimental.pallas import tpu as pltpu

F32 = jnp.float32
BF16 = jnp.bfloat16
EPS = 1e-6

S5_GROUP = 16
S5_STATE = 64
S5_BLOCK_GROUPS = 8
GDN_DK = 128
GDN_DV = 128
GDN_CONV = 4
GDN_CHUNK = 64
GDN_ROWS_PER_STEP = 2
INV_BLOCK = 16
LANES = 128
SUBLANES = 8
HIST_ROWS = SUBLANES
V7X_VMEM_BYTES = 64 * 1024 * 1024
VMEM_LIMIT = V7X_VMEM_BYTES * 7 // 8
ROW_TILE = 512
STREAM_ROW_TILE = 1024
S5_TILE_STEPS = 64
ADA_COL_TILE = 768
HI = lax.Precision.HIGHEST
NT_DIMS = (((1,), (1,)), ((), ()))


def _params(*sem):
    return pltpu.CompilerParams(dimension_semantics=sem, vmem_limit_bytes=VMEM_LIMIT)


def _dot(a, b):
    return jnp.dot(a.astype(BF16), b.astype(BF16), preferred_element_type=F32)


def _row_tile(L, rows=ROW_TILE):
    return min(rows, L)


def _ada_kernel(c_ref, w_ref, b_ref, o_ref):
    c = c_ref[...]
    o_ref[0] = _dot(jax.nn.silu(c), w_ref[0]) + b_ref[0]


def _ada_mod(c_all, w_ada, b_ada):
    depth, d, d3 = w_ada.shape
    r = c_all.shape[0]
    tn = ADA_COL_TILE
    return pl.pallas_call(
        _ada_kernel,
        grid=(depth, d3 // tn),
        in_specs=[
            pl.BlockSpec((r, d), lambda i, j: (0, 0)),
            pl.BlockSpec((1, d, tn), lambda i, j: (i, 0, j)),
            pl.BlockSpec((1, 1, tn), lambda i, j: (i, 0, j)),
        ],
        out_specs=pl.BlockSpec((1, r, tn), lambda i, j: (i, 0, j)),
        out_shape=jax.ShapeDtypeStruct((depth, r, d3), F32),
        compiler_params=_params("parallel", "parallel"),
        name="ada_mod",
    )(c_all, w_ada, b_ada.reshape(depth, 1, d3))


def _norm_mod(x, g, m):
    h = x * lax.rsqrt(jnp.mean(x * x, axis=-1, keepdims=True) + EPS) * g
    return h * (1.0 + m[1:2]) + m[0:1]


def _s5_disc_kernel(ls_ref, lr_ref, li_ref, br_ref, bi_ref, cr_ref, ci_ref, seg_ref,
                    a2r_ref, a2i_ref, bbr_ref, bbi_ref, abr_ref, abi_ref, car_ref, cai_ref,
                    ca2r_ref, ca2i_ref, k0_ref, k1_ref):
    step = jnp.exp(ls_ref[...])
    lr = lr_ref[...]
    li = li_ref[...]
    mag = jnp.exp(lr * step)
    ar = mag * jnp.cos(li * step)
    ai = mag * jnp.sin(li * step)
    den = lr * lr + li * li
    xr = ar - 1.0
    nr = (xr * lr + ai * li) / den
    ni = (ai * lr - xr * li) / den
    bbr = nr * br_ref[...] - ni * bi_ref[...]
    bbi = nr * bi_ref[...] + ni * br_ref[...]
    abr = ar * bbr - ai * bbi
    abi = ar * bbi + ai * bbr
    a2r = ar * ar - ai * ai
    a2i = 2.0 * (ar * ai)
    cr = cr_ref[...]
    ci = ci_ref[...]
    a2r_ref[...] = a2r
    a2i_ref[...] = a2i
    bbr_ref[...] = bbr
    bbi_ref[...] = bbi
    abr_ref[...] = abr
    abi_ref[...] = abi
    car_ref[...] = cr * ar - ci * ai
    cai_ref[...] = cr * ai + ci * ar
    ca2r_ref[...] = cr * a2r - ci * a2i
    ca2i_ref[...] = cr * a2i + ci * a2r
    nc = br_ref.shape[0]
    prod0 = jnp.concatenate([cr * bbr[c:c + 1] - ci * bbi[c:c + 1] for c in range(nc)], axis=0)
    prod1 = jnp.concatenate([cr * abr[c:c + 1] - ci * abi[c:c + 1] for c in range(nc)], axis=0)
    seg = seg_ref[...]
    k0_ref[...] = jnp.dot(prod0, seg, precision=HI, preferred_element_type=F32)
    k1_ref[...] = jnp.dot(prod1, seg, precision=HI, preferred_element_type=F32)


def _s5_disc(log_step, lam_re, lam_im, b_re, b_im, c_re, c_im):
    G, P = lam_re.shape
    n = G * P
    c = b_re.shape[-1]
    ls = jnp.broadcast_to(log_step[:, None], (G, P)).reshape(1, n)
    brT = b_re.transpose(2, 0, 1).reshape(c, n)
    biT = b_im.transpose(2, 0, 1).reshape(c, n)
    crT = c_re.transpose(1, 0, 2).reshape(c, n)
    ciT = c_im.transpose(1, 0, 2).reshape(c, n)
    seg = jnp.repeat(jnp.eye(G, dtype=F32), P, axis=0)
    vec = jax.ShapeDtypeStruct((1, n), F32)
    mat = jax.ShapeDtypeStruct((c, n), F32)
    kmat = jax.ShapeDtypeStruct((c * c, G), F32)
    return pl.pallas_call(
        _s5_disc_kernel,
        out_shape=[vec, vec] + [mat] * 8 + [kmat, kmat],
        name="s5_disc",
    )(ls, lam_re.reshape(1, n), lam_im.reshape(1, n), brT, biT, crT, ciT, seg)


def _s5_block_weights(bbrT, bbiT, abrT, abiT, carT, caiT, ca2rT, ca2iT, k0, k1):
    nb = S5_BLOCK_GROUPS
    G = k0.shape[-1]
    nblk = G // nb
    eye = jnp.eye(nb, dtype=F32)

    def bdiag_b(bbT):
        t = bbT.reshape(S5_GROUP, nblk, nb, S5_STATE)
        w = jnp.einsum("cjhp,gh->jgchp", t, eye)
        return w.reshape(nblk, nb * S5_GROUP, nb * S5_STATE)

    def bdiag_c(cT):
        t = cT.reshape(S5_GROUP, nblk, nb, S5_STATE)
        w = jnp.einsum("ojgp,gh->jhpgo", t, eye)
        return w.reshape(nblk, nb * S5_STATE, nb * S5_GROUP)

    def bdiag_k(k):
        t = k.reshape(S5_GROUP, S5_GROUP, nblk, nb)
        w = jnp.einsum("iojg,gh->jgiho", t, eye)
        return w.reshape(nblk, nb * S5_GROUP, nb * S5_GROUP)

    wb = jnp.concatenate([
        jnp.concatenate([bdiag_b(abrT), bdiag_b(abiT)], axis=2),
        jnp.concatenate([bdiag_b(bbrT), bdiag_b(bbiT)], axis=2)], axis=1).astype(BF16)
    wc = jnp.concatenate([
        jnp.concatenate([bdiag_c(carT), bdiag_c(ca2rT)], axis=2),
        jnp.concatenate([-bdiag_c(caiT), -bdiag_c(ca2iT)], axis=2)], axis=1).astype(BF16)
    d0, d1 = bdiag_k(k0), bdiag_k(k1)
    wd = jnp.concatenate([
        jnp.concatenate([d0, d1], axis=2),
        jnp.concatenate([jnp.zeros_like(d0), d0], axis=2)], axis=1).astype(BF16)
    return wb, wc, wd


def _s5_layer_kernel(x_ref, mod_ref, g_ref, win_ref, ar_ref, ai_ref, wb_ref, wc_ref, wd_ref,
                     hr0_ref, hi0_ref, d_ref, wg_ref, bglu_ref, wo_ref, o_ref, hr_ref, hi_ref,
                     u_scr, z_scr, y_scr, bu_scr, xr_scr, xi_scr, *, steps, batch):
    i = pl.program_id(0)
    rows = steps * batch
    pairs = steps // 2
    prow = pairs * batch
    d_model = x_ref.shape[-1]
    e = u_scr.shape[-1]
    nblk = wb_ref.shape[0]
    half = wb_ref.shape[2] // 2
    ulanes = wb_ref.shape[1] // 2

    @pl.when(i == 0)
    def _():
        xr_scr[...] = hr0_ref[...]
        xi_scr[...] = hi0_ref[...]

    x3 = jnp.swapaxes(x_ref[...], 0, 1)
    m = mod_ref[...]
    h3 = x3 * lax.rsqrt(jnp.mean(x3 * x3, axis=-1, keepdims=True) + EPS) * g_ref[...]
    h3 = h3 * (1.0 + m[1]) + m[0]
    p = _dot(h3.reshape(rows, d_model), win_ref[...])
    u_scr[...] = p[:, :e].reshape(pairs, 2, batch, e)
    z_scr[...] = p[:, e:]

    for j in range(nblk):
        sl = slice(j * half, (j + 1) * half)
        ch = slice(j * ulanes, (j + 1) * ulanes)
        buf = bu_scr.at[j % 2]
        lhs = jnp.concatenate([u_scr[:, 0, :, ch].reshape(prow, ulanes),
                               u_scr[:, 1, :, ch].reshape(prow, ulanes)], axis=1).astype(BF16)
        buf[batch:batch + prow, :] = jnp.dot(lhs, wb_ref[j], preferred_element_type=F32)
        ar = jnp.broadcast_to(ar_ref[:, sl], (batch, half))
        ai = jnp.broadcast_to(ai_ref[:, sl], (batch, half))
        xr = xr_scr[:, sl]
        xi = xi_scr[:, sl]
        buf[0:batch, 0:half] = xr
        buf[0:batch, half:2 * half] = xi
        for t in range(1, pairs + 1):
            rows_t = slice(t * batch, (t + 1) * batch)
            xr, xi = (ar * xr - ai * xi + buf[rows_t, 0:half],
                      ar * xi + ai * xr + buf[rows_t, half:2 * half])
            buf[rows_t, 0:half] = xr
            buf[rows_t, half:2 * half] = xi
        xr_scr[:, sl] = xr
        xi_scr[:, sl] = xi
        ypair = _dot(buf[0:prow, :], wc_ref[j]) + jnp.dot(lhs, wd_ref[j],
                                                          preferred_element_type=F32)
        y_scr[:, 0, :, ch] = ypair[:, :ulanes].reshape(pairs, batch, ulanes)
        y_scr[:, 1, :, ch] = ypair[:, ulanes:].reshape(pairs, batch, ulanes)

    y = jax.nn.gelu(y_scr[...].reshape(rows, e) + d_ref[...] * u_scr[...].reshape(rows, e))
    y = y * jax.nn.sigmoid(_dot(y, wg_ref[...]) + bglu_ref[...])
    y = y * jax.nn.silu(z_scr[...])
    out3 = _dot(y, wo_ref[...]).reshape(steps, batch, d_model)
    o_ref[...] = jnp.swapaxes(x3 + m[2] * out3, 0, 1)

    @pl.when(i == pl.num_programs(0) - 1)
    def _():
        hr_ref[...] = xr_scr[...]
        hi_ref[...] = xi_scr[...]


def _s5_layer(x, mod, g, w_in, ar, ai, wb, wc, wd, hr0, hi0, d, w_glu, b_glu, w_out):
    B, L, D = x.shape
    E = w_in.shape[1] // 2
    n = ar.shape[1]
    steps = min(S5_TILE_STEPS, L)
    assert steps % 2 == 0 and L % steps == 0 and B % SUBLANES == 0
    rows = steps * B
    pairs = steps // 2
    const2 = lambda i: (0, 0)
    const3 = lambda i: (0, 0, 0)

    def resident(shape):
        return pl.BlockSpec(shape, const2 if len(shape) == 2 else const3,
                            pipeline_mode=pl.Buffered(1))

    return pl.pallas_call(
        functools.partial(_s5_layer_kernel, steps=steps, batch=B),
        grid=(L // steps,),
        in_specs=[
            pl.BlockSpec((B, steps, D), lambda i: (0, i, 0)),
            resident((3, B, D)),
            resident((1, D)),
            resident((D, 2 * E)),
            resident((1, n)),
            resident((1, n)),
            resident(wb.shape),
            resident(wc.shape),
            resident(wd.shape),
            resident((B, n)),
            resident((B, n)),
            resident((1, E)),
            resident((E, E)),
            resident((1, E)),
            resident((E, D)),
        ],
        out_specs=[
            pl.BlockSpec((B, steps, D), lambda i: (0, i, 0)),
            pl.BlockSpec((B, n), const2),
            pl.BlockSpec((B, n), const2),
        ],
        out_shape=[
            jax.ShapeDtypeStruct((B, L, D), F32),
            jax.ShapeDtypeStruct((B, n), F32),
            jax.ShapeDtypeStruct((B, n), F32),
        ],
        scratch_shapes=[
            pltpu.VMEM((pairs, 2, B, E), F32),
            pltpu.VMEM((rows, E), F32),
            pltpu.VMEM((pairs, 2, B, E), F32),
            pltpu.VMEM((2, (pairs + 1) * B, wb.shape[2]), F32),
            pltpu.VMEM((B, n), F32),
            pltpu.VMEM((B, n), F32),
        ],
        compiler_params=_params("arbitrary"),
        name="s5_layer",
    )(x, mod, g.reshape(1, D), w_in.astype(BF16), ar, ai, wb, wc, wd, hr0, hi0, d.reshape(1, E),
      w_glu.astype(BF16), b_glu.reshape(1, E), w_out.astype(BF16))


def _l2norm(x):
    return x * lax.rsqrt(jnp.sum(x * x, axis=-1, keepdims=True) + EPS)


def _gdn_chunk(L):
    return GDN_CHUNK if L % GDN_CHUNK == 0 else L


def _split3(x):
    hi = x.astype(BF16)
    r = x - hi.astype(F32)
    mid = r.astype(BF16)
    lo = (r - mid.astype(F32)).astype(BF16)
    return hi, mid, lo


def _gdn_in_kernel(x_ref, mod_ref, g_ref, w_ref, alog_ref, dtb_ref, cw_ref,
                   hist_ref, act_ref, z_ref, bgc_ref, hist_out_ref, ext_scr, tri_scr,
                   *, heads, qk_heads, chunk):
    i = pl.program_id(1)
    tm = x_ref.shape[1]
    qk_width = qk_heads * GDN_DK
    conv_ch = act_ref.shape[-1]
    cb = qk_width

    @pl.when(i == 0)
    def _():
        ext_scr[0:HIST_ROWS, :] = hist_ref[0]
        row = lax.broadcasted_iota(jnp.int32, (tm, tm), 0)
        col = lax.broadcasted_iota(jnp.int32, (tm, tm), 1)
        sh = int(math.log2(chunk))
        tri_scr[...] = ((row >= col) & ((row >> sh) == (col >> sh))).astype(BF16)

    h = _norm_mod(x_ref[0], g_ref[...], mod_ref[0]).astype(BF16)
    cw = cw_ref[...]
    assert GDN_CONV == 4 and HIST_ROWS >= GDN_CONV - 1

    nblk = conv_ch // cb
    zb = z_ref.shape[-1] // nblk
    wcols = cb + zb

    def project(nb):
        width = wcols + (LANES if nb == 0 else 0)
        start = nb * wcols + (LANES if nb > 0 else 0)
        r = jnp.dot(h, w_ref[:, start:start + width], preferred_element_type=F32)
        ext_scr[HIST_ROWS:HIST_ROWS + tm, nb * cb:(nb + 1) * cb] = r[:, :cb]
        z_ref[0, :, nb * zb:(nb + 1) * zb] = r[:, cb:wcols].astype(z_ref.dtype)
        return r[:, wcols:]

    def conv_act(nb):
        cols = slice(nb * cb, (nb + 1) * cb)
        e = ext_scr[:, cols]
        e1 = pltpu.roll(e, 1, 0)
        wj = cw[:, cols]
        near = e * wj[3:4] + e1 * wj[2:3]
        far = pltpu.roll(e * wj[1:2] + e1 * wj[0:1], 2, 0)
        conv = (near + far)[HIST_ROWS:]
        ext_scr[0:HIST_ROWS, cols] = ext_scr[tm:tm + HIST_ROWS, cols]
        act = jax.nn.silu(conv)
        for hh in range(cb // GDN_DK):
            a = act[:, hh * GDN_DK:(hh + 1) * GDN_DK]
            if nb == 0:
                a = _l2norm(a) * (GDN_DK ** -0.5)
            elif nb == 1:
                a = _l2norm(a)
            act_ref[0, :, nb * cb + hh * GDN_DK:nb * cb + (hh + 1) * GDN_DK] = a

    for nb in range(nblk):
        tail = project(nb)
        if nb == 0:
            ba = tail
        conv_act(nb)

    beta = jax.nn.sigmoid(ba)
    a = ba + dtb_ref[...]
    softplus = jnp.maximum(a, 0.0) + jnp.log(1.0 + jnp.exp(-jnp.abs(a)))
    lane = lax.broadcasted_iota(jnp.int32, ba.shape, 1)
    live = (lane >= heads) & (lane < 2 * heads)
    g = jnp.where(live, -jnp.exp(alog_ref[...]) * softplus, 0.0)
    g_hi, g_mid, g_lo = _split3(g)
    packed = (g_hi.astype(F32) + pltpu.roll(g_mid.astype(F32), heads, 1)
              + pltpu.roll(g_lo.astype(F32), 2 * heads, 1)).astype(BF16)
    r = jnp.dot(tri_scr[...], packed, preferred_element_type=F32)
    gc = r + pltpu.roll(r, LANES - heads, 1) + pltpu.roll(r, LANES - 2 * heads, 1)
    bgc_ref[0] = jnp.where(lane < heads, beta, gc)

    @pl.when(i == pl.num_programs(1) - 1)
    def _():
        hist_out_ref[0] = ext_scr[0:HIST_ROWS, :]


def _gdn_in(x, mod, g, w_in, a_log, dt_bias, conv_w, hist, v_width):
    B, L, D = x.shape
    H = a_log.shape[0]
    conv_ch = conv_w.shape[-1]
    tm = _row_tile(L)
    o1, o2 = conv_ch, conv_ch + v_width
    wba = jnp.zeros((D, LANES), F32).at[:, :2 * H].set(w_in[:, o2:o2 + 2 * H])
    alog = jnp.zeros((1, LANES), F32).at[0, H:2 * H].set(a_log)
    dtb = jnp.zeros((1, LANES), F32).at[0, H:2 * H].set(dt_bias)
    hist8 = jnp.zeros((B, HIST_ROWS, conv_ch), F32).at[:, HIST_ROWS - (GDN_CONV - 1):, :].set(hist)
    cb = (H // 2) * GDN_DK
    nblk = conv_ch // cb
    zb = v_width // nblk
    pieces = []
    for nb in range(nblk):
        pieces += [w_in[:, nb * cb:(nb + 1) * cb], w_in[:, o1 + nb * zb:o1 + (nb + 1) * zb]]
        if nb == 0:
            pieces.append(wba)
    w_cat = jnp.concatenate(pieces, axis=1)
    const = lambda b, i: (0, 0)

    def resident(shape):
        return pl.BlockSpec(shape, const, pipeline_mode=pl.Buffered(1))

    return pl.pallas_call(
        functools.partial(_gdn_in_kernel, heads=H, qk_heads=H // 2, chunk=_gdn_chunk(L)),
        grid=(B, L // tm),
        in_specs=[
            pl.BlockSpec((1, tm, D), lambda b, i: (b, i, 0)),
            pl.BlockSpec((1, 3, D), lambda b, i: (b, 0, 0)),
            resident((1, D)),
            resident((D, conv_ch + v_width + LANES)),
            resident((1, LANES)),
            resident((1, LANES)),
            resident((GDN_CONV, conv_ch)),
            pl.BlockSpec((1, HIST_ROWS, conv_ch), lambda b, i: (b, 0, 0)),
        ],
        out_specs=[
            pl.BlockSpec((1, tm, conv_ch), lambda b, i: (b, i, 0)),
            pl.BlockSpec((1, tm, v_width), lambda b, i: (b, i, 0)),
            pl.BlockSpec((1, tm, LANES), lambda b, i: (b, i, 0)),
            pl.BlockSpec((1, HIST_ROWS, conv_ch), lambda b, i: (b, 0, 0)),
        ],
        out_shape=[
            jax.ShapeDtypeStruct((B, L, conv_ch), F32),
            jax.ShapeDtypeStruct((B, L, v_width), BF16),
            jax.ShapeDtypeStruct((B, L, LANES), F32),
            jax.ShapeDtypeStruct((B, HIST_ROWS, conv_ch), F32),
        ],
        scratch_shapes=[pltpu.VMEM((HIST_ROWS + tm, conv_ch), F32), pltpu.VMEM((tm, tm), BF16)],
        compiler_params=_params("parallel", "arbitrary"),
        name="gdn_in",
    )(x, mod, g.reshape(1, D), w_cat.astype(BF16), alog, dtb, conv_w, hist8)


def _gdn_core_kernel(act_ref, bgc_ref, s0_ref, o_ref, sout_ref, s_scr, *, T, qk_heads, v_heads):
    c = pl.program_id(1)
    nrows = act_ref.shape[0]
    qk_width = qk_heads * GDN_DK
    rep = v_heads // qk_heads

    @pl.when(c == 0)
    def _():
        s_scr[...] = s0_ref[...]

    row = lax.broadcasted_iota(jnp.int32, (T, T), 0)
    col = lax.broadcasted_iota(jnp.int32, (T, T), 1)
    causal = row >= col
    strict = row > col
    eye = (lax.broadcasted_iota(jnp.int32, (LANES, LANES), 0)
           == lax.broadcasted_iota(jnp.int32, (LANES, LANES), 1)).astype(BF16)
    bgs, gcts = [], []
    for r in range(nrows):
        bg = bgc_ref[r]
        gc_hi, gc_mid, gc_lo = _split3(bg)
        bgs.append(bg)
        gcts.append(lax.dot_general(eye, gc_hi, NT_DIMS, preferred_element_type=F32)
                    + lax.dot_general(eye, gc_mid, NT_DIMS, preferred_element_type=F32)
                    + lax.dot_general(eye, gc_lo, NT_DIMS, preferred_element_type=F32))
    sh = int(math.log2(INV_BLOCK))
    diag_blk = (row >> sh) == (col >> sh)
    merges = []
    while (1 << sh) < T:
        merges.append(((row >> (sh + 1)) == (col >> (sh + 1))) & ((row >> sh) > (col >> sh)))
        sh += 1

    qkp = [(r, hq) for r in range(nrows) for hq in range(qk_heads)]
    units = [(r, h) for r in range(nrows) for h in range(v_heads)]
    n = range(len(units))
    qk_of = [r * qk_heads + h // rep for r, h in units]
    qs = [act_ref[r, :, hq * GDN_DK:(hq + 1) * GDN_DK] for r, hq in qkp]
    ks = [act_ref[r, :, qk_width + hq * GDN_DK:qk_width + (hq + 1) * GDN_DK] for r, hq in qkp]
    kq = [lax.dot_general(jnp.concatenate([ks[i], qs[i]], axis=0).astype(BF16),
                          ks[i].astype(BF16), NT_DIMS, preferred_element_type=F32)
          for i in range(len(qkp))]
    kks = [kq[i][:T] for i in qk_of]
    qks = [kq[i][T:] for i in qk_of]
    qu = [qs[i] for i in qk_of]
    ku = [ks[i] for i in qk_of]
    beta = [bgs[r][:, h:h + 1] for r, h in units]
    gcol = [bgs[r][:, v_heads + h:v_heads + h + 1] for r, h in units]
    grow = [gcts[r][v_heads + h:v_heads + h + 1, :] for r, h in units]
    decay = [jnp.where(causal, jnp.exp(jnp.where(causal, gcol[u] - grow[u], 0.0)), 0.0) for u in n]
    egc = [jnp.exp(gcol[u]) for u in n]
    m = [jnp.where(strict, (beta[u] * kks[u]) * decay[u], 0.0) for u in n]
    p = [jnp.where(diag_blk, -m[u], 0.0) for u in n]
    e = list(p)
    levels = int(math.log2(INV_BLOCK))
    p = [_dot(p[u], p[u]) for u in n]
    for k in range(1, levels):
        if k + 1 < levels:
            pe = [_dot(jnp.concatenate([p[u], e[u]], axis=0), p[u]) for u in n]
            e = [e[u] + p[u] + pe[u][T:] for u in n]
            p = [pe[u][:T] for u in n]
        else:
            e = [e[u] + p[u] + _dot(e[u], p[u]) for u in n]
    for blk in merges:
        m21 = [jnp.where(blk, m[u], 0.0) for u in n]
        y = [m21[u] + _dot(m21[u], e[u]) for u in n]
        e = [e[u] - (y[u] + _dot(e[u], y[u])) for u in n]
    rhs = [jnp.concatenate(
        [act_ref[r, :, 2 * qk_width + h * GDN_DV:2 * qk_width + (h + 1) * GDN_DV] * beta[u],
         ku[u] * (beta[u] * egc[u])], axis=1) for u, (r, h) in enumerate(units)]
    sol = [rhs[u] + _dot(e[u], rhs[u]) for u in n]
    s = [s_scr[r, h] for r, h in units]
    ws = [_dot(jnp.concatenate([sol[u][:, GDN_DV:], qu[u] * egc[u]], axis=0), s[u]) for u in n]
    v_new = [sol[u][:, :GDN_DV] - ws[u][:T] for u in n]
    for u, (r, h) in enumerate(units):
        o_ref[r, :, h * GDN_DV:(h + 1) * GDN_DV] = (
            ws[u][T:] + _dot(qks[u] * decay[u], v_new[u])).astype(o_ref.dtype)
    g_last = [grow[u][:, T - 1:T] for u in n]
    for u, (r, h) in enumerate(units):
        kd = ku[u] * jnp.exp(g_last[u] - gcol[u])
        s_scr[r, h] = s[u] * jnp.exp(g_last[u]) + lax.dot_general(
            kd.astype(BF16), v_new[u].astype(BF16), (((0,), (0,)), ((), ())),
            preferred_element_type=F32)

    @pl.when(c == pl.num_programs(1) - 1)
    def _():
        sout_ref[...] = s_scr[...]


def _gdn_core(act, bgc, s0):
    B, L, C = act.shape
    H = s0.shape[1]
    T = _gdn_chunk(L)
    nr = GDN_ROWS_PER_STEP
    return pl.pallas_call(
        functools.partial(_gdn_core_kernel, T=T, qk_heads=H // 2, v_heads=H),
        grid=(B // nr, L // T),
        in_specs=[
            pl.BlockSpec((nr, T, C), lambda b, c: (b, c, 0)),
            pl.BlockSpec((nr, T, LANES), lambda b, c: (b, c, 0)),
            pl.BlockSpec((nr, H, GDN_DK, GDN_DV), lambda b, c: (b, 0, 0, 0)),
        ],
        out_specs=[
            pl.BlockSpec((nr, T, H * GDN_DV), lambda b, c: (b, c, 0)),
            pl.BlockSpec((nr, H, GDN_DK, GDN_DV), lambda b, c: (b, 0, 0, 0)),
        ],
        out_shape=[
            jax.ShapeDtypeStruct((B, L, H * GDN_DV), BF16),
            jax.ShapeDtypeStruct((B, H, GDN_DK, GDN_DV), F32),
        ],
        scratch_shapes=[pltpu.VMEM((nr, H, GDN_DK, GDN_DV), F32)],
        compiler_params=_params("parallel", "arbitrary"),
        name="gdn_core",
    )(act, bgc, s0)


def _gdn_out_kernel(o_ref, z_ref, x_ref, mod_ref, ng_ref, wo_ref, fg_ref, y_ref, *, heads):
    o = o_ref[0].astype(F32)
    parts = []
    for h in range(heads):
        oh = o[:, h * GDN_DV:(h + 1) * GDN_DV]
        parts.append(oh * lax.rsqrt(jnp.mean(oh * oh, axis=-1, keepdims=True) + EPS))
    on = jnp.concatenate(parts, axis=1) * ng_ref[...]
    out = _dot(on * jax.nn.silu(z_ref[0].astype(F32)), wo_ref[...])
    x2 = x_ref[0] + mod_ref[0][2:3] * out
    y_ref[0] = x2 * lax.rsqrt(jnp.mean(x2 * x2, axis=-1, keepdims=True) + EPS) * fg_ref[...]


def _gdn_out(o, z, x, mod, norm_g, w_out, final_g):
    B, L, D = x.shape
    V = o.shape[-1]
    H = V // GDN_DV
    tm = _row_tile(L, STREAM_ROW_TILE)
    const = lambda b, i: (0, 0)
    return pl.pallas_call(
        functools.partial(_gdn_out_kernel, heads=H),
        grid=(B, L // tm),
        in_specs=[
            pl.BlockSpec((1, tm, V), lambda b, i: (b, i, 0)),
            pl.BlockSpec((1, tm, V), lambda b, i: (b, i, 0)),
            pl.BlockSpec((1, tm, D), lambda b, i: (b, i, 0)),
            pl.BlockSpec((1, 3, D), lambda b, i: (b, 0, 0)),
            pl.BlockSpec((1, V), const),
            pl.BlockSpec((V, D), const),
            pl.BlockSpec((1, D), const),
        ],
        out_specs=pl.BlockSpec((1, tm, D), lambda b, i: (b, i, 0)),
        out_shape=jax.ShapeDtypeStruct((B, L, D), F32),
        compiler_params=_params("parallel", "parallel"),
        name="gdn_out",
    )(o, z, x, mod, jnp.tile(norm_g, H).reshape(1, V), w_out.astype(BF16), final_g.reshape(1, D))


def _trunk(x, mod, s5_re0, s5_im0, gdn_s0, gdn_conv0, w):
    B, L, D = x.shape
    G, P = w["s5_lambda_re"].shape[1:]
    n = G * P

    a2r, a2i, *ops = _s5_disc(w["s5_log_step"][0], w["s5_lambda_re"][0], w["s5_lambda_im"][0],
                              w["s5_b_re"][0], w["s5_b_im"][0], w["s5_c_re"][0], w["s5_c_im"][0])
    wb, wc, wd = _s5_block_weights(*ops)
    x1, hr, hi = _s5_layer(x, mod[0].transpose(1, 0, 2), w["norm_g"][0], w["s5_w_in"][0], a2r, a2i,
                           wb, wc, wd, s5_re0[0].reshape(B, n), s5_im0[0].reshape(B, n),
                           w["s5_d"][0], w["s5_w_glu"][0], w["s5_b_glu"][0], w["s5_w_out"][0])

    v_width = w["gdn_w_out"].shape[1]
    act, z2, bgc, hist8 = _gdn_in(x1, mod[1], w["norm_g"][1], w["gdn_w_in"][0], w["gdn_a_log"][0],
                                  w["gdn_dt_bias"][0], w["gdn_conv_w"][0], gdn_conv0[0], v_width)
    o, s_new = _gdn_core(act, bgc, gdn_s0[0])
    y = _gdn_out(o, z2, x1, mod[1], w["gdn_norm_g"][0], w["gdn_w_out"][0], w["final_g"])
    new_hist = hist8[:, HIST_ROWS - (GDN_CONV - 1):, :]
    return (y, hr.reshape(1, B, G, P), hi.reshape(1, B, G, P), s_new[None], new_hist[None])


def kernel(x_prompt, x_sample, c_prompt, c_sample, state_s5_re, state_s5_im, state_gdn, state_gdn_conv, norm_g, w_ada, b_ada, s5_w_in, s5_log_step, s5_lambda_re, s5_lambda_im, s5_b_re, s5_b_im, s5_c_re, s5_c_im, s5_d, s5_w_glu, s5_b_glu, s5_w_out, gdn_w_in, gdn_conv_w, gdn_a_log, gdn_dt_bias, gdn_norm_g, gdn_w_out, final_g):
    w = dict(norm_g=norm_g, s5_w_in=s5_w_in, s5_log_step=s5_log_step, s5_lambda_re=s5_lambda_re,
             s5_lambda_im=s5_lambda_im, s5_b_re=s5_b_re, s5_b_im=s5_b_im, s5_c_re=s5_c_re,
             s5_c_im=s5_c_im, s5_d=s5_d, s5_w_glu=s5_w_glu, s5_b_glu=s5_b_glu, s5_w_out=s5_w_out,
             gdn_w_in=gdn_w_in, gdn_conv_w=gdn_conv_w, gdn_a_log=gdn_a_log,
             gdn_dt_bias=gdn_dt_bias, gdn_norm_g=gdn_norm_g, gdn_w_out=gdn_w_out, final_g=final_g)
    bp, _, d = x_prompt.shape
    bs = x_sample.shape[0]
    depth = w_ada.shape[0]
    mod = _ada_mod(jnp.concatenate([c_prompt, c_sample], axis=0), w_ada, b_ada)
    mod = mod.reshape(depth, bp + bs, 3, d)

    z_s5 = jnp.zeros((state_s5_re.shape[0], bp) + state_s5_re.shape[2:], F32)
    z_gdn = jnp.zeros((state_gdn.shape[0], bp) + state_gdn.shape[2:], F32)
    z_conv = jnp.zeros((state_gdn_conv.shape[0], bp) + state_gdn_conv.shape[2:], F32)
    yp, s5r_p, s5i_p, gdn_p, conv_p = _trunk(x_prompt, mod[:, :bp], z_s5, z_s5, z_gdn, z_conv, w)
    ys, s5r_s, s5i_s, gdn_s, conv_s = _trunk(x_sample, mod[:, bp:], state_s5_re, state_s5_im,
                                             state_gdn, state_gdn_conv, w)
    return (yp, ys, s5r_p, s5i_p, gdn_p, conv_p, s5r_s, s5i_s, gdn_s, conv_s)
```

```python
import functools
import math

import jax
import jax.numpy as jnp
from jax import lax
from jax.experimental import pallas as pl
from jax.experimental.pallas import tpu as pltpu

F32 = jnp.float32
BF16 = jnp.bfloat16
EPS = 1e-6

S5_GROUP = 16
S5_STATE = 64
S5_BLOCK_GROUPS = 8
GDN_DK = 128
GDN_DV = 128
GDN_CONV = 4
GDN_CHUNK = 64
GDN_ROWS_PER_STEP = 2
INV_BLOCK = 16
LANES = 128
SUBLANES = 8
HIST_ROWS = SUBLANES
V7X_VMEM_BYTES = 64 * 1024 * 1024
VMEM_LIMIT = V7X_VMEM_BYTES * 7 // 8
ROW_TILE = 512
STREAM_ROW_TILE = 1024
S5_TILE_STEPS = 64
ADA_COL_TILE = 768
HI = lax.Precision.HIGHEST
NT_DIMS = (((1,), (1,)), ((), ()))


def _params(*sem):
    return pltpu.CompilerParams(dimension_semantics=sem, vmem_limit_bytes=VMEM_LIMIT)


def _dot(a, b):
    return jnp.dot(a.astype(BF16), b.astype(BF16), preferred_element_type=F32)


def _row_tile(L, rows=ROW_TILE):
    return min(rows, L)


def _ada_kernel(c_ref, w_ref, b_ref, o_ref):
    c = c_ref[...]
    o_ref[0] = _dot(jax.nn.silu(c), w_ref[0]) + b_ref[0]


def _ada_mod(c_all, w_ada, b_ada):
    depth, d, d3 = w_ada.shape
    r = c_all.shape[0]
    tn = ADA_COL_TILE
    return pl.pallas_call(
        _ada_kernel,
        grid=(depth, d3 // tn),
        in_specs=[
            pl.BlockSpec((r, d), lambda i, j: (0, 0)),
            pl.BlockSpec((1, d, tn), lambda i, j: (i, 0, j)),
            pl.BlockSpec((1, 1, tn), lambda i, j: (i, 0, j)),
        ],
        out_specs=pl.BlockSpec((1, r, tn), lambda i, j: (i, 0, j)),
        out_shape=jax.ShapeDtypeStruct((depth, r, d3), F32),
        compiler_params=_params("parallel", "parallel"),
        name="ada_mod",
    )(c_all, w_ada, b_ada.reshape(depth, 1, d3))


def _norm_mod(x, g, m):
    h = x * lax.rsqrt(jnp.mean(x * x, axis=-1, keepdims=True) + EPS) * g
    return h * (1.0 + m[1:2]) + m[0:1]


def _s5_disc_kernel(ls_ref, lr_ref, li_ref, br_ref, bi_ref, cr_ref, ci_ref, seg_ref,
                    a2r_ref, a2i_ref, bbr_ref, bbi_ref, abr_ref, abi_ref, car_ref, cai_ref,
                    ca2r_ref, ca2i_ref, k0_ref, k1_ref):
    step = jnp.exp(ls_ref[...])
    lr = lr_ref[...]
    li = li_ref[...]
    mag = jnp.exp(lr * step)
    ar = mag * jnp.cos(li * step)
    ai = mag * jnp.sin(li * step)
    den = lr * lr + li * li
    xr = ar - 1.0
    nr = (xr * lr + ai * li) / den
    ni = (ai * lr - xr * li) / den
    bbr = nr * br_ref[...] - ni * bi_ref[...]
    bbi = nr * bi_ref[...] + ni * br_ref[...]
    abr = ar * bbr - ai * bbi
    abi = ar * bbi + ai * bbr
    a2r = ar * ar - ai * ai
    a2i = 2.0 * (ar * ai)
    cr = cr_ref[...]
    ci = ci_ref[...]
    a2r_ref[...] = a2r
    a2i_ref[...] = a2i
    bbr_ref[...] = bbr
    bbi_ref[...] = bbi
    abr_ref[...] = abr
    abi_ref[...] = abi
    car_ref[...] = cr * ar - ci * ai
    cai_ref[...] = cr * ai + ci * ar
    ca2r_ref[...] = cr * a2r - ci * a2i
    ca2i_ref[...] = cr * a2i + ci * a2r
    nc = br_ref.shape[0]
    prod0 = jnp.concatenate([cr * bbr[c:c + 1] - ci * bbi[c:c + 1] for c in range(nc)], axis=0)
    prod1 = jnp.concatenate([cr * abr[c:c + 1] - ci * abi[c:c + 1] for c in range(nc)], axis=0)
    seg = seg_ref[...]
    k0_ref[...] = jnp.dot(prod0, seg, precision=HI, preferred_element_type=F32)
    k1_ref[...] = jnp.dot(prod1, seg, precision=HI, preferred_element_type=F32)


def _s5_disc(log_step, lam_re, lam_im, b_re, b_im, c_re, c_im):
    G, P = lam_re.shape
    n = G * P
    c = b_re.shape[-1]
    ls = jnp.broadcast_to(log_step[:, None], (G, P)).reshape(1, n)
    brT = b_re.transpose(2, 0, 1).reshape(c, n)
    biT = b_im.transpose(2, 0, 1).reshape(c, n)
    crT = c_re.transpose(1, 0, 2).reshape(c, n)
    ciT = c_im.transpose(1, 0, 2).reshape(c, n)
    seg = jnp.repeat(jnp.eye(G, dtype=F32), P, axis=0)
    vec = jax.ShapeDtypeStruct((1, n), F32)
    mat = jax.ShapeDtypeStruct((c, n), F32)
    kmat = jax.ShapeDtypeStruct((c * c, G), F32)
    return pl.pallas_call(
        _s5_disc_kernel,
        out_shape=[vec, vec] + [mat] * 8 + [kmat, kmat],
        name="s5_disc",
    )(ls, lam_re.reshape(1, n), lam_im.reshape(1, n), brT, biT, crT, ciT, seg)


def _s5_block_weights(bbrT, bbiT, abrT, abiT, carT, caiT, ca2rT, ca2iT, k0, k1):
    nb = S5_BLOCK_GROUPS
    G = k0.shape[-1]
    nblk = G // nb
    eye = jnp.eye(nb, dtype=F32)

    def bdiag_b(bbT):
        t = bbT.reshape(S5_GROUP, nblk, nb, S5_STATE)
        w = jnp.einsum("cjhp,gh->jgchp", t, eye)
        return w.reshape(nblk, nb * S5_GROUP, nb * S5_STATE)

    def bdiag_c(cT):
        t = cT.reshape(S5_GROUP, nblk, nb, S5_STATE)
        w = jnp.einsum("ojgp,gh->jhpgo", t, eye)
        return w.reshape(nblk, nb * S5_STATE, nb * S5_GROUP)

    def bdiag_k(k):
        t = k.reshape(S5_GROUP, S5_GROUP, nblk, nb)
        w = jnp.einsum("iojg,gh->jgiho", t, eye)
        return w.reshape(nblk, nb * S5_GROUP, nb * S5_GROUP)

    wb = jnp.concatenate([
        jnp.concatenate([bdiag_b(abrT), bdiag_b(abiT)], axis=2),
        jnp.concatenate([bdiag_b(bbrT), bdiag_b(bbiT)], axis=2)], axis=1).astype(BF16)
    wc = jnp.concatenate([
        jnp.concatenate([bdiag_c(carT), bdiag_c(ca2rT)], axis=2),
        jnp.concatenate([-bdiag_c(caiT), -bdiag_c(ca2iT)], axis=2)], axis=1).astype(BF16)
    d0, d1 = bdiag_k(k0), bdiag_k(k1)
    wd = jnp.concatenate([
        jnp.concatenate([d0, d1], axis=2),
        jnp.concatenate([jnp.zeros_like(d0), d0], axis=2)], axis=1).astype(BF16)
    return wb, wc, wd


def _s5_layer_kernel(x_ref, mod_ref, g_ref, win_ref, ar_ref, ai_ref, wb_ref, wc_ref, wd_ref,
                     hr0_ref, hi0_ref, d_ref, wg_ref, bglu_ref, wo_ref, o_ref, hr_ref, hi_ref,
                     u_scr, z_scr, y_scr, bu_scr, xr_scr, xi_scr, *, steps, batch):
    i = pl.program_id(0)
    rows = steps * batch
    pairs = steps // 2
    prow = pairs * batch
    d_model = x_ref.shape[-1]
    e = u_scr.shape[-1]
    nblk = wb_ref.shape[0]
    half = wb_ref.shape[2] // 2
    ulanes = wb_ref.shape[1] // 2

    @pl.when(i == 0)
    def _():
        xr_scr[...] = hr0_ref[...]
        xi_scr[...] = hi0_ref[...]

    x3 = jnp.swapaxes(x_ref[...], 0, 1)
    m = mod_ref[...]
    nparts = 2 if pairs % 2 == 0 else 1
    ppairs = pairs // nparts
    x3s = []
    for part in range(nparts):
        x3p = x3[part * 2 * ppairs:(part + 1) * 2 * ppairs]
        h3 = x3p * lax.rsqrt(jnp.mean(x3p * x3p, axis=-1, keepdims=True) + EPS) * g_ref[...]
        h3 = h3 * (1.0 + m[1]) + m[0]
        p = _dot(h3.reshape(2 * ppairs * batch, d_model), win_ref[...])
        u_scr[part * ppairs:(part + 1) * ppairs] = p[:, :e].reshape(ppairs, 2, batch, e)
        z_scr[part * 2 * ppairs * batch:(part + 1) * 2 * ppairs * batch, :] = p[:, e:]
        x3s.append(x3p)

    for j in range(nblk):
        sl = slice(j * half, (j + 1) * half)
        ch = slice(j * ulanes, (j + 1) * ulanes)
        buf = bu_scr.at[j % 2]
        lhs = jnp.concatenate([u_scr[:, 0, :, ch].reshape(prow, ulanes),
                               u_scr[:, 1, :, ch].reshape(prow, ulanes)], axis=1).astype(BF16)
        buf[batch:batch + prow, :] = jnp.dot(lhs, wb_ref[j], preferred_element_type=F32)
        ar = jnp.broadcast_to(ar_ref[:, sl], (batch, half))
        ai = jnp.broadcast_to(ai_ref[:, sl], (batch, half))
        xr = xr_scr[:, sl]
        xi = xi_scr[:, sl]
        buf[0:batch, 0:half] = xr
        buf[0:batch, half:2 * half] = xi
        for t in range(1, pairs + 1):
            rows_t = slice(t * batch, (t + 1) * batch)
            xr, xi = (ar * xr - ai * xi + buf[rows_t, 0:half],
                      ar * xi + ai * xr + buf[rows_t, half:2 * half])
            buf[rows_t, 0:half] = xr
            buf[rows_t, half:2 * half] = xi
        xr_scr[:, sl] = xr
        xi_scr[:, sl] = xi
        ypair = _dot(buf[0:prow, :], wc_ref[j]) + jnp.dot(lhs, wd_ref[j],
                                                          preferred_element_type=F32)
        y_scr[:, 0, :, ch] = ypair[:, :ulanes].reshape(pairs, batch, ulanes)
        y_scr[:, 1, :, ch] = ypair[:, ulanes:].reshape(pairs, batch, ulanes)

    for part in range(nparts):
        prs = slice(part * ppairs, (part + 1) * ppairs)
        nrow = 2 * ppairs * batch
        y = jax.nn.gelu(y_scr[prs].reshape(nrow, e) + d_ref[...] * u_scr[prs].reshape(nrow, e))
        y = y * jax.nn.sigmoid(_dot(y, wg_ref[...]) + bglu_ref[...])
        y = y * jax.nn.silu(z_scr[part * nrow:(part + 1) * nrow, :])
        out3 = _dot(y, wo_ref[...]).reshape(2 * ppairs, batch, d_model)
        o_ref[:, part * 2 * ppairs:(part + 1) * 2 * ppairs, :] = jnp.swapaxes(
            x3s[part] + m[2] * out3, 0, 1)

    @pl.when(i == pl.num_programs(0) - 1)
    def _():
        hr_ref[...] = xr_scr[...]
        hi_ref[...] = xi_scr[...]


def _s5_layer(x, mod, g, w_in, ar, ai, wb, wc, wd, hr0, hi0, d, w_glu, b_glu, w_out):
    B, L, D = x.shape
    E = w_in.shape[1] // 2
    n = ar.shape[1]
    steps = min(S5_TILE_STEPS, L)
    assert steps % 2 == 0 and L % steps == 0 and B % SUBLANES == 0
    rows = steps * B
    pairs = steps // 2
    const2 = lambda i: (0, 0)
    const3 = lambda i: (0, 0, 0)

    def resident(shape):
        return pl.BlockSpec(shape, const2 if len(shape) == 2 else const3,
                            pipeline_mode=pl.Buffered(1))

    return pl.pallas_call(
        functools.partial(_s5_layer_kernel, steps=steps, batch=B),
        grid=(L // steps,),
        in_specs=[
            pl.BlockSpec((B, steps, D), lambda i: (0, i, 0)),
            resident((3, B, D)),
            resident((1, D)),
            resident((D, 2 * E)),
            resident((1, n)),
            resident((1, n)),
            resident(wb.shape),
            resident(wc.shape),
            resident(wd.shape),
            resident((B, n)),
            resident((B, n)),
            resident((1, E)),
            resident((E, E)),
            resident((1, E)),
            resident((E, D)),
        ],
        out_specs=[
            pl.BlockSpec((B, steps, D), lambda i: (0, i, 0)),
            pl.BlockSpec((B, n), const2),
            pl.BlockSpec((B, n), const2),
        ],
        out_shape=[
            jax.ShapeDtypeStruct((B, L, D), F32),
            jax.ShapeDtypeStruct((B, n), F32),
            jax.ShapeDtypeStruct((B, n), F32),
        ],
        scratch_shapes=[
            pltpu.VMEM((pairs, 2, B, E), F32),
            pltpu.VMEM((rows, E), F32),
            pltpu.VMEM((pairs, 2, B, E), F32),
            pltpu.VMEM((2, (pairs + 1) * B, wb.shape[2]), F32),
            pltpu.VMEM((B, n), F32),
            pltpu.VMEM((B, n), F32),
        ],
        compiler_params=_params("arbitrary"),
        name="s5_layer",
    )(x, mod, g.reshape(1, D), w_in.astype(BF16), ar, ai, wb, wc, wd, hr0, hi0, d.reshape(1, E),
      w_glu.astype(BF16), b_glu.reshape(1, E), w_out.astype(BF16))


def _l2norm(x):
    return x * lax.rsqrt(jnp.sum(x * x, axis=-1, keepdims=True) + EPS)


def _gdn_chunk(L):
    return GDN_CHUNK if L % GDN_CHUNK == 0 else L


def _split3(x):
    hi = x.astype(BF16)
    r = x - hi.astype(F32)
    mid = r.astype(BF16)
    lo = (r - mid.astype(F32)).astype(BF16)
    return hi, mid, lo


def _gdn_in_kernel(x_ref, mod_ref, g_ref, w_ref, alog_ref, dtb_ref, cw_ref,
                   hist_ref, act_ref, z_ref, bgc_ref, hist_out_ref, ext_scr, tri_scr,
                   *, heads, qk_heads, chunk):
    i = pl.program_id(1)
    tm = x_ref.shape[1]
    qk_width = qk_heads * GDN_DK
    conv_ch = act_ref.shape[-1]
    cb = qk_width

    @pl.when(i == 0)
    def _():
        ext_scr[0:HIST_ROWS, :] = hist_ref[0]
        row = lax.broadcasted_iota(jnp.int32, (tm, tm), 0)
        col = lax.broadcasted_iota(jnp.int32, (tm, tm), 1)
        sh = int(math.log2(chunk))
        tri_scr[...] = ((row >= col) & ((row >> sh) == (col >> sh))).astype(BF16)

    h = _norm_mod(x_ref[0], g_ref[...], mod_ref[0]).astype(BF16)
    cw = cw_ref[...]
    assert GDN_CONV == 4 and HIST_ROWS >= GDN_CONV - 1

    nblk = conv_ch // cb
    zb = z_ref.shape[-1] // nblk
    wcols = cb + zb

    def project(nb):
        width = wcols + (LANES if nb == 0 else 0)
        start = nb * wcols + (LANES if nb > 0 else 0)
        r = jnp.dot(h, w_ref[:, start:start + width], preferred_element_type=F32)
        ext_scr[HIST_ROWS:HIST_ROWS + tm, nb * cb:(nb + 1) * cb] = r[:, :cb]
        z_ref[0, :, nb * zb:(nb + 1) * zb] = r[:, cb:wcols].astype(z_ref.dtype)
        return r[:, wcols:]

    def conv_act(nb):
        cols = slice(nb * cb, (nb + 1) * cb)
        e = ext_scr[:, cols]
        e1 = pltpu.roll(e, 1, 0)
        wj = cw[:, cols]
        near = e * wj[3:4] + e1 * wj[2:3]
        far = pltpu.roll(e * wj[1:2] + e1 * wj[0:1], 2, 0)
        conv = (near + far)[HIST_ROWS:]
        ext_scr[0:HIST_ROWS, cols] = ext_scr[tm:tm + HIST_ROWS, cols]
        act = jax.nn.silu(conv)
        for hh in range(cb // GDN_DK):
            a = act[:, hh * GDN_DK:(hh + 1) * GDN_DK]
            if nb == 0:
                a = _l2norm(a) * (GDN_DK ** -0.5)
            elif nb == 1:
                a = _l2norm(a)
            act_ref[0, :, nb * cb + hh * GDN_DK:nb * cb + (hh + 1) * GDN_DK] = a

    for nb in range(nblk):
        tail = project(nb)
        if nb == 0:
            ba = tail
        conv_act(nb)

    beta = jax.nn.sigmoid(ba)
    a = ba + dtb_ref[...]
    softplus = jnp.maximum(a, 0.0) + jnp.log(1.0 + jnp.exp(-jnp.abs(a)))
    lane = lax.broadcasted_iota(jnp.int32, ba.shape, 1)
    live = (lane >= heads) & (lane < 2 * heads)
    g = jnp.where(live, -jnp.exp(alog_ref[...]) * softplus, 0.0)
    g_hi, g_mid, g_lo = _split3(g)
    packed = (g_hi.astype(F32) + pltpu.roll(g_mid.astype(F32), heads, 1)
              + pltpu.roll(g_lo.astype(F32), 2 * heads, 1)).astype(BF16)
    r = jnp.dot(tri_scr[...], packed, preferred_element_type=F32)
    gc = r + pltpu.roll(r, LANES - heads, 1) + pltpu.roll(r, LANES - 2 * heads, 1)
    bgc_ref[0] = jnp.where(lane < heads, beta, gc)

    @pl.when(i == pl.num_programs(1) - 1)
    def _():
        hist_out_ref[0] = ext_scr[0:HIST_ROWS, :]


def _gdn_in(x, mod, g, w_in, a_log, dt_bias, conv_w, hist, v_width):
    B, L, D = x.shape
    H = a_log.shape[0]
    conv_ch = conv_w.shape[-1]
    tm = _row_tile(L)
    o1, o2 = conv_ch, conv_ch + v_width
    wba = jnp.zeros((D, LANES), F32).at[:, :2 * H].set(w_in[:, o2:o2 + 2 * H])
    alog = jnp.zeros((1, LANES), F32).at[0, H:2 * H].set(a_log)
    dtb = jnp.zeros((1, LANES), F32).at[0, H:2 * H].set(dt_bias)
    hist8 = jnp.zeros((B, HIST_ROWS, conv_ch), F32).at[:, HIST_ROWS - (GDN_CONV - 1):, :].set(hist)
    cb = (H // 2) * GDN_DK
    nblk = conv_ch // cb
    zb = v_width // nblk
    pieces = []
    for nb in range(nblk):
        pieces += [w_in[:, nb * cb:(nb + 1) * cb], w_in[:, o1 + nb * zb:o1 + (nb + 1) * zb]]
        if nb == 0:
            pieces.append(wba)
    w_cat = jnp.concatenate(pieces, axis=1)
    const = lambda b, i: (0, 0)

    def resident(shape):
        return pl.BlockSpec(shape, const, pipeline_mode=pl.Buffered(1))

    return pl.pallas_call(
        functools.partial(_gdn_in_kernel, heads=H, qk_heads=H // 2, chunk=_gdn_chunk(L)),
        grid=(B, L // tm),
        in_specs=[
            pl.BlockSpec((1, tm, D), lambda b, i: (b, i, 0)),
            pl.BlockSpec((1, 3, D), lambda b, i: (b, 0, 0)),
            resident((1, D)),
            resident((D, conv_ch + v_width + LANES)),
            resident((1, LANES)),
            resident((1, LANES)),
            resident((GDN_CONV, conv_ch)),
            pl.BlockSpec((1, HIST_ROWS, conv_ch), lambda b, i: (b, 0, 0)),
        ],
        out_specs=[
            pl.BlockSpec((1, tm, conv_ch), lambda b, i: (b, i, 0)),
            pl.BlockSpec((1, tm, v_width), lambda b, i: (b, i, 0)),
            pl.BlockSpec((1, tm, LANES), lambda b, i: (b, i, 0)),
            pl.BlockSpec((1, HIST_ROWS, conv_ch), lambda b, i: (b, 0, 0)),
        ],
        out_shape=[
            jax.ShapeDtypeStruct((B, L, conv_ch), F32),
            jax.ShapeDtypeStruct((B, L, v_width), BF16),
            jax.ShapeDtypeStruct((B, L, LANES), F32),
            jax.ShapeDtypeStruct((B, HIST_ROWS, conv_ch), F32),
        ],
        scratch_shapes=[pltpu.VMEM((HIST_ROWS + tm, conv_ch), F32), pltpu.VMEM((tm, tm), BF16)],
        compiler_params=_params("parallel", "arbitrary"),
        name="gdn_in",
    )(x, mod, g.reshape(1, D), w_cat.astype(BF16), alog, dtb, conv_w, hist8)


def _gdn_core_kernel(act_ref, bgc_ref, s0_ref, o_ref, sout_ref, s_scr, *, T, qk_heads, v_heads):
    c = pl.program_id(1)
    nrows = act_ref.shape[0]
    qk_width = qk_heads * GDN_DK
    rep = v_heads // qk_heads

    @pl.when(c == 0)
    def _():
        s_scr[...] = s0_ref[...]

    row = lax.broadcasted_iota(jnp.int32, (T, T), 0)
    col = lax.broadcasted_iota(jnp.int32, (T, T), 1)
    causal = row >= col
    strict = row > col
    eye = (lax.broadcasted_iota(jnp.int32, (LANES, LANES), 0)
           == lax.broadcasted_iota(jnp.int32, (LANES, LANES), 1)).astype(BF16)
    bgs, gcts = [], []
    for r in range(nrows):
        bg = bgc_ref[r]
        gc_hi, gc_mid, gc_lo = _split3(bg)
        bgs.append(bg)
        gcts.append(lax.dot_general(eye, gc_hi, NT_DIMS, preferred_element_type=F32)
                    + lax.dot_general(eye, gc_mid, NT_DIMS, preferred_element_type=F32)
                    + lax.dot_general(eye, gc_lo, NT_DIMS, preferred_element_type=F32))
    sh = int(math.log2(INV_BLOCK))
    diag_blk = (row >> sh) == (col >> sh)
    merges = []
    while (1 << sh) < T:
        merges.append(((row >> (sh + 1)) == (col >> (sh + 1))) & ((row >> sh) > (col >> sh)))
        sh += 1

    qkp = [(r, hq) for r in range(nrows) for hq in range(qk_heads)]
    units = [(r, h) for r in range(nrows) for h in range(v_heads)]
    n = range(len(units))
    qk_of = [r * qk_heads + h // rep for r, h in units]
    qs = [act_ref[r, :, hq * GDN_DK:(hq + 1) * GDN_DK] for r, hq in qkp]
    ks = [act_ref[r, :, qk_width + hq * GDN_DK:qk_width + (hq + 1) * GDN_DK] for r, hq in qkp]
    kq = [lax.dot_general(jnp.concatenate([ks[i], qs[i]], axis=0).astype(BF16),
                          ks[i].astype(BF16), NT_DIMS, preferred_element_type=F32)
          for i in range(len(qkp))]
    kks = [kq[i][:T] for i in qk_of]
    qks = [kq[i][T:] for i in qk_of]
    qu = [qs[i] for i in qk_of]
    ku = [ks[i] for i in qk_of]
    beta = [bgs[r][:, h:h + 1] for r, h in units]
    gcol = [bgs[r][:, v_heads + h:v_heads + h + 1] for r, h in units]
    grow = [gcts[r][v_heads + h:v_heads + h + 1, :] for r, h in units]
    decay = [jnp.where(causal, jnp.exp(jnp.where(causal, gcol[u] - grow[u], 0.0)), 0.0) for u in n]
    egc = [jnp.exp(gcol[u]) for u in n]
    m = [jnp.where(strict, (beta[u] * kks[u]) * decay[u], 0.0) for u in n]
    p = [jnp.where(diag_blk, -m[u], 0.0) for u in n]
    e = list(p)
    levels = int(math.log2(INV_BLOCK))
    p = [_dot(p[u], p[u]) for u in n]
    for k in range(1, levels):
        if k + 1 < levels:
            pe = [_dot(jnp.concatenate([p[u], e[u]], axis=0), p[u]) for u in n]
            e = [e[u] + p[u] + pe[u][T:] for u in n]
            p = [pe[u][:T] for u in n]
        else:
            e = [e[u] + p[u] + _dot(e[u], p[u]) for u in n]
    for blk in merges:
        m21 = [jnp.where(blk, m[u], 0.0) for u in n]
        y = [m21[u] + _dot(m21[u], e[u]) for u in n]
        e = [e[u] - (y[u] + _dot(e[u], y[u])) for u in n]
    rhs = [jnp.concatenate(
        [act_ref[r, :, 2 * qk_width + h * GDN_DV:2 * qk_width + (h + 1) * GDN_DV] * beta[u],
         ku[u] * (beta[u] * egc[u])], axis=1) for u, (r, h) in enumerate(units)]
    sol = [rhs[u] + _dot(e[u], rhs[u]) for u in n]
    s = [s_scr[r, h] for r, h in units]
    ws = [_dot(jnp.concatenate([sol[u][:, GDN_DV:], qu[u] * egc[u]], axis=0), s[u]) for u in n]
    v_new = [sol[u][:, :GDN_DV] - ws[u][:T] for u in n]
    for u, (r, h) in enumerate(units):
        o_ref[r, :, h * GDN_DV:(h + 1) * GDN_DV] = (
            ws[u][T:] + _dot(qks[u] * decay[u], v_new[u])).astype(o_ref.dtype)
    g_last = [grow[u][:, T - 1:T] for u in n]
    for u, (r, h) in enumerate(units):
        kd = ku[u] * jnp.exp(g_last[u] - gcol[u])
        s_scr[r, h] = s[u] * jnp.exp(g_last[u]) + lax.dot_general(
            kd.astype(BF16), v_new[u].astype(BF16), (((0,), (0,)), ((), ())),
            preferred_element_type=F32)

    @pl.when(c == pl.num_programs(1) - 1)
    def _():
        sout_ref[...] = s_scr[...]


def _gdn_core(act, bgc, s0):
    B, L, C = act.shape
    H = s0.shape[1]
    T = _gdn_chunk(L)
    nr = GDN_ROWS_PER_STEP
    return pl.pallas_call(
        functools.partial(_gdn_core_kernel, T=T, qk_heads=H // 2, v_heads=H),
        grid=(B // nr, L // T),
        in_specs=[
            pl.BlockSpec((nr, T, C), lambda b, c: (b, c, 0)),
            pl.BlockSpec((nr, T, LANES), lambda b, c: (b, c, 0)),
            pl.BlockSpec((nr, H, GDN_DK, GDN_DV), lambda b, c: (b, 0, 0, 0)),
        ],
        out_specs=[
            pl.BlockSpec((nr, T, H * GDN_DV), lambda b, c: (b, c, 0)),
            pl.BlockSpec((nr, H, GDN_DK, GDN_DV), lambda b, c: (b, 0, 0, 0)),
        ],
        out_shape=[
            jax.ShapeDtypeStruct((B, L, H * GDN_DV), BF16),
            jax.ShapeDtypeStruct((B, H, GDN_DK, GDN_DV), F32),
        ],
        scratch_shapes=[pltpu.VMEM((nr, H, GDN_DK, GDN_DV), F32)],
        compiler_params=_params("parallel", "arbitrary"),
        name="gdn_core",
    )(act, bgc, s0)


def _gdn_out_kernel(o_ref, z_ref, x_ref, mod_ref, ng_ref, wo_ref, fg_ref, y_ref, *, heads):
    o = o_ref[0].astype(F32)
    parts = []
    for h in range(heads):
        oh = o[:, h * GDN_DV:(h + 1) * GDN_DV]
        parts.append(oh * lax.rsqrt(jnp.mean(oh * oh, axis=-1, keepdims=True) + EPS))
    on = jnp.concatenate(parts, axis=1) * ng_ref[...]
    out = _dot(on * jax.nn.silu(z_ref[0].astype(F32)), wo_ref[...])
    x2 = x_ref[0] + mod_ref[0][2:3] * out
    y_ref[0] = x2 * lax.rsqrt(jnp.mean(x2 * x2, axis=-1, keepdims=True) + EPS) * fg_ref[...]


def _gdn_out(o, z, x, mod, norm_g, w_out, final_g):
    B, L, D = x.shape
    V = o.shape[-1]
    H = V // GDN_DV
    tm = _row_tile(L, STREAM_ROW_TILE)
    const = lambda b, i: (0, 0)
    return pl.pallas_call(
        functools.partial(_gdn_out_kernel, heads=H),
        grid=(B, L // tm),
        in_specs=[
            pl.BlockSpec((1, tm, V), lambda b, i: (b, i, 0)),
            pl.BlockSpec((1, tm, V), lambda b, i: (b, i, 0)),
            pl.BlockSpec((1, tm, D), lambda b, i: (b, i, 0)),
            pl.BlockSpec((1, 3, D), lambda b, i: (b, 0, 0)),
            pl.BlockSpec((1, V), const),
            pl.BlockSpec((V, D), const),
            pl.BlockSpec((1, D), const),
        ],
        out_specs=pl.BlockSpec((1, tm, D), lambda b, i: (b, i, 0)),
        out_shape=jax.ShapeDtypeStruct((B, L, D), F32),
        compiler_params=_params("parallel", "parallel"),
        name="gdn_out",
    )(o, z, x, mod, jnp.tile(norm_g, H).reshape(1, V), w_out.astype(BF16), final_g.reshape(1, D))


def _trunk(x, mod, s5_re0, s5_im0, gdn_s0, gdn_conv0, w):
    B, L, D = x.shape
    G, P = w["s5_lambda_re"].shape[1:]
    n = G * P

    a2r, a2i, *ops = _s5_disc(w["s5_log_step"][0], w["s5_lambda_re"][0], w["s5_lambda_im"][0],
                              w["s5_b_re"][0], w["s5_b_im"][0], w["s5_c_re"][0], w["s5_c_im"][0])
    wb, wc, wd = _s5_block_weights(*ops)
    x1, hr, hi = _s5_layer(x, mod[0].transpose(1, 0, 2), w["norm_g"][0], w["s5_w_in"][0], a2r, a2i,
                           wb, wc, wd, s5_re0[0].reshape(B, n), s5_im0[0].reshape(B, n),
                           w["s5_d"][0], w["s5_w_glu"][0], w["s5_b_glu"][0], w["s5_w_out"][0])

    v_width = w["gdn_w_out"].shape[1]
    act, z2, bgc, hist8 = _gdn_in(x1, mod[1], w["norm_g"][1], w["gdn_w_in"][0], w["gdn_a_log"][0],
                                  w["gdn_dt_bias"][0], w["gdn_conv_w"][0], gdn_conv0[0], v_width)
    o, s_new = _gdn_core(act, bgc, gdn_s0[0])
    y = _gdn_out(o, z2, x1, mod[1], w["gdn_norm_g"][0], w["gdn_w_out"][0], w["final_g"])
    new_hist = hist8[:, HIST_ROWS - (GDN_CONV - 1):, :]
    return (y, hr.reshape(1, B, G, P), hi.reshape(1, B, G, P), s_new[None], new_hist[None])


def kernel(x_prompt, x_sample, c_prompt, c_sample, state_s5_re, state_s5_im, state_gdn, state_gdn_conv, norm_g, w_ada, b_ada, s5_w_in, s5_log_step, s5_lambda_re, s5_lambda_im, s5_b_re, s5_b_im, s5_c_re, s5_c_im, s5_d, s5_w_glu, s5_b_glu, s5_w_out, gdn_w_in, gdn_conv_w, gdn_a_log, gdn_dt_bias, gdn_norm_g, gdn_w_out, final_g):
    w = dict(norm_g=norm_g, s5_w_in=s5_w_in, s5_log_step=s5_log_step, s5_lambda_re=s5_lambda_re,
             s5_lambda_im=s5_lambda_im, s5_b_re=s5_b_re, s5_b_im=s5_b_im, s5_c_re=s5_c_re,
             s5_c_im=s5_c_im, s5_d=s5_d, s5_w_glu=s5_w_glu, s5_b_glu=s5_b_glu, s5_w_out=s5_w_out,
             gdn_w_in=gdn_w_in, gdn_conv_w=gdn_conv_w, gdn_a_log=gdn_a_log,
             gdn_dt_bias=gdn_dt_bias, gdn_norm_g=gdn_norm_g, gdn_w_out=gdn_w_out, final_g=final_g)
    bp, _, d = x_prompt.shape
    bs = x_sample.shape[0]
    depth = w_ada.shape[0]
    mod = _ada_mod(jnp.concatenate([c_prompt, c_sample], axis=0), w_ada, b_ada)
    mod = mod.reshape(depth, bp + bs, 3, d)

    z_s5 = jnp.zeros((state_s5_re.shape[0], bp) + state_s5_re.shape[2:], F32)
    z_gdn = jnp.zeros((state_gdn.shape[0], bp) + state_gdn.shape[2:], F32)
    z_conv = jnp.zeros((state_gdn_conv.shape[0], bp) + state_gdn_conv.shape[2:], F32)
    yp, s5r_p, s5i_p, gdn_p, conv_p = _trunk(x_prompt, mod[:, :bp], z_s5, z_s5, z_gdn, z_conv, w)
    ys, s5r_s, s5i_s, gdn_s, conv_s = _trunk(x_sample, mod[:, bp:], state_s5_re, state_s5_im,
                                             state_gdn, state_gdn_conv, w)
    return (yp, ys, s5r_p, s5i_p, gdn_p, conv_p, s5r_s, s5i_s, gdn_s, conv_s)
```
